```python
import math
import jax, jax.numpy as jnp
from jax import lax
import numpy as np

D_MODEL = 1024
BATCH = 32
SEQ = 256
DEPTH = 2
DEC_BATCH = 4
DEC_SEQ = 2048
PAST_LEN = 256

GRID_W = 64
Q_BLOCK = 128
ROPE_BASE = 10000.0
EPS = 1e-6
N_MIXERS = 4
GROUP_W = D_MODEL // N_MIXERS
D_MIX = N_MIXERS * GROUP_W
MLA_HEADS = 4
MLA_NOPE = 64
MLA_ROPE = 32
MLA_V = GROUP_W // MLA_HEADS
MLA_Q_RANK = D_MODEL // 4
MLA_KV_RANK = D_MODEL // 8
GQA_HEADS = 4
GQA_KV_HEADS = 2
GQA_HD = GROUP_W // GQA_HEADS
DIFF_HEADS = 4
DIFF_VD = GROUP_W // DIFF_HEADS
DIFF_HD = DIFF_VD // 2
HY_W = GROUP_W
HY_ORDER = 2
HY_CONV = 3
HY_EMB = 33
HY_BANDS = (HY_EMB - 1) // 2
HY_FFN = 64
HY_DECAY_MIN = 3.07
HY_DECAY_MAX = 15.35
SPLIT_SIZES = (MLA_Q_RANK, MLA_KV_RANK, MLA_ROPE, GROUP_W,
               GQA_HEADS * GQA_HD, GQA_KV_HEADS * GQA_HD, GQA_KV_HEADS * GQA_HD, GROUP_W,
               DIFF_HEADS * 2 * DIFF_HD, DIFF_HEADS * 2 * DIFF_HD, DIFF_HEADS * DIFF_VD, GROUP_W,
               3 * HY_W, GROUP_W)
P_IN = sum(SPLIT_SIZES)

kernel_name = 'hybrid_parallel_heads_diffusion_step'


def rmsnorm(x, g):
    xf = x.astype(jnp.float32)
    y = xf * lax.rsqrt(jnp.mean(xf * xf, axis=-1, keepdims=True) + EPS)
    return (y * g.astype(jnp.float32)).astype(x.dtype)


def grid_positions(n_tokens):
    rows = n_tokens // GRID_W
    row = jnp.repeat(jnp.arange(rows, dtype=jnp.float32), GRID_W)
    col = jnp.tile(jnp.arange(GRID_W, dtype=jnp.float32), rows)
    return row, col


def rope_1d(x, pos):
    m = x.shape[-1]
    inv = ROPE_BASE ** (-jnp.arange(0, m, 2, dtype=jnp.float32) / m)
    ang = pos[:, None] * inv[None, :]
    bshape = (1, x.shape[1]) + (1,) * (x.ndim - 3) + (m,)
    cos = jnp.concatenate([jnp.cos(ang), jnp.cos(ang)], axis=-1).reshape(bshape).astype(x.dtype)
    sin = jnp.concatenate([jnp.sin(ang), jnp.sin(ang)], axis=-1).reshape(bshape).astype(x.dtype)
    x1, x2 = jnp.split(x, 2, axis=-1)
    return x * cos + jnp.concatenate([-x2, x1], axis=-1) * sin


def axial_rope(x, row, col):
    half = x.shape[-1] // 2
    return jnp.concatenate([rope_1d(x[..., :half], row), rope_1d(x[..., half:], col)], axis=-1)


def over_query_blocks(fn, q):
    b, lq = q.shape[0], q.shape[1]
    nb = lq // Q_BLOCK
    qb = jnp.swapaxes(q.reshape((b, nb, Q_BLOCK) + q.shape[2:]), 0, 1)
    out = lax.map(fn, qb)
    return jnp.swapaxes(out, 0, 1).reshape((b, lq) + out.shape[3:])


def mha_blocks(q, k, v):
    scale = q.shape[-1] ** -0.5
    def blk(qb):
        s = jnp.einsum('bqhd,bkhd->bhqk', qb, k).astype(jnp.float32) * scale
        p = jax.nn.softmax(s, axis=-1).astype(v.dtype)
        return jnp.einsum('bhqk,bkhd->bqhd', p, v)
    return over_query_blocks(blk, q)


def gqa_blocks(q, k, v):
    scale = q.shape[-1] ** -0.5
    def blk(qb):
        s = jnp.einsum('bqgrd,bkgd->bgrqk', qb, k).astype(jnp.float32) * scale
        p = jax.nn.softmax(s, axis=-1).astype(v.dtype)
        return jnp.einsum('bgrqk,bkgd->bqgrd', p, v)
    return over_query_blocks(blk, q)


def diff_blocks(q, k, v, lam):
    scale = q.shape[-1] ** -0.5
    def blk(qb):
        s = jnp.einsum('bqhsd,bkhsd->bhsqk', qb, k).astype(jnp.float32) * scale
        p = jax.nn.softmax(s, axis=-1)
        a = (p[:, :, 0] - lam * p[:, :, 1]).astype(v.dtype)
        return jnp.einsum('bhqk,bkhd->bqhd', a, v)
    return over_query_blocks(blk, q)


def short_conv(u, w, b):
    n = u.shape[1]
    up = jnp.pad(u, ((0, 0), (1, 1), (0, 0)))
    return up[:, :n] * w[0] + up[:, 1:n + 1] * w[1] + up[:, 2:] * w[2] + b


def hyena_filters(n, lp):
    f32 = jnp.float32
    t = jnp.linspace(0.0, 1.0, n, dtype=f32)[:, None]
    w = 2.0 * math.pi * jnp.arange(n, dtype=f32)[:, None] / n
    f = jnp.linspace(1e-4, HY_BANDS - 1, HY_BANDS, dtype=f32)[None, :]
    feats = jnp.concatenate([t, jnp.cos(f * w), -jnp.sin(f * w)], axis=-1)
    freq = lp['hy_freq'].astype(f32)
    h = jnp.sin(freq[0] * (feats @ lp['hy_fw1'].astype(f32) + lp['hy_fb1'].astype(f32)))
    h = jnp.sin(freq[1] * (h @ lp['hy_fw2'].astype(f32) + lp['hy_fb2'].astype(f32)))
    h = (h @ lp['hy_fw3'].astype(f32)).reshape(n, HY_ORDER, 2, HY_W)
    decay = jnp.exp(-t[:, :, None, None] * jnp.abs(lp['hy_decay'].astype(f32)))
    return h * decay


def bidir_long_conv(u, h_fwd, h_bwd, bias):
    n = u.shape[1]
    filt = jnp.concatenate([h_fwd, jnp.zeros((1, h_fwd.shape[-1]), jnp.float32), h_bwd[1:][::-1]], axis=0)
    uf = u.astype(jnp.float32)
    spec = jnp.fft.rfft(uf, n=2 * n, axis=1) * jnp.fft.rfft(filt, n=2 * n, axis=0)[None]
    y = jnp.fft.irfft(spec, n=2 * n, axis=1)[:, :n]
    return (y + uf * bias.astype(jnp.float32)).astype(u.dtype)


def hyena(u, lp):
    n = u.shape[1]
    x1, x2, v = jnp.split(short_conv(u, lp['hy_conv_w'], lp['hy_conv_b']), 3, axis=-1)
    filt = hyena_filters(n, lp)
    z = v
    for o, g in enumerate((x1, x2)):
        z = g * bidir_long_conv(z, filt[:, o, 0], filt[:, o, 1], lp['hy_bias'][o])
    return z


def mixer(h, lp, layer_idx, ctx):
    b, n, _ = h.shape
    z = h @ lp['w_in']
    (m_cq, m_ckv, m_kr, m_gate, g_q, g_k, g_v, g_gate,
     d_q, d_k, d_v, d_gate, hy_in, hy_gate) = jnp.split(z, np.cumsum(SPLIT_SIZES)[:-1].tolist(), axis=-1)
    q_mla = (rmsnorm(m_cq, lp['mla_q_norm']) @ lp['mla_wq_b']).reshape(b, n, MLA_HEADS, MLA_NOPE + MLA_ROPE)
    q_nope, q_rope = q_mla[..., :MLA_NOPE], q_mla[..., MLA_NOPE:]
    ckv = rmsnorm(m_ckv, lp['mla_kv_norm'])
    kr = m_kr
    gq = rmsnorm(g_q.reshape(b, n, GQA_HEADS, GQA_HD), lp['gqa_q_norm'])
    gk = rmsnorm(g_k.reshape(b, n, GQA_KV_HEADS, GQA_HD), lp['gqa_k_norm'])
    gv = g_v.reshape(b, n, GQA_KV_HEADS, GQA_HD)
    dq = d_q.reshape(b, n, DIFF_HEADS, 2, DIFF_HD)
    dk = d_k.reshape(b, n, DIFF_HEADS, 2, DIFF_HD)
    dv = d_v.reshape(b, n, DIFF_HEADS, DIFF_VD)
    state = (ckv, kr, gk, gv, dk.reshape(b, n, DIFF_HEADS, 2 * DIFF_HD), dv)
    if ctx is None:
        k_ckv, k_kr, k_g, v_g, k_d, v_d = state
    else:
        row, col = grid_positions(n)
        q_rope = axial_rope(q_rope, row, col)
        gq = axial_rope(gq, row, col)
        dq = axial_rope(dq, row, col)
        lat = (ckv, axial_rope(kr, row, col), axial_rope(gk, row, col), gv,
               axial_rope(dk, row, col).reshape(b, n, DIFF_HEADS, 2 * DIFF_HD), dv)
        k_ckv, k_kr, k_g, v_g, k_d, v_d = [jnp.concatenate([cs.astype(ls.dtype), ls], axis=1)
                                           for cs, ls in zip(ctx, lat)]
    kv = (k_ckv @ lp['mla_wkv_b']).reshape(b, -1, MLA_HEADS, MLA_NOPE + MLA_V)
    k_mla = jnp.concatenate([kv[..., :MLA_NOPE],
                             jnp.broadcast_to(k_kr[:, :, None, :], k_kr.shape[:2] + (MLA_HEADS, MLA_ROPE))], axis=-1)
    o_mla = mha_blocks(jnp.concatenate([q_nope, q_rope], axis=-1), k_mla, kv[..., MLA_NOPE:]).reshape(b, n, GROUP_W)
    o_gqa = gqa_blocks(gq.reshape(b, n, GQA_KV_HEADS, GQA_HEADS // GQA_KV_HEADS, GQA_HD), k_g, v_g).reshape(b, n, GROUP_W)
    lam_init = 0.8 - 0.6 * math.exp(-0.3 * layer_idx)
    f32 = jnp.float32
    lam = (jnp.exp(jnp.sum(lp['diff_lq1'].astype(f32) * lp['diff_lk1'].astype(f32)))
           - jnp.exp(jnp.sum(lp['diff_lq2'].astype(f32) * lp['diff_lk2'].astype(f32))) + lam_init)
    o_diff = diff_blocks(dq, k_d.reshape(b, -1, DIFF_HEADS, 2, DIFF_HD), v_d, lam)
    o_diff = (rmsnorm(o_diff, lp['diff_subln']) * (1.0 - lam_init)).reshape(b, n, GROUP_W)
    o_hy = hyena(hy_in, lp)
    o = jnp.concatenate([o_mla * jax.nn.silu(m_gate), o_gqa * jax.nn.silu(g_gate),
                         o_diff * jax.nn.silu(d_gate), o_hy * jax.nn.silu(hy_gate)], axis=-1)
    return o @ lp['w_out'], state


def trunk_layer(x, mod, lp, layer_idx, ctx):
    shift, scale, gate = jnp.split(mod[:, None, :], 3, axis=-1)
    h = rmsnorm(x, lp['norm_pre']) * (1.0 + scale) + shift
    out, state = mixer(h, lp, layer_idx, ctx)
    return x + gate * rmsnorm(out, lp['norm_post']), state


def setup_inputs(seed: int = 0) -> dict:
    key = jax.random.key(seed)
    ks = jax.random.split(key, 40)
    f32 = jnp.float32

    def nrm(i, shape, scale):
        return jax.random.normal(ks[i], shape, f32) * scale

    def gain(i, shape):
        return 1.0 + nrm(i, shape, 0.1)

    decay = jnp.broadcast_to(jnp.linspace(HY_DECAY_MIN, HY_DECAY_MAX, HY_W, dtype=f32), (DEPTH, HY_W)) + nrm(35, (DEPTH, HY_W), 0.1)
    return {
        'x_prompt': nrm(0, (BATCH, SEQ, D_MODEL), 1.0),
        'x_sample': nrm(1, (DEC_BATCH, DEC_SEQ, D_MODEL), 1.0),
        'cache_mla_ckv': nrm(2, (DEC_BATCH, DEPTH, PAST_LEN, MLA_KV_RANK), 1.0),
        'cache_mla_krope': nrm(3, (DEC_BATCH, DEPTH, PAST_LEN, MLA_ROPE), 1.0),
        'cache_gqa_k': nrm(4, (DEC_BATCH, DEPTH, PAST_LEN, GQA_KV_HEADS, GQA_HD), 1.0),
        'cache_gqa_v': nrm(5, (DEC_BATCH, DEPTH, PAST_LEN, GQA_KV_HEADS, GQA_HD), 1.0),
        'cache_diff_k': nrm(6, (DEC_BATCH, DEPTH, PAST_LEN, DIFF_HEADS, 2 * DIFF_HD), 1.0),
        'cache_diff_v': nrm(7, (DEC_BATCH, DEPTH, PAST_LEN, DIFF_HEADS, DIFF_VD), 1.0),
        'c': nrm(8, (DEC_BATCH, D_MODEL), 1.0),
        'c_ctx': nrm(9, (D_MODEL,), 1.0),
        'norm_pre': gain(10, (DEPTH, D_MODEL)),
        'norm_post': gain(11, (DEPTH, D_MODEL)),
        'ada_w': nrm(12, (DEPTH, D_MODEL, 3 * D_MODEL), D_MODEL ** -0.5),
        'ada_b': nrm(13, (DEPTH, 3 * D_MODEL), 0.02),
        'w_in': nrm(14, (DEPTH, D_MODEL, P_IN), D_MODEL ** -0.5),
        'w_out': nrm(15, (DEPTH, D_MIX, D_MODEL), D_MIX ** -0.5),
        'mla_q_norm': gain(16, (DEPTH, MLA_Q_RANK)),
        'mla_wq_b': nrm(17, (DEPTH, MLA_Q_RANK, MLA_HEADS * (MLA_NOPE + MLA_ROPE)), MLA_Q_RANK ** -0.5),
        'mla_kv_norm': gain(18, (DEPTH, MLA_KV_RANK)),
        'mla_wkv_b': nrm(19, (DEPTH, MLA_KV_RANK, MLA_HEADS * (MLA_NOPE + MLA_V)), MLA_KV_RANK ** -0.5),
        'gqa_q_norm': gain(20, (DEPTH, GQA_HD)),
        'gqa_k_norm': gain(21, (DEPTH, GQA_HD)),
        'diff_lq1': nrm(22, (DEPTH, DIFF_HD), 0.1),
        'diff_lk1': nrm(23, (DEPTH, DIFF_HD), 0.1),
        'diff_lq2': nrm(24, (DEPTH, DIFF_HD), 0.1),
        'diff_lk2': nrm(25, (DEPTH, DIFF_HD), 0.1),
        'diff_subln': gain(26, (DEPTH, DIFF_VD)),
        'hy_conv_w': nrm(27, (DEPTH, HY_CONV, 3 * HY_W), HY_CONV ** -0.5),
        'hy_conv_b': nrm(28, (DEPTH, 3 * HY_W), 0.02),
        'hy_fw1': nrm(29, (DEPTH, HY_EMB, HY_FFN), HY_EMB ** -0.5),
        'hy_fb1': nrm(30, (DEPTH, HY_FFN), 0.02),
        'hy_fw2': nrm(31, (DEPTH, HY_FFN, HY_FFN), HY_FFN ** -0.5),
        'hy_fb2': nrm(32, (DEPTH, HY_FFN), 0.02),
        'hy_fw3': nrm(33, (DEPTH, HY_FFN, HY_ORDER * 2 * HY_W), 0.1 * HY_FFN ** -0.5),
        'hy_freq': gain(34, (DEPTH, 2, HY_FFN)),
        'hy_decay': decay,
        'hy_bias': nrm(36, (DEPTH, HY_ORDER, HY_W), 0.5),
    }


def reference(x_prompt, x_sample, cache_mla_ckv, cache_mla_krope, cache_gqa_k, cache_gqa_v,
              cache_diff_k, cache_diff_v, c, c_ctx, norm_pre, norm_post, ada_w, ada_b, w_in, w_out,
              mla_q_norm, mla_wq_b, mla_kv_norm, mla_wkv_b, gqa_q_norm, gqa_k_norm,
              diff_lq1, diff_lk1, diff_lq2, diff_lk2, diff_subln, hy_conv_w, hy_conv_b,
              hy_fw1, hy_fb1, hy_fw2, hy_fb2, hy_fw3, hy_freq, hy_decay, hy_bias):
    y_prompt = x_prompt
    y_sample = x_sample
    s_ckv, s_kr, s_gk, s_gv, s_dk, s_dv = [], [], [], [], [], []
    for i in range(DEPTH):
        lp = dict(norm_pre=norm_pre[i], norm_post=norm_post[i], w_in=w_in[i], w_out=w_out[i],
                  mla_q_norm=mla_q_norm[i], mla_wq_b=mla_wq_b[i], mla_kv_norm=mla_kv_norm[i],
                  mla_wkv_b=mla_wkv_b[i], gqa_q_norm=gqa_q_norm[i], gqa_k_norm=gqa_k_norm[i],
                  diff_lq1=diff_lq1[i], diff_lk1=diff_lk1[i], diff_lq2=diff_lq2[i], diff_lk2=diff_lk2[i],
                  diff_subln=diff_subln[i], hy_conv_w=hy_conv_w[i], hy_conv_b=hy_conv_b[i],
                  hy_fw1=hy_fw1[i], hy_fb1=hy_fb1[i], hy_fw2=hy_fw2[i], hy_fb2=hy_fb2[i],
                  hy_fw3=hy_fw3[i], hy_freq=hy_freq[i], hy_decay=hy_decay[i], hy_bias=hy_bias[i])
        mod_ctx = (jax.nn.silu(c_ctx) @ ada_w[i] + ada_b[i])[None, :]
        mod_lat = jax.nn.silu(c) @ ada_w[i] + ada_b[i]
        y_prompt, st = trunk_layer(y_prompt, mod_ctx, lp, i, None)
        s_ckv.append(st[0]); s_kr.append(st[1]); s_gk.append(st[2])
        s_gv.append(st[3]); s_dk.append(st[4]); s_dv.append(st[5])
        ctx = (cache_mla_ckv[:, i], cache_mla_krope[:, i], cache_gqa_k[:, i], cache_gqa_v[:, i],
               cache_diff_k[:, i], cache_diff_v[:, i])
        y_sample, _ = trunk_layer(y_sample, mod_lat, lp, i, ctx)
    new_mla_ckv = jnp.stack(s_ckv, axis=1)
    new_mla_krope = jnp.stack(s_kr, axis=1)
    new_gqa_k = jnp.stack(s_gk, axis=1)
    new_gqa_v = jnp.stack(s_gv, axis=1)
    new_diff_k = jnp.stack(s_dk, axis=1)
    new_diff_v = jnp.stack(s_dv, axis=1)
    return (y_prompt, y_sample, new_mla_ckv, new_mla_krope, new_gqa_k, new_gqa_v, new_diff_k, new_diff_v)
```

```python
import functools
import math

import jax
import jax.numpy as jnp
from jax import lax
from jax.experimental import pallas as pl
from jax.experimental.pallas import tpu as pltpu

F32 = jnp.float32
BF16 = jnp.bfloat16

D_MODEL = 1024
DEPTH = 2
GRID_W = 64
ROPE_BASE = 10000.0
EPS = 1e-6
GROUP_W = 256
MLA_HEADS = 4
MLA_NOPE = 64
MLA_ROPE = 32
MLA_V = 64
MLA_Q_RANK = 256
MLA_KV_RANK = 128
GQA_HEADS = 4
GQA_KV_HEADS = 2
GQA_HD = 64
DIFF_HEADS = 4
DIFF_VD = 64
DIFF_HD = 32
HY_W = 256
HY_EMB = 33
HY_BANDS = 16
HY_FFN = 64
SPLIT_SIZES = (256, 128, 32, 256, 256, 128, 128, 256, 256, 256, 256, 256, 768, 256)
P_IN = sum(SPLIT_SIZES)

LANES = 128
MXU_W = 256
P_PAD = 14 * MXU_W
N_QBLK = 4
N_KVBLK = 6
HY_CHUNK = 512
HY_HALO = 16
NT_DIMS = (((1,), (1,)), ((), ()))
VMEM_LIMIT = 56 * 1024 * 1024


def _dot(a, b):
    return jnp.dot(a, b, preferred_element_type=F32)


def _dot_nt(a, b):
    return lax.dot_general(a, b, NT_DIMS, preferred_element_type=F32)


def _rms_full(v, g):
    ms = jnp.mean(v * v, axis=-1, keepdims=True)
    return v * lax.rsqrt(ms + EPS) * g


def _rms_heads64(v, g):
    outs = []
    lane = lax.broadcasted_iota(jnp.int32, (v.shape[0], LANES), 1)
    lo = lane < 64
    for b in range(v.shape[1] // LANES):
        vb = v[:, b * LANES:(b + 1) * LANES]
        v2 = vb * vb
        s_lo = jnp.sum(jnp.where(lo, v2, 0.0), axis=-1, keepdims=True)
        s_hi = jnp.sum(jnp.where(lo, 0.0, v2), axis=-1, keepdims=True)
        r = jnp.where(lo, lax.rsqrt(s_lo * (1.0 / 64) + EPS), lax.rsqrt(s_hi * (1.0 / 64) + EPS))
        outs.append(vb * r)
    out = outs[0] if len(outs) == 1 else jnp.concatenate(outs, axis=1)
    return out * g


def _rope(v, cos_t, sin_t, half):
    lane = lax.broadcasted_iota(jnp.int32, (v.shape[0], LANES), 1)
    first = (lane % (2 * half)) < half
    outs = []
    for b in range(v.shape[1] // LANES):
        vb = v[:, b * LANES:(b + 1) * LANES]
        up = pltpu.roll(vb, LANES - half, axis=1)
        dn = pltpu.roll(vb, half, axis=1)
        outs.append(vb * cos_t + jnp.where(first, up, dn) * sin_t)
    return outs[0] if len(outs) == 1 else jnp.concatenate(outs, axis=1)


def _ada_kernel(cc_ref, w_ref, b_ref, o_ref):
    a = cc_ref[...]
    a = a * jax.nn.sigmoid(a)
    o_ref[0] = _dot(a.astype(BF16), w_ref[0].astype(BF16)) + b_ref[0]


def _ada_call(cc8, ada_w, ada_b):
    tn = 768
    return pl.pallas_call(
        _ada_kernel,
        grid=(DEPTH, 3 * D_MODEL // tn),
        in_specs=[
            pl.BlockSpec((8, D_MODEL), lambda l, j: (0, 0)),
            pl.BlockSpec((1, D_MODEL, tn), lambda l, j: (l, 0, j)),
            pl.BlockSpec((1, 1, tn), lambda l, j: (l, 0, j)),
        ],
        out_specs=pl.BlockSpec((1, 8, tn), lambda l, j: (l, 0, j)),
        out_shape=jax.ShapeDtypeStruct((DEPTH, 8, 3 * D_MODEL), F32),
        compiler_params=pltpu.CompilerParams(dimension_semantics=("parallel", "parallel")),
        name="ada_mod",
    )(cc8, ada_w, ada_b.reshape(DEPTH, 1, 3 * D_MODEL))


def _ctx_kernel(ckv_ref, kr4_ref, gk_ref, gv_ref, dk_ref, dv_ref, wkvb_ref, o_ref):
    kvp = _dot(ckv_ref[0, 0].astype(BF16), wkvb_ref[0])
    kr4 = kr4_ref[0, 0].astype(BF16)
    o_ref[0, 0, 0] = jnp.concatenate([kvp[:, 0:128].astype(BF16), kr4], axis=1)
    o_ref[0, 0, 1] = jnp.concatenate([kvp[:, 128:256].astype(BF16), kr4], axis=1)
    o_ref[0, 0, 2] = kvp[:, 256:512].astype(BF16)
    o_ref[0, 0, 3] = jnp.concatenate([gk_ref[0, 0].astype(BF16), gv_ref[0, 0].astype(BF16)], axis=1)
    o_ref[0, 0, 4] = dk_ref[0, 0].astype(BF16)
    o_ref[0, 0, 5] = dv_ref[0, 0].astype(BF16)


def _ctx_call(ckv, kr4, gk, gv, dk, dv, wkvb):
    nb, _, p, _ = ckv.shape

    def spec(w):
        return pl.BlockSpec((1, 1, p, w), lambda l, b: (b, l, 0, 0))

    return pl.pallas_call(
        _ctx_kernel,
        grid=(DEPTH, nb),
        in_specs=[spec(128), spec(128), spec(128), spec(128), spec(256), spec(256),
                  pl.BlockSpec((1, MLA_KV_RANK, 512), lambda l, b: (l, 0, 0))],
        out_specs=pl.BlockSpec((1, 1, N_KVBLK, p, MXU_W), lambda l, b: (l, b, 0, 0, 0)),
        out_shape=jax.ShapeDtypeStruct((DEPTH, nb, N_KVBLK, p, MXU_W), BF16),
        compiler_params=pltpu.CompilerParams(dimension_semantics=("parallel", "parallel")),
        name="ctx_kv",
    )(ckv, kr4, gk, gv, dk, dv, wkvb)


def _inproj_kernel(*refs, rope, states):
    it = iter(refs)
    x_ref, mod_ref, gpre_ref, w_ref, wqb_ref, wkvb_ref = (next(it) for _ in range(6))
    gq_mla_ref, gkv_mla_ref, gqn_ref, gkn_ref = (next(it) for _ in range(4))
    if rope:
        cos64_ref, sin64_ref, cos32_ref, sin32_ref = (next(it) for _ in range(4))
    q_ref, kv_ref, hy_ref, gate_ref = (next(it) for _ in range(4))
    if states:
        s_ckv_ref, s_kr_ref, s_gk_ref, s_gv_ref, s_dk_ref, s_dv_ref = (next(it) for _ in range(6))
    h_scr = next(it)

    x = x_ref[0]
    mod = mod_ref[0]
    shift = mod[:, 0:D_MODEL]
    scale = mod[:, D_MODEL:2 * D_MODEL]
    ms = jnp.mean(x * x, axis=-1, keepdims=True)
    h = (x * lax.rsqrt(ms + EPS) * gpre_ref[0]) * (1.0 + scale) + shift
    h_scr[...] = h.astype(BF16)

    def proj(g, n=1):
        return _dot(h_scr[...], w_ref[0, :, g * MXU_W:(g + n) * MXU_W])

    def rope32(v):
        return _rope(v, cos32_ref[...], sin32_ref[...], 8) if rope else v

    def rope64(v):
        return _rope(v, cos64_ref[...], sin64_ref[...], 16) if rope else v

    cq = _rms_full(proj(0), gq_mla_ref[0])
    q3 = _dot(cq.astype(BF16), wqb_ref[0]) * ((MLA_NOPE + MLA_ROPE) ** -0.5)
    qr = rope32(q3[:, 256:384]).astype(BF16)
    q_ref[0, 0] = jnp.concatenate([q3[:, 0:128].astype(BF16), qr], axis=1)
    q_ref[0, 1] = jnp.concatenate([q3[:, 128:256].astype(BF16), qr], axis=1)

    z1 = proj(1)
    ckv = _rms_full(z1[:, 0:128], gkv_mla_ref[0])
    kr4 = z1[:, 128:256]
    if states:
        s_ckv_ref[0] = ckv
        s_kr_ref[0] = kr4[:, 0:MLA_ROPE]
    kvp = _dot(ckv.astype(BF16), wkvb_ref[0])
    kr4 = rope32(kr4).astype(BF16)
    kv_ref[0, 0] = jnp.concatenate([kvp[:, 0:128].astype(BF16), kr4], axis=1)
    kv_ref[0, 1] = jnp.concatenate([kvp[:, 128:256].astype(BF16), kr4], axis=1)
    kv_ref[0, 2] = kvp[:, 256:512].astype(BF16)

    gq = rope64(_rms_heads64(proj(2), gqn_ref[0])) * (GQA_HD ** -0.5)
    q_ref[0, 2] = gq.astype(BF16)
    z3 = proj(3)
    gk = _rms_heads64(z3[:, 0:128], gkn_ref[0])
    gv = z3[:, 128:256]
    if states:
        s_gk_ref[0] = gk
        s_gv_ref[0] = gv
    kv_ref[0, 3] = jnp.concatenate([rope64(gk).astype(BF16), gv.astype(BF16)], axis=1)

    q_ref[0, 3] = (rope32(proj(4)) * (DIFF_HD ** -0.5)).astype(BF16)
    dk = proj(5)
    dv = proj(6)
    if states:
        s_dk_ref[0] = dk
        s_dv_ref[0] = dv
    kv_ref[0, 4] = rope32(dk).astype(BF16)
    kv_ref[0, 5] = dv.astype(BF16)

    for j in range(3):
        hy_ref[0, :, j * MXU_W:(j + 1) * MXU_W] = proj(7 + j).astype(BF16)
    for j in range(4):
        g = proj(10 + j)
        gate_ref[0, :, j * MXU_W:(j + 1) * MXU_W] = (g * jax.nn.sigmoid(g)).astype(BF16)


def _inproj_call(layer, x, mod, prep, tabs, *, rope, states, tm):
    nb, n, _ = x.shape
    per_batch_mod = mod.shape[0] > 1
    tiles = n // tm

    def lspec(shape):
        nd = len(shape)
        return pl.BlockSpec((1,) + tuple(shape[1:]), lambda b, i: (layer,) + (0,) * (nd - 1))

    in_specs = [
        pl.BlockSpec((1, tm, D_MODEL), lambda b, i: (b, i, 0)),
        pl.BlockSpec((1, 1, 3 * D_MODEL), (lambda b, i: (b, 0, 0)) if per_batch_mod else (lambda b, i: (0, 0, 0))),
    ]
    args = [x, mod]
    for name in ("norm_pre", "w_in", "wqb", "wkvb", "mla_q_norm", "mla_kv_norm", "gqa_q_norm", "gqa_k_norm"):
        a = prep[name]
        in_specs.append(lspec(a.shape))
        args.append(a)
    if rope:
        for t in tabs:
            in_specs.append(pl.BlockSpec((tm, LANES), lambda b, i: (i, 0)))
            args.append(t)

    out_shape = [
        jax.ShapeDtypeStruct((nb, N_QBLK, n, MXU_W), BF16),
        jax.ShapeDtypeStruct((nb, N_KVBLK, n, MXU_W), BF16),
        jax.ShapeDtypeStruct((nb, n, 3 * HY_W), BF16),
        jax.ShapeDtypeStruct((nb, n, 4 * GROUP_W), BF16),
    ]
    out_specs = [
        pl.BlockSpec((1, N_QBLK, tm, MXU_W), lambda b, i: (b, 0, i, 0)),
        pl.BlockSpec((1, N_KVBLK, tm, MXU_W), lambda b, i: (b, 0, i, 0)),
        pl.BlockSpec((1, tm, 3 * HY_W), lambda b, i: (b, i, 0)),
        pl.BlockSpec((1, tm, 4 * GROUP_W), lambda b, i: (b, i, 0)),
    ]
    if states:
        for w in (MLA_KV_RANK, MLA_ROPE, 128, 128, 256, 256):
            out_shape.append(jax.ShapeDtypeStruct((nb, n, w), F32))
            out_specs.append(pl.BlockSpec((1, tm, w), lambda b, i: (b, i, 0)))

    return pl.pallas_call(
        functools.partial(_inproj_kernel, rope=rope, states=states),
        grid=(nb, tiles),
        in_specs=in_specs,
        out_specs=out_specs,
        out_shape=out_shape,
        scratch_shapes=[pltpu.VMEM((tm, D_MODEL), BF16)],
        compiler_params=pltpu.CompilerParams(
            dimension_semantics=("parallel", "parallel"), vmem_limit_bytes=VMEM_LIMIT),
        name="inproj_rope" if rope else "inproj_ctx",
    )(*args)


def _chunk_loop(nchunk, body):
    if nchunk == 1:
        body(0)
    else:
        def wrapped(k, carry):
            body(k)
            return carry
        lax.fori_loop(0, nchunk, wrapped, 0)


def _hyena_kernel(hy_ref, gate_ref, cw_ref, cb_ref, feats_ref, fw1_ref, fb1_ref, fw2_ref, fb2_ref,
                  fw3_ref, freq_ref, decay_ref, bias_ref, cs_ref, ss_ref,
                  o_ref, fr_scr, fi_scr, work, hy_pad, *, n, bt):
    hp = lax.Precision.HIGHEST
    rc = min(n, HY_CHUNK)
    nchunk = n // rc
    rowc = lax.broadcasted_iota(jnp.int32, (rc, HY_W), 0)

    def rows(k):
        if nchunk == 1:
            return slice(0, rc)
        return pl.ds(pl.multiple_of(k * rc, rc), rc)

    @pl.when(pl.program_id(0) == 0)
    def _filters():
        freq = freq_ref[0]
        bias = bias_ref[0]

        def taps(k):
            ft = feats_ref[rows(k), :]
            t = ft[:, 0:1]
            h1 = jnp.sin(freq[0:1, :] * (jnp.dot(ft, fw1_ref[0], precision=hp, preferred_element_type=F32) + fb1_ref[0]))
            h2 = jnp.sin(freq[1:2, :] * (jnp.dot(h1, fw2_ref[0], precision=hp, preferred_element_type=F32) + fb2_ref[0]))
            h3 = jnp.dot(h2, fw3_ref[0], precision=hp, preferred_element_type=F32)
            dec = jnp.exp(-t * jnp.abs(decay_ref[0]))
            lag0 = (rowc + k * rc) == 0
            for o in range(2):
                hf = h3[:, o * 512:o * 512 + 256] * dec
                hb = h3[:, o * 512 + 256:o * 512 + 512] * dec
                hf = hf + jnp.where(lag0, bias[o:o + 1, :], 0.0)
                hb = jnp.where(lag0, 0.0, hb)
                work[rows(k), o * 512:o * 512 + 256] = hf.astype(BF16)
                work[rows(k), o * 512 + 256:o * 512 + 512] = hb.astype(BF16)

        _chunk_loop(nchunk, taps)

        def spectra(k):
            ft = feats_ref[rows(k), :]
            c = ft[:, HY_EMB + 7:HY_EMB + 8]
            s = ft[:, HY_EMB + 8:HY_EMB + 9]
            p = _dot(cs_ref[rows(k), :], work[...])
            q = _dot(ss_ref[rows(k), :], work[...])
            for o in range(2):
                pf, pb = p[:, o * 512:o * 512 + 256], p[:, o * 512 + 256:o * 512 + 512]
                qf, qb = q[:, o * 512:o * 512 + 256], q[:, o * 512 + 256:o * 512 + 512]
                fr_scr[o, rows(k), :] = (c * (pf + pb) + s * (qf + qb)) * (1.0 / n)
                fi_scr[o, rows(k), :] = (s * (pf - pb) - c * (qf - qb)) * (1.0 / n)

        _chunk_loop(nchunk, spectra)

    cw = cw_ref[0]
    cb = cb_ref[0]
    hy_pad[0:HY_HALO, :] = jnp.zeros((HY_HALO, 3 * HY_W), BF16)
    hy_pad[HY_HALO + n:2 * HY_HALO + n, :] = jnp.zeros((HY_HALO, 3 * HY_W), BF16)

    def short_conv(k, j):
        sl = slice(j * HY_W, (j + 1) * HY_W)
        if nchunk == 1:
            ext = hy_pad[:, sl].astype(F32)
        else:
            ext = hy_pad[pl.ds(pl.multiple_of(k * rc, rc), rc + 2 * HY_HALO), sl].astype(F32)
        um = pltpu.roll(ext, 1, axis=0)[HY_HALO:HY_HALO + rc]
        up = pltpu.roll(ext, rc + 2 * HY_HALO - 1, axis=0)[HY_HALO:HY_HALO + rc]
        u = ext[HY_HALO:HY_HALO + rc]
        return um * cw[0:1, sl] + u * cw[1:2, sl] + up * cw[2:3, sl] + cb[:, sl]

    zc, yrc, yic = slice(0, 256), slice(256, 512), slice(512, 768)

    def one(bi):
        hy_pad[HY_HALO:HY_HALO + n, :] = hy_ref[bi]

        def conv_v(k):
            work[rows(k), zc] = short_conv(k, 2).astype(BF16)

        _chunk_loop(nchunk, conv_v)
        for o in range(2):
            def forward(k, o=o):
                a = _dot(cs_ref[rows(k), :], work[:, zc])
                b = _dot(ss_ref[rows(k), :], work[:, zc])
                fr = fr_scr[o, rows(k), :]
                fi = fi_scr[o, rows(k), :]
                work[rows(k), yrc] = (a * fr + b * fi).astype(BF16)
                work[rows(k), yic] = (a * fi - b * fr).astype(BF16)

            def inverse(k, o=o):
                y = _dot(cs_ref[rows(k), :], work[:, yrc]) - _dot(ss_ref[rows(k), :], work[:, yic])
                z = short_conv(k, o) * y
                if o == 0:
                    work[rows(k), zc] = z.astype(BF16)
                else:
                    o_ref[bi, rows(k), :] = (z * gate_ref[bi, rows(k), :].astype(F32)).astype(BF16)

            _chunk_loop(nchunk, forward)
            _chunk_loop(nchunk, inverse)

    if bt == 1:
        one(0)
    else:
        def body(bi, carry):
            one(bi)
            return carry
        lax.fori_loop(0, bt, body, 0)


def _hyena_call(layer, hy, gates, prep, dft, *, bt):
    nb, n, _ = hy.shape
    cs, ss, feats = dft

    def lspec(shape):
        nd = len(shape)
        return pl.BlockSpec((1,) + tuple(shape[1:]), lambda b: (layer,) + (0,) * (nd - 1))

    def cspec(shape):
        nd = len(shape)
        return pl.BlockSpec(tuple(shape), lambda b: (0,) * nd, pipeline_mode=pl.Buffered(1))

    big = n >= 1024
    in_specs = [
        pl.BlockSpec((bt, n, 3 * HY_W), lambda b: (b, 0, 0)),
        pl.BlockSpec((bt, n, GROUP_W), lambda b: (b, 0, 3)),
    ]
    args = [hy, gates]
    for name in ("hy_conv_w", "hy_conv_b"):
        in_specs.append(lspec(prep[name].shape))
        args.append(prep[name])
    in_specs.append(cspec(feats.shape))
    args.append(feats)
    for name in ("hy_fw1", "hy_fb1", "hy_fw2", "hy_fb2", "hy_fw3", "hy_freq", "hy_decay", "hy_bias"):
        in_specs.append(lspec(prep[name].shape))
        args.append(prep[name])
    in_specs += [cspec(cs.shape), cspec(ss.shape)]
    args += [cs, ss]

    return pl.pallas_call(
        functools.partial(_hyena_kernel, n=n, bt=bt),
        grid=(nb // bt,),
        in_specs=in_specs,
        out_specs=pl.BlockSpec((bt, n, HY_W), lambda b: (b, 0, 0)),
        out_shape=jax.ShapeDtypeStruct((nb, n, HY_W), BF16),
        scratch_shapes=[pltpu.VMEM((2, n, HY_W), F32), pltpu.VMEM((2, n, HY_W), F32),
                        pltpu.VMEM((n, 4 * HY_W), BF16),
                        pltpu.VMEM((n + 2 * HY_HALO, 3 * HY_W), BF16)],
        compiler_params=pltpu.CompilerParams(
            dimension_semantics=("arbitrary",), vmem_limit_bytes=VMEM_LIMIT),
        name="hyena_long" if big else "hyena_short",
    )(*args)


def _attn_kernel(*refs, has_ctx, lam_init, tq):
    it = iter(refs)
    q_ref, kv_ref = next(it), next(it)
    ctx_ref = next(it) if has_ctx else None
    gate_ref, ohy_ref, x_ref, mod_ref, gpost_ref, wout_ref, subln_ref, lamp_ref = (next(it) for _ in range(8))
    y_ref = next(it)
    o_scr = next(it)

    lane = lax.broadcasted_iota(jnp.int32, (tq, MXU_W), 1)

    def band(lo, width):
        return (lane >= lo) & (lane < lo + width)

    def softmax_parts(qm, kblk):
        s_l = _dot_nt(qm, kv_ref[0, kblk])
        m = jnp.max(s_l, axis=-1, keepdims=True)
        if has_ctx:
            s_c = _dot_nt(qm, ctx_ref[0, kblk])
            m = jnp.maximum(m, jnp.max(s_c, axis=-1, keepdims=True))
            p_c = jnp.exp(s_c - m)
        p_l = jnp.exp(s_l - m)
        l = jnp.sum(p_l, axis=-1, keepdims=True)
        if has_ctx:
            l = l + jnp.sum(p_c, axis=-1, keepdims=True)
            return p_l, p_c, l
        return p_l, None, l

    def pv(p_l, p_c, vblk):
        o = _dot(p_l.astype(BF16), kv_ref[0, vblk])
        if has_ctx:
            o = o + _dot(p_c.astype(BF16), ctx_ref[0, vblk])
        return o

    o_scr[...] = jnp.zeros((tq, 3 * GROUP_W), F32)

    def mla_body(h, carry):
        blk = h // 2
        q = q_ref[0, blk].astype(F32)
        msk = band(64 * (h % 2), MLA_NOPE) | band(128 + MLA_ROPE * h, MLA_ROPE)
        qm = jnp.where(msk, q, 0.0).astype(BF16)
        p_l, p_c, l = softmax_parts(qm, blk)
        o = pv(p_l, p_c, 2) / l
        o_scr[:, 0:256] = jnp.where(band(MLA_V * h, MLA_V), o, o_scr[:, 0:256])
        return carry

    lax.fori_loop(0, MLA_HEADS, mla_body, 0)

    def gqa_body(hq, carry):
        j = hq % 2
        g = hq // 2
        q = q_ref[0, 2].astype(F32)
        qsel = jnp.where(band(128 * j + GQA_HD * g, GQA_HD), q, 0.0)
        q128 = qsel[:, 0:128] + qsel[:, 128:256]
        qm = jnp.concatenate([q128, jnp.zeros_like(q128)], axis=1).astype(BF16)
        p_l, p_c, l = softmax_parts(qm, 3)
        o = pv(p_l, p_c, 3) / l
        part = o[:, 128:256]
        part = jnp.concatenate([part, part], axis=1)
        o_scr[:, 256:512] = jnp.where(band(128 * j + GQA_HD * g, GQA_HD), part, o_scr[:, 256:512])
        return carry

    lax.fori_loop(0, GQA_HEADS, gqa_body, 0)

    lamp = lamp_ref[0]
    lam = (jnp.exp(jnp.sum(lamp[0:1, :] * lamp[1:2, :], axis=-1, keepdims=True))
           - jnp.exp(jnp.sum(lamp[2:3, :] * lamp[3:4, :], axis=-1, keepdims=True)) + lam_init)

    def diff_body(h, carry):
        q = q_ref[0, 3].astype(F32)
        qm1 = jnp.where(band(64 * h, DIFF_HD), q, 0.0).astype(BF16)
        qm2 = jnp.where(band(64 * h + DIFF_HD, DIFF_HD), q, 0.0).astype(BF16)
        p1_l, p1_c, l1 = softmax_parts(qm1, 4)
        p2_l, p2_c, l2 = softmax_parts(qm2, 4)
        w1 = 1.0 / l1
        w2 = lam / l2
        a_l = p1_l * w1 - p2_l * w2
        a_c = (p1_c * w1 - p2_c * w2) if has_ctx else None
        o = pv(a_l, a_c, 5)
        o_scr[:, 512:768] = jnp.where(band(DIFF_VD * h, DIFF_VD), o, o_scr[:, 512:768])
        return carry

    lax.fori_loop(0, DIFF_HEADS, diff_body, 0)

    od = o_scr[:, 512:768]
    od2 = od * od
    r = jnp.zeros_like(od)
    for h in range(DIFF_HEADS):
        mh = band(DIFF_VD * h, DIFF_VD)
        sh = jnp.sum(jnp.where(mh, od2, 0.0), axis=-1, keepdims=True)
        r = jnp.where(mh, lax.rsqrt(sh * (1.0 / DIFF_VD) + EPS), r)
    od = (od * r * subln_ref[0]) * (1.0 - lam_init)

    g = gate_ref[0].astype(F32)
    ob = jnp.concatenate([
        (o_scr[:, 0:512] * g[:, 0:512]).astype(BF16),
        (od * g[:, 512:768]).astype(BF16),
        ohy_ref[0],
    ], axis=1)
    out = _dot(ob, wout_ref[0])
    ms = jnp.mean(out * out, axis=-1, keepdims=True)
    gate_mod = mod_ref[0][:, 2 * D_MODEL:3 * D_MODEL]
    y_ref[0] = x_ref[0] + gate_mod * (out * lax.rsqrt(ms + EPS) * gpost_ref[0])


def _attn_call(layer, q, kv, ctx, gates, ohy, x, mod, prep, *, tq):
    nb, n, _ = x.shape
    has_ctx = ctx is not None
    per_batch_mod = mod.shape[0] > 1
    lam_init = 0.8 - 0.6 * math.exp(-0.3 * layer)

    def lspec(shape):
        nd = len(shape)
        return pl.BlockSpec((1,) + tuple(shape[1:]), lambda b, i: (layer,) + (0,) * (nd - 1))

    in_specs = [
        pl.BlockSpec((1, N_QBLK, tq, MXU_W), lambda b, i: (b, 0, i, 0)),
        pl.BlockSpec((1, N_KVBLK, n, MXU_W), lambda b, i: (b, 0, 0, 0)),
    ]
    args = [q, kv]
    if has_ctx:
        p = ctx.shape[3]
        in_specs.append(pl.BlockSpec((None, 1, N_KVBLK, p, MXU_W), lambda b, i: (layer, b, 0, 0, 0)))
        args.append(ctx)
    in_specs += [
        pl.BlockSpec((1, tq, 4 * GROUP_W), lambda b, i: (b, i, 0)),
        pl.BlockSpec((1, tq, HY_W), lambda b, i: (b, i, 0)),
        pl.BlockSpec((1, tq, D_MODEL), lambda b, i: (b, i, 0)),
        pl.BlockSpec((1, 1, 3 * D_MODEL), (lambda b, i: (b, 0, 0)) if per_batch_mod else (lambda b, i: (0, 0, 0))),
    ]
    args += [gates, ohy, x, mod]
    for name in ("norm_post", "w_out", "diff_subln", "diff_lam"):
        in_specs.append(lspec(prep[name].shape))
        args.append(prep[name])

    return pl.pallas_call(
        functools.partial(_attn_kernel, has_ctx=has_ctx, lam_init=lam_init, tq=tq),
        grid=(nb, n // tq),
        in_specs=in_specs,
        out_specs=pl.BlockSpec((1, tq, D_MODEL), lambda b, i: (b, i, 0)),
        out_shape=jax.ShapeDtypeStruct((nb, n, D_MODEL), F32),
        scratch_shapes=[pltpu.VMEM((tq, 3 * GROUP_W), F32)],
        compiler_params=pltpu.CompilerParams(
            dimension_semantics=("parallel", "arbitrary"), vmem_limit_bytes=VMEM_LIMIT),
        name="attn_lat" if has_ctx else "attn_ctx",
    )(*args)


def _rope_tables(n):
    tok = jnp.arange(n, dtype=jnp.int32)
    row = (tok // GRID_W).astype(F32)
    col = (tok % GRID_W).astype(F32)
    lane = jnp.arange(LANES, dtype=jnp.int32)
    tabs = []
    for d in (64, 32):
        m = d // 2
        i = lane % d
        ii = i % m
        f = ii % (m // 2)
        inv = ROPE_BASE ** (-(2 * f).astype(F32) / m)
        pos = jnp.where((i // m)[None, :] == 0, row[:, None], col[:, None])
        ang = pos * inv[None, :]
        sign = jnp.where(ii < m // 2, -1.0, 1.0).astype(F32)
        tabs += [jnp.cos(ang), jnp.sin(ang) * sign[None, :]]
    return tabs


def _dft_tables(n):
    k = jnp.arange(n, dtype=jnp.int32)
    idx = ((2 * k[:, None] + 1) * (2 * k[None, :] + 1)) % (8 * n)
    ang = idx.astype(F32) * (math.pi / (4 * n))
    cs = jnp.cos(ang).astype(BF16)
    ss = jnp.sin(ang).astype(BF16)
    th = (2 * k + 1).astype(F32) * (math.pi / (4 * n))
    ch = jnp.cos(th)[:, None]
    sh = jnp.sin(th)[:, None]
    t = jnp.linspace(0.0, 1.0, n, dtype=F32)[:, None]
    w = 2.0 * math.pi * jnp.arange(n, dtype=F32)[:, None] / n
    f = jnp.linspace(1e-4, HY_BANDS - 1, HY_BANDS, dtype=F32)[None, :]
    feats = jnp.concatenate([t, jnp.cos(f * w), -jnp.sin(f * w), jnp.zeros((n, 7), F32), ch, sh], axis=-1)
    feats = jnp.pad(feats, ((0, 0), (0, LANES - feats.shape[1])))
    return cs, ss, feats


def _prep_weights(p):
    offs = [0]
    for s in SPLIT_SIZES:
        offs.append(offs[-1] + s)
    (o_cq, o_ckv, o_kr, o_mg, o_gq, o_gk, o_gv, o_gg, o_dq, o_dk, o_dv, o_dg, o_hy, o_hg) = offs[:-1]
    w = p["w_in"]

    def cols(o, n):
        return w[:, :, o:o + n]

    def perm_heads(a, axis):
        parts = jnp.split(a, 4, axis=axis)
        return jnp.concatenate([parts[0], parts[2], parts[1], parts[3]], axis=axis)

    kr = cols(o_kr, 32)
    w_in = jnp.concatenate([
        cols(o_cq, 256), cols(o_ckv, 128), kr, kr, kr, kr,
        perm_heads(cols(o_gq, 256), 2), cols(o_gk, 128), cols(o_gv, 128),
        cols(o_dq, 256), cols(o_dk, 256), cols(o_dv, 256), cols(o_hy, 768),
        cols(o_mg, 256), perm_heads(cols(o_gg, 256), 2), cols(o_dg, 256), cols(o_hg, 256),
    ], axis=2).astype(BF16)

    wq = p["mla_wq_b"].reshape(DEPTH, MLA_Q_RANK, MLA_HEADS, MLA_NOPE + MLA_ROPE)
    wqb = jnp.concatenate([wq[..., :MLA_NOPE].reshape(DEPTH, MLA_Q_RANK, 256),
                           wq[..., MLA_NOPE:].reshape(DEPTH, MLA_Q_RANK, 128)], axis=2).astype(BF16)
    wk = p["mla_wkv_b"].reshape(DEPTH, MLA_KV_RANK, MLA_HEADS, MLA_NOPE + MLA_V)
    wkvb = jnp.concatenate([wk[..., :MLA_NOPE].reshape(DEPTH, MLA_KV_RANK, 256),
                            wk[..., MLA_NOPE:].reshape(DEPTH, MLA_KV_RANK, 256)], axis=2).astype(BF16)

    wo = p["w_out"]
    w_out = jnp.concatenate([wo[:, 0:256], perm_heads(wo[:, 256:512], 1), wo[:, 512:]], axis=1).astype(BF16)

    def pad_to(a, shape):
        return jnp.pad(a, [(0, s - d) for d, s in zip(a.shape, shape)])

    return {
        "norm_pre": p["norm_pre"].reshape(DEPTH, 1, D_MODEL),
        "norm_post": p["norm_post"].reshape(DEPTH, 1, D_MODEL),
        "w_in": w_in, "wqb": wqb, "wkvb": wkvb, "w_out": w_out,
        "mla_q_norm": p["mla_q_norm"].reshape(DEPTH, 1, MLA_Q_RANK),
        "mla_kv_norm": p["mla_kv_norm"].reshape(DEPTH, 1, MLA_KV_RANK),
        "gqa_q_norm": jnp.tile(p["gqa_q_norm"], (1, 4)).reshape(DEPTH, 1, 256),
        "gqa_k_norm": jnp.tile(p["gqa_k_norm"], (1, 2)).reshape(DEPTH, 1, 128),
        "diff_subln": jnp.tile(p["diff_subln"], (1, 4)).reshape(DEPTH, 1, 256),
        "diff_lam": jnp.stack([p["diff_lq1"], p["diff_lk1"], p["diff_lq2"], p["diff_lk2"]], axis=1),
        "hy_conv_w": p["hy_conv_w"],
        "hy_conv_b": p["hy_conv_b"].reshape(DEPTH, 1, 3 * HY_W),
        "hy_fw1": pad_to(p["hy_fw1"], (DEPTH, LANES, LANES)),
        "hy_fb1": pad_to(p["hy_fb1"].reshape(DEPTH, 1, HY_FFN), (DEPTH, 1, LANES)),
        "hy_fw2": pad_to(p["hy_fw2"], (DEPTH, LANES, LANES)),
        "hy_fb2": pad_to(p["hy_fb2"].reshape(DEPTH, 1, HY_FFN), (DEPTH, 1, LANES)),
        "hy_fw3": pad_to(p["hy_fw3"], (DEPTH, LANES, 4 * HY_W)),
        "hy_freq": pad_to(p["hy_freq"], (DEPTH, 2, LANES)),
        "hy_decay": p["hy_decay"].reshape(DEPTH, 1, HY_W),
        "hy_bias": p["hy_bias"],
    }


def kernel(x_prompt, x_sample, cache_mla_ckv, cache_mla_krope, cache_gqa_k, cache_gqa_v, cache_diff_k, cache_diff_v, c, c_ctx, norm_pre, norm_post, ada_w, ada_b, w_in, w_out, mla_q_norm, mla_wq_b, mla_kv_norm, mla_wkv_b, gqa_q_norm, gqa_k_norm, diff_lq1, diff_lk1, diff_lq2, diff_lk2, diff_subln, hy_conv_w, hy_conv_b, hy_fw1, hy_fb1, hy_fw2, hy_fb2, hy_fw3, hy_freq, hy_decay, hy_bias):
    nb_p, n_p, _ = x_prompt.shape
    nb_s, n_s, _ = x_sample.shape
    past = cache_mla_ckv.shape[2]

    prep = _prep_weights(dict(
        norm_pre=norm_pre, norm_post=norm_post, w_in=w_in, w_out=w_out, mla_q_norm=mla_q_norm,
        mla_wq_b=mla_wq_b, mla_kv_norm=mla_kv_norm, mla_wkv_b=mla_wkv_b, gqa_q_norm=gqa_q_norm,
        gqa_k_norm=gqa_k_norm, diff_lq1=diff_lq1, diff_lk1=diff_lk1, diff_lq2=diff_lq2, diff_lk2=diff_lk2,
        diff_subln=diff_subln, hy_conv_w=hy_conv_w, hy_conv_b=hy_conv_b, hy_fw1=hy_fw1, hy_fb1=hy_fb1,
        hy_fw2=hy_fw2, hy_fb2=hy_fb2, hy_fw3=hy_fw3, hy_freq=hy_freq, hy_decay=hy_decay, hy_bias=hy_bias))
    tabs = _rope_tables(n_s)
    dft_p = _dft_tables(n_p)
    dft_s = _dft_tables(n_s)

    cc8 = jnp.concatenate([c_ctx[None, :], c, jnp.zeros((8 - 1 - nb_s, D_MODEL), F32)], axis=0)
    mod = _ada_call(cc8, ada_w, ada_b)

    ctx = _ctx_call(
        cache_mla_ckv, jnp.tile(cache_mla_krope, (1, 1, 1, 4)),
        cache_gqa_k.reshape(nb_s, DEPTH, past, 128), cache_gqa_v.reshape(nb_s, DEPTH, past, 128),
        cache_diff_k.reshape(nb_s, DEPTH, past, 256), cache_diff_v.reshape(nb_s, DEPTH, past, 256),
        prep["wkvb"])

    y_p, y_s = x_prompt, x_sample
    st = [[] for _ in range(6)]
    for i in range(DEPTH):
        mod_p = mod[i, 0:1].reshape(1, 1, 3 * D_MODEL)
        mod_s = mod[i, 1:1 + nb_s].reshape(nb_s, 1, 3 * D_MODEL)
        outs = _inproj_call(i, y_p, mod_p, prep, None, rope=False, states=True, tm=n_p)
        q, kv, hy, gates = outs[:4]
        for lst, s in zip(st, outs[4:]):
            lst.append(s)
        ohy = _hyena_call(i, hy, gates, prep, dft_p, bt=4)
        y_p = _attn_call(i, q, kv, None, gates, ohy, y_p, mod_p, prep, tq=n_p)
        q, kv, hy, gates = _inproj_call(i, y_s, mod_s, prep, tabs, rope=True, states=False, tm=256)
        ohy = _hyena_call(i, hy, gates, prep, dft_s, bt=1)
        y_s = _attn_call(i, q, kv, ctx, gates, ohy, y_s, mod_s, prep, tq=256)

    new_ckv = jnp.stack(st[0], axis=1)
    new_kr = jnp.stack(st[1], axis=1)
    new_gk = jnp.stack(st[2], axis=1).reshape(nb_p, DEPTH, n_p, GQA_KV_HEADS, GQA_HD)
    new_gv = jnp.stack(st[3], axis=1).reshape(nb_p, DEPTH, n_p, GQA_KV_HEADS, GQA_HD)
    new_dk = jnp.stack(st[4], axis=1).reshape(nb_p, DEPTH, n_p, DIFF_HEADS, 2 * DIFF_HD)
    new_dv = jnp.stack(st[5], axis=1).reshape(nb_p, DEPTH, n_p, DIFF_HEADS, DIFF_VD)
    return (y_p, y_s, new_ckv, new_kr, new_gk, new_gv, new_dk, new_dv)
```

```python
import functools
import math

import jax
import jax.numpy as jnp
from jax import lax
from jax.experimental import pallas as pl
from jax.experimental.pallas import tpu as pltpu

F32 = jnp.float32
BF16 = jnp.bfloat16

D_MODEL = 1024
DEPTH = 2
GRID_W = 64
ROPE_BASE = 10000.0
EPS = 1e-6
GROUP_W = 256
MLA_HEADS = 4
MLA_NOPE = 64
MLA_ROPE = 32
MLA_V = 64
MLA_Q_RANK = 256
MLA_KV_RANK = 128
GQA_HEADS = 4
GQA_KV_HEADS = 2
GQA_HD = 64
DIFF_HEADS = 4
DIFF_VD = 64
DIFF_HD = 32
HY_W = 256
HY_EMB = 33
HY_BANDS = 16
HY_FFN = 64
SPLIT_SIZES = (256, 128, 32, 256, 256, 128, 128, 256, 256, 256, 256, 256, 768, 256)
P_IN = sum(SPLIT_SIZES)

LANES = 128
MXU_W = 256
P_PAD = 14 * MXU_W
N_QBLK = 4
N_KVBLK = 6
HY_CHUNK = 512
HY_HALO = 16
NT_DIMS = (((1,), (1,)), ((), ()))
VMEM_LIMIT = 56 * 1024 * 1024


def _dot(a, b):
    return jnp.dot(a, b, preferred_element_type=F32)


def _dot_nt(a, b):
    return lax.dot_general(a, b, NT_DIMS, preferred_element_type=F32)


def _rms_full(v, g):
    ms = jnp.mean(v * v, axis=-1, keepdims=True)
    return v * lax.rsqrt(ms + EPS) * g


def _rms_heads64(v, g):
    outs = []
    lane = lax.broadcasted_iota(jnp.int32, (v.shape[0], LANES), 1)
    lo = lane < 64
    for b in range(v.shape[1] // LANES):
        vb = v[:, b * LANES:(b + 1) * LANES]
        v2 = vb * vb
        s_lo = jnp.sum(jnp.where(lo, v2, 0.0), axis=-1, keepdims=True)
        s_hi = jnp.sum(jnp.where(lo, 0.0, v2), axis=-1, keepdims=True)
        r = jnp.where(lo, lax.rsqrt(s_lo * (1.0 / 64) + EPS), lax.rsqrt(s_hi * (1.0 / 64) + EPS))
        outs.append(vb * r)
    out = outs[0] if len(outs) == 1 else jnp.concatenate(outs, axis=1)
    return out * g


def _rope(v, cos_t, sin_t, half):
    lane = lax.broadcasted_iota(jnp.int32, (v.shape[0], LANES), 1)
    first = (lane % (2 * half)) < half
    outs = []
    for b in range(v.shape[1] // LANES):
        vb = v[:, b * LANES:(b + 1) * LANES]
        up = pltpu.roll(vb, LANES - half, axis=1)
        dn = pltpu.roll(vb, half, axis=1)
        outs.append(vb * cos_t + jnp.where(first, up, dn) * sin_t)
    return outs[0] if len(outs) == 1 else jnp.concatenate(outs, axis=1)


def _ada_kernel(cc_ref, w_ref, b_ref, o_ref):
    a = cc_ref[...]
    a = a * jax.nn.sigmoid(a)
    o_ref[0] = _dot(a.astype(BF16), w_ref[0].astype(BF16)) + b_ref[0]


def _ada_call(cc8, ada_w, ada_b):
    tn = 768
    return pl.pallas_call(
        _ada_kernel,
        grid=(DEPTH, 3 * D_MODEL // tn),
        in_specs=[
            pl.BlockSpec((8, D_MODEL), lambda l, j: (0, 0)),
            pl.BlockSpec((1, D_MODEL, tn), lambda l, j: (l, 0, j)),
            pl.BlockSpec((1, 1, tn), lambda l, j: (l, 0, j)),
        ],
        out_specs=pl.BlockSpec((1, 8, tn), lambda l, j: (l, 0, j)),
        out_shape=jax.ShapeDtypeStruct((DEPTH, 8, 3 * D_MODEL), F32),
        compiler_params=pltpu.CompilerParams(dimension_semantics=("parallel", "parallel")),
        name="ada_mod",
    )(cc8, ada_w, ada_b.reshape(DEPTH, 1, 3 * D_MODEL))


def _ctx_kernel(ckv_ref, kr4_ref, gk_ref, gv_ref, dk_ref, dv_ref, wkvb_ref, o_ref):
    kvp = _dot(ckv_ref[0, 0].astype(BF16), wkvb_ref[0])
    kr4 = kr4_ref[0, 0].astype(BF16)
    o_ref[0, 0, 0] = jnp.concatenate([kvp[:, 0:128].astype(BF16), kr4], axis=1)
    o_ref[0, 0, 1] = jnp.concatenate([kvp[:, 128:256].astype(BF16), kr4], axis=1)
    o_ref[0, 0, 2] = kvp[:, 256:512].astype(BF16)
    o_ref[0, 0, 3] = jnp.concatenate([gk_ref[0, 0].astype(BF16), gv_ref[0, 0].astype(BF16)], axis=1)
    o_ref[0, 0, 4] = dk_ref[0, 0].astype(BF16)
    o_ref[0, 0, 5] = dv_ref[0, 0].astype(BF16)


def _ctx_call(ckv, kr4, gk, gv, dk, dv, wkvb):
    nb, _, p, _ = ckv.shape

    def spec(w):
        return pl.BlockSpec((1, 1, p, w), lambda l, b: (b, l, 0, 0))

    return pl.pallas_call(
        _ctx_kernel,
        grid=(DEPTH, nb),
        in_specs=[spec(128), spec(128), spec(128), spec(128), spec(256), spec(256),
                  pl.BlockSpec((1, MLA_KV_RANK, 512), lambda l, b: (l, 0, 0))],
        out_specs=pl.BlockSpec((1, 1, N_KVBLK, p, MXU_W), lambda l, b: (l, b, 0, 0, 0)),
        out_shape=jax.ShapeDtypeStruct((DEPTH, nb, N_KVBLK, p, MXU_W), BF16),
        compiler_params=pltpu.CompilerParams(dimension_semantics=("parallel", "parallel")),
        name="ctx_kv",
    )(ckv, kr4, gk, gv, dk, dv, wkvb)


def _inproj_kernel(*refs, rope, states):
    it = iter(refs)
    x_ref, mod_ref, gpre_ref, w_ref, wqb_ref, wkvb_ref = (next(it) for _ in range(6))
    gq_mla_ref, gkv_mla_ref, gqn_ref, gkn_ref = (next(it) for _ in range(4))
    if rope:
        cos64_ref, sin64_ref, cos32_ref, sin32_ref = (next(it) for _ in range(4))
    q_ref, kv_ref, hy_ref, gate_ref = (next(it) for _ in range(4))
    if states:
        s_ckv_ref, s_kr_ref, s_gk_ref, s_gv_ref, s_dk_ref, s_dv_ref = (next(it) for _ in range(6))
    h_scr = next(it)

    x = x_ref[0]
    mod = mod_ref[0]
    shift = mod[:, 0:D_MODEL]
    scale = mod[:, D_MODEL:2 * D_MODEL]
    ms = jnp.mean(x * x, axis=-1, keepdims=True)
    h = (x * lax.rsqrt(ms + EPS) * gpre_ref[0]) * (1.0 + scale) + shift
    h_scr[...] = h.astype(BF16)

    def proj(g, n=1):
        return _dot(h_scr[...], w_ref[0, :, g * MXU_W:(g + n) * MXU_W])

    def rope32(v):
        return _rope(v, cos32_ref[...], sin32_ref[...], 8) if rope else v

    def rope64(v):
        return _rope(v, cos64_ref[...], sin64_ref[...], 16) if rope else v

    cq = _rms_full(proj(0), gq_mla_ref[0])
    q3 = _dot(cq.astype(BF16), wqb_ref[0]) * ((MLA_NOPE + MLA_ROPE) ** -0.5)
    qr = rope32(q3[:, 256:384]).astype(BF16)
    q_ref[0, 0] = jnp.concatenate([q3[:, 0:128].astype(BF16), qr], axis=1)
    q_ref[0, 1] = jnp.concatenate([q3[:, 128:256].astype(BF16), qr], axis=1)

    z1 = proj(1)
    ckv = _rms_full(z1[:, 0:128], gkv_mla_ref[0])
    kr4 = z1[:, 128:256]
    if states:
        s_ckv_ref[0] = ckv
        s_kr_ref[0] = kr4[:, 0:MLA_ROPE]
    kvp = _dot(ckv.astype(BF16), wkvb_ref[0])
    kr4 = rope32(kr4).astype(BF16)
    kv_ref[0, 0] = jnp.concatenate([kvp[:, 0:128].astype(BF16), kr4], axis=1)
    kv_ref[0, 1] = jnp.concatenate([kvp[:, 128:256].astype(BF16), kr4], axis=1)
    kv_ref[0, 2] = kvp[:, 256:512].astype(BF16)

    gq = rope64(_rms_heads64(proj(2), gqn_ref[0])) * (GQA_HD ** -0.5)
    q_ref[0, 2] = gq.astype(BF16)
    z3 = proj(3)
    gk = _rms_heads64(z3[:, 0:128], gkn_ref[0])
    gv = z3[:, 128:256]
    if states:
        s_gk_ref[0] = gk
        s_gv_ref[0] = gv
    kv_ref[0, 3] = jnp.concatenate([rope64(gk).astype(BF16), gv.astype(BF16)], axis=1)

    q_ref[0, 3] = (rope32(proj(4)) * (DIFF_HD ** -0.5)).astype(BF16)
    dk = proj(5)
    dv = proj(6)
    if states:
        s_dk_ref[0] = dk
        s_dv_ref[0] = dv
    kv_ref[0, 4] = rope32(dk).astype(BF16)
    kv_ref[0, 5] = dv.astype(BF16)

    for j in range(3):
        hy_ref[0, :, j * MXU_W:(j + 1) * MXU_W] = proj(7 + j).astype(BF16)
    for j in range(4):
        g = proj(10 + j)
        gate_ref[0, :, j * MXU_W:(j + 1) * MXU_W] = (g * jax.nn.sigmoid(g)).astype(BF16)


def _inproj_call(layer, x, mod, prep, tabs, *, rope, states, tm):
    nb, n, _ = x.shape
    per_batch_mod = mod.shape[0] > 1
    tiles = n // tm

    def lspec(shape):
        nd = len(shape)
        return pl.BlockSpec((1,) + tuple(shape[1:]), lambda b, i: (layer,) + (0,) * (nd - 1))

    in_specs = [
        pl.BlockSpec((1, tm, D_MODEL), lambda b, i: (b, i, 0)),
        pl.BlockSpec((1, 1, 3 * D_MODEL), (lambda b, i: (b, 0, 0)) if per_batch_mod else (lambda b, i: (0, 0, 0))),
    ]
    args = [x, mod]
    for name in ("norm_pre", "w_in", "wqb", "wkvb", "mla_q_norm", "mla_kv_norm", "gqa_q_norm", "gqa_k_norm"):
        a = prep[name]
        in_specs.append(lspec(a.shape))
        args.append(a)
    if rope:
        for t in tabs:
            in_specs.append(pl.BlockSpec((tm, LANES), lambda b, i: (i, 0)))
            args.append(t)

    out_shape = [
        jax.ShapeDtypeStruct((nb, N_QBLK, n, MXU_W), BF16),
        jax.ShapeDtypeStruct((nb, N_KVBLK, n, MXU_W), BF16),
        jax.ShapeDtypeStruct((nb, n, 3 * HY_W), BF16),
        jax.ShapeDtypeStruct((nb, n, 4 * GROUP_W), BF16),
    ]
    out_specs = [
        pl.BlockSpec((1, N_QBLK, tm, MXU_W), lambda b, i: (b, 0, i, 0)),
        pl.BlockSpec((1, N_KVBLK, tm, MXU_W), lambda b, i: (b, 0, i, 0)),
        pl.BlockSpec((1, tm, 3 * HY_W), lambda b, i: (b, i, 0)),
        pl.BlockSpec((1, tm, 4 * GROUP_W), lambda b, i: (b, i, 0)),
    ]
    if states:
        for w in (MLA_KV_RANK, MLA_ROPE, 128, 128, 256, 256):
            out_shape.append(jax.ShapeDtypeStruct((nb, n, w), F32))
            out_specs.append(pl.BlockSpec((1, tm, w), lambda b, i: (b, i, 0)))

    return pl.pallas_call(
        functools.partial(_inproj_kernel, rope=rope, states=states),
        grid=(nb, tiles),
        in_specs=in_specs,
        out_specs=out_specs,
        out_shape=out_shape,
        scratch_shapes=[pltpu.VMEM((tm, D_MODEL), BF16)],
        compiler_params=pltpu.CompilerParams(
            dimension_semantics=("parallel", "parallel"), vmem_limit_bytes=VMEM_LIMIT),
        name="inproj_rope" if rope else "inproj_ctx",
    )(*args)


def _chunk_loop(nchunk, body):
    if nchunk == 1:
        body(0)
    else:
        def wrapped(k, carry):
            body(k)
            return carry
        lax.fori_loop(0, nchunk, wrapped, 0)


def _hyena_kernel(hy_ref, gate_ref, cw_ref, cb_ref, feats_ref, fw1_ref, fb1_ref, fw2_ref, fb2_ref,
                  fw3_ref, freq_ref, decay_ref, bias_ref, cs_ref, ss_ref,
                  o_ref, fr_scr, fi_scr, work, hy_pad, *, n, bt):
    hp = lax.Precision.HIGHEST
    rc = min(n, HY_CHUNK)
    nchunk = n // rc
    rowc = lax.broadcasted_iota(jnp.int32, (rc, HY_W), 0)

    def rows(k):
        if nchunk == 1:
            return slice(0, rc)
        return pl.ds(pl.multiple_of(k * rc, rc), rc)

    @pl.when(pl.program_id(0) == 0)
    def _filters():
        freq = freq_ref[0]
        bias = bias_ref[0]

        def taps(k):
            ft = feats_ref[rows(k), :]
            t = ft[:, 0:1]
            h1 = jnp.sin(freq[0:1, :] * (jnp.dot(ft, fw1_ref[0], precision=hp, preferred_element_type=F32) + fb1_ref[0]))
            h2 = jnp.sin(freq[1:2, :] * (jnp.dot(h1, fw2_ref[0], precision=hp, preferred_element_type=F32) + fb2_ref[0]))
            h3 = jnp.dot(h2, fw3_ref[0], precision=hp, preferred_element_type=F32)
            dec = jnp.exp(-t * jnp.abs(decay_ref[0]))
            lag0 = (rowc + k * rc) == 0
            for o in range(2):
                hf = h3[:, o * 512:o * 512 + 256] * dec
                hb = h3[:, o * 512 + 256:o * 512 + 512] * dec
                hf = hf + jnp.where(lag0, bias[o:o + 1, :], 0.0)
                hb = jnp.where(lag0, 0.0, hb)
                work[rows(k), o * 512:o * 512 + 256] = hf.astype(BF16)
                work[rows(k), o * 512 + 256:o * 512 + 512] = hb.astype(BF16)

        _chunk_loop(nchunk, taps)

        def spectra(k):
            ft = feats_ref[rows(k), :]
            c = ft[:, HY_EMB + 7:HY_EMB + 8]
            s = ft[:, HY_EMB + 8:HY_EMB + 9]
            p = _dot(cs_ref[rows(k), :], work[...])
            q = _dot(ss_ref[rows(k), :], work[...])
            for o in range(2):
                pf, pb = p[:, o * 512:o * 512 + 256], p[:, o * 512 + 256:o * 512 + 512]
                qf, qb = q[:, o * 512:o * 512 + 256], q[:, o * 512 + 256:o * 512 + 512]
                fr_scr[o, rows(k), :] = (c * (pf + pb) + s * (qf + qb)) * (1.0 / n)
                fi_scr[o, rows(k), :] = (s * (pf - pb) - c * (qf - qb)) * (1.0 / n)

        _chunk_loop(nchunk, spectra)

    cw = cw_ref[0]
    cb = cb_ref[0]
    hy_pad[0:HY_HALO, :] = jnp.zeros((HY_HALO, 3 * HY_W), BF16)
    hy_pad[HY_HALO + n:2 * HY_HALO + n, :] = jnp.zeros((HY_HALO, 3 * HY_W), BF16)

    def short_conv(k, j):
        sl = slice(j * HY_W, (j + 1) * HY_W)
        if nchunk == 1:
            ext = hy_pad[:, sl].astype(F32)
        else:
            ext = hy_pad[pl.ds(pl.multiple_of(k * rc, rc), rc + 2 * HY_HALO), sl].astype(F32)
        um = pltpu.roll(ext, 1, axis=0)[HY_HALO:HY_HALO + rc]
        up = pltpu.roll(ext, rc + 2 * HY_HALO - 1, axis=0)[HY_HALO:HY_HALO + rc]
        u = ext[HY_HALO:HY_HALO + rc]
        return um * cw[0:1, sl] + u * cw[1:2, sl] + up * cw[2:3, sl] + cb[:, sl]

    zc, yrc, yic = slice(0, 256), slice(256, 512), slice(512, 768)

    def one(bi):
        hy_pad[HY_HALO:HY_HALO + n, :] = hy_ref[bi]

        def conv_v(k):
            work[rows(k), zc] = short_conv(k, 2).astype(BF16)

        _chunk_loop(nchunk, conv_v)
        for o in range(2):
            def forward(k, o=o):
                a = _dot(cs_ref[rows(k), :], work[:, zc])
                b = _dot(ss_ref[rows(k), :], work[:, zc])
                fr = fr_scr[o, rows(k), :]
                fi = fi_scr[o, rows(k), :]
                work[rows(k), yrc] = (a * fr + b * fi).astype(BF16)
                work[rows(k), yic] = (a * fi - b * fr).astype(BF16)

            def inverse(k, o=o):
                y = _dot(cs_ref[rows(k), :], work[:, yrc]) - _dot(ss_ref[rows(k), :], work[:, yic])
                z = short_conv(k, o) * y
                if o == 0:
                    work[rows(k), zc] = z.astype(BF16)
                else:
                    o_ref[bi, rows(k), :] = (z * gate_ref[bi, rows(k), :].astype(F32)).astype(BF16)

            _chunk_loop(nchunk, forward)
            _chunk_loop(nchunk, inverse)

    if bt == 1:
        one(0)
    else:
        def body(bi, carry):
            one(bi)
            return carry
        lax.fori_loop(0, bt, body, 0)


def _hyena_call(layer, hy, gates, prep, dft, *, bt):
    nb, n, _ = hy.shape
    cs, ss, feats = dft

    def lspec(shape):
        nd = len(shape)
        return pl.BlockSpec((1,) + tuple(shape[1:]), lambda b: (layer,) + (0,) * (nd - 1))

    def cspec(shape):
        nd = len(shape)
        return pl.BlockSpec(tuple(shape), lambda b: (0,) * nd, pipeline_mode=pl.Buffered(1))

    big = n >= 1024
    in_specs = [
        pl.BlockSpec((bt, n, 3 * HY_W), lambda b: (b, 0, 0)),
        pl.BlockSpec((bt, n, GROUP_W), lambda b: (b, 0, 3)),
    ]
    args = [hy, gates]
    for name in ("hy_conv_w", "hy_conv_b"):
        in_specs.append(lspec(prep[name].shape))
        args.append(prep[name])
    in_specs.append(cspec(feats.shape))
    args.append(feats)
    for name in ("hy_fw1", "hy_fb1", "hy_fw2", "hy_fb2", "hy_fw3", "hy_freq", "hy_decay", "hy_bias"):
        in_specs.append(lspec(prep[name].shape))
        args.append(prep[name])
    in_specs += [cspec(cs.shape), cspec(ss.shape)]
    args += [cs, ss]

    return pl.pallas_call(
        functools.partial(_hyena_kernel, n=n, bt=bt),
        grid=(nb // bt,),
        in_specs=in_specs,
        out_specs=pl.BlockSpec((bt, n, HY_W), lambda b: (b, 0, 0)),
        out_shape=jax.ShapeDtypeStruct((nb, n, HY_W), BF16),
        scratch_shapes=[pltpu.VMEM((2, n, HY_W), F32), pltpu.VMEM((2, n, HY_W), F32),
                        pltpu.VMEM((n, 4 * HY_W), BF16),
                        pltpu.VMEM((n + 2 * HY_HALO, 3 * HY_W), BF16)],
        compiler_params=pltpu.CompilerParams(
            dimension_semantics=("arbitrary",), vmem_limit_bytes=VMEM_LIMIT),
        name="hyena_long" if big else "hyena_short",
    )(*args)


def _attn_kernel(*refs, has_ctx, lam_init, tq):
    it = iter(refs)
    q_ref, kv_ref = next(it), next(it)
    ctx_ref = next(it) if has_ctx else None
    gate_ref, ohy_ref, x_ref, mod_ref, gpost_ref, wout_ref, subln_ref, lamp_ref = (next(it) for _ in range(8))
    y_ref = next(it)

    lane = lax.broadcasted_iota(jnp.int32, (tq, MXU_W), 1)

    def band(lo, width):
        return (lane >= lo) & (lane < lo + width)

    def scores(qs, kblk):
        s_l = _dot_nt(qs, kv_ref[0, kblk])
        s_c = _dot_nt(qs, ctx_ref[0, kblk]) if has_ctx else None
        return s_l, s_c

    def softmax_rows(s_l, s_c):
        m = jnp.max(s_l, axis=-1, keepdims=True)
        if has_ctx:
            m = jnp.maximum(m, jnp.max(s_c, axis=-1, keepdims=True))
            p_c = jnp.exp(s_c - m)
        p_l = jnp.exp(s_l - m)
        l = jnp.sum(p_l, axis=-1, keepdims=True)
        if has_ctx:
            l = l + jnp.sum(p_c, axis=-1, keepdims=True)
            return p_l, p_c, l
        return p_l, None, l

    def pv(p_l, p_c, vblk):
        o = _dot(p_l.astype(BF16), kv_ref[0, vblk])
        if has_ctx:
            o = o + _dot(p_c.astype(BF16), ctx_ref[0, vblk])
        return o

    def stack(parts):
        return jnp.concatenate(parts, axis=0)

    def pick_heads(blocks, width):
        out = blocks[0]
        for h in range(1, 4):
            out = jnp.where(band(width * h, width), blocks[h], out)
        return out

    p_ls, p_cs, ls = [], [], []
    for slab in range(2):
        qf = q_ref[0, slab].astype(F32)
        qs = stack([jnp.where(band(64 * j, MLA_NOPE) | band(128 + MLA_ROPE * (2 * slab + j), MLA_ROPE), qf, 0.0)
                    for j in range(2)]).astype(BF16)
        p_l, p_c, l = softmax_rows(*scores(qs, slab))
        p_ls.append(p_l)
        p_cs.append(p_c)
        ls.append(l)
    o = pv(stack(p_ls), stack(p_cs) if has_ctx else None, 2)
    o01 = o[0:2 * tq] / ls[0]
    o23 = o[2 * tq:4 * tq] / ls[1]
    o_mla = pick_heads([o01[0:tq], o01[tq:2 * tq], o23[0:tq], o23[tq:2 * tq]], MLA_V)

    qf = q_ref[0, 2].astype(F32)
    qparts = []
    for j in range(2):
        for g in range(2):
            qsel = jnp.where(band(128 * j + GQA_HD * g, GQA_HD), qf, 0.0)
            q128 = qsel[:, 0:128] + qsel[:, 128:256]
            qparts.append(jnp.concatenate([q128, jnp.zeros_like(q128)], axis=1))
    p_l, p_c, l = softmax_rows(*scores(stack(qparts).astype(BF16), 3))
    o = (pv(p_l, p_c, 3) / l)[:, 128:256]
    lo_half = lax.broadcasted_iota(jnp.int32, (tq, LANES), 1) < GQA_HD
    o_gqa = jnp.concatenate([jnp.where(lo_half, o[(2 * j) * tq:(2 * j + 1) * tq], o[(2 * j + 1) * tq:(2 * j + 2) * tq])
                             for j in range(2)], axis=1)

    lamp = lamp_ref[0]
    lam = (jnp.exp(jnp.sum(lamp[0:1, :] * lamp[1:2, :], axis=-1, keepdims=True))
           - jnp.exp(jnp.sum(lamp[2:3, :] * lamp[3:4, :], axis=-1, keepdims=True)) + lam_init)
    qf = q_ref[0, 3].astype(F32)
    qs = stack([jnp.where(band(64 * h + DIFF_HD * s, DIFF_HD), qf, 0.0)
                for s in range(2) for h in range(DIFF_HEADS)]).astype(BF16)
    s_l, s_c = scores(qs, 4)
    half = DIFF_HEADS * tq
    p1_l, p1_c, l1 = softmax_rows(s_l[0:half], s_c[0:half] if has_ctx else None)
    p2_l, p2_c, l2 = softmax_rows(s_l[half:2 * half], s_c[half:2 * half] if has_ctx else None)
    w1 = 1.0 / l1
    w2 = lam / l2
    a_l = p1_l * w1 - p2_l * w2
    a_c = (p1_c * w1 - p2_c * w2) if has_ctx else None
    o = pv(a_l, a_c, 5)
    od = pick_heads([o[h * tq:(h + 1) * tq] for h in range(DIFF_HEADS)], DIFF_VD)

    od2 = od * od
    r = jnp.zeros_like(od)
    for h in range(DIFF_HEADS):
        mh = band(DIFF_VD * h, DIFF_VD)
        sh = jnp.sum(jnp.where(mh, od2, 0.0), axis=-1, keepdims=True)
        r = jnp.where(mh, lax.rsqrt(sh * (1.0 / DIFF_VD) + EPS), r)
    od = (od * r * subln_ref[0]) * (1.0 - lam_init)

    g = gate_ref[0].astype(F32)
    ob = jnp.concatenate([
        (o_mla * g[:, 0:256]).astype(BF16),
        (o_gqa * g[:, 256:512]).astype(BF16),
        (od * g[:, 512:768]).astype(BF16),
        ohy_ref[0],
    ], axis=1)
    out = _dot(ob, wout_ref[0])
    ms = jnp.mean(out * out, axis=-1, keepdims=True)
    gate_mod = mod_ref[0][:, 2 * D_MODEL:3 * D_MODEL]
    y_ref[0] = x_ref[0] + gate_mod * (out * lax.rsqrt(ms + EPS) * gpost_ref[0])


def _attn_call(layer, q, kv, ctx, gates, ohy, x, mod, prep, *, tq):
    nb, n, _ = x.shape
    has_ctx = ctx is not None
    per_batch_mod = mod.shape[0] > 1
    lam_init = 0.8 - 0.6 * math.exp(-0.3 * layer)

    def lspec(shape):
        nd = len(shape)
        return pl.BlockSpec((1,) + tuple(shape[1:]), lambda b, i: (layer,) + (0,) * (nd - 1))

    in_specs = [
        pl.BlockSpec((1, N_QBLK, tq, MXU_W), lambda b, i: (b, 0, i, 0)),
        pl.BlockSpec((1, N_KVBLK, n, MXU_W), lambda b, i: (b, 0, 0, 0)),
    ]
    args = [q, kv]
    if has_ctx:
        p = ctx.shape[3]
        in_specs.append(pl.BlockSpec((None, 1, N_KVBLK, p, MXU_W), lambda b, i: (layer, b, 0, 0, 0)))
        args.append(ctx)
    in_specs += [
        pl.BlockSpec((1, tq, 4 * GROUP_W), lambda b, i: (b, i, 0)),
        pl.BlockSpec((1, tq, HY_W), lambda b, i: (b, i, 0)),
        pl.BlockSpec((1, tq, D_MODEL), lambda b, i: (b, i, 0)),
        pl.BlockSpec((1, 1, 3 * D_MODEL), (lambda b, i: (b, 0, 0)) if per_batch_mod else (lambda b, i: (0, 0, 0))),
    ]
    args += [gates, ohy, x, mod]
    for name in ("norm_post", "w_out", "diff_subln", "diff_lam"):
        in_specs.append(lspec(prep[name].shape))
        args.append(prep[name])

    return pl.pallas_call(
        functools.partial(_attn_kernel, has_ctx=has_ctx, lam_init=lam_init, tq=tq),
        grid=(nb, n // tq),
        in_specs=in_specs,
        out_specs=pl.BlockSpec((1, tq, D_MODEL), lambda b, i: (b, i, 0)),
        out_shape=jax.ShapeDtypeStruct((nb, n, D_MODEL), F32),
        compiler_params=pltpu.CompilerParams(
            dimension_semantics=("parallel", "parallel"), vmem_limit_bytes=VMEM_LIMIT),
        name="attn_lat" if has_ctx else "attn_ctx",
    )(*args)


def _rope_tables(n):
    tok = jnp.arange(n, dtype=jnp.int32)
    row = (tok // GRID_W).astype(F32)
    col = (tok % GRID_W).astype(F32)
    lane = jnp.arange(LANES, dtype=jnp.int32)
    tabs = []
    for d in (64, 32):
        m = d // 2
        i = lane % d
        ii = i % m
        f = ii % (m // 2)
        inv = ROPE_BASE ** (-(2 * f).astype(F32) / m)
        pos = jnp.where((i // m)[None, :] == 0, row[:, None], col[:, None])
        ang = pos * inv[None, :]
        sign = jnp.where(ii < m // 2, -1.0, 1.0).astype(F32)
        tabs += [jnp.cos(ang), jnp.sin(ang) * sign[None, :]]
    return tabs


def _dft_tables(n):
    k = jnp.arange(n, dtype=jnp.int32)
    idx = ((2 * k[:, None] + 1) * (2 * k[None, :] + 1)) % (8 * n)
    ang = idx.astype(F32) * (math.pi / (4 * n))
    cs = jnp.cos(ang).astype(BF16)
    ss = jnp.sin(ang).astype(BF16)
    th = (2 * k + 1).astype(F32) * (math.pi / (4 * n))
    ch = jnp.cos(th)[:, None]
    sh = jnp.sin(th)[:, None]
    t = jnp.linspace(0.0, 1.0, n, dtype=F32)[:, None]
    w = 2.0 * math.pi * jnp.arange(n, dtype=F32)[:, None] / n
    f = jnp.linspace(1e-4, HY_BANDS - 1, HY_BANDS, dtype=F32)[None, :]
    feats = jnp.concatenate([t, jnp.cos(f * w), -jnp.sin(f * w), jnp.zeros((n, 7), F32), ch, sh], axis=-1)
    feats = jnp.pad(feats, ((0, 0), (0, LANES - feats.shape[1])))
    return cs, ss, feats


def _prep_weights(p):
    offs = [0]
    for s in SPLIT_SIZES:
        offs.append(offs[-1] + s)
    (o_cq, o_ckv, o_kr, o_mg, o_gq, o_gk, o_gv, o_gg, o_dq, o_dk, o_dv, o_dg, o_hy, o_hg) = offs[:-1]
    w = p["w_in"]

    def cols(o, n):
        return w[:, :, o:o + n]

    def perm_heads(a, axis):
        parts = jnp.split(a, 4, axis=axis)
        return jnp.concatenate([parts[0], parts[2], parts[1], parts[3]], axis=axis)

    kr = cols(o_kr, 32)
    w_in = jnp.concatenate([
        cols(o_cq, 256), cols(o_ckv, 128), kr, kr, kr, kr,
        perm_heads(cols(o_gq, 256), 2), cols(o_gk, 128), cols(o_gv, 128),
        cols(o_dq, 256), cols(o_dk, 256), cols(o_dv, 256), cols(o_hy, 768),
        cols(o_mg, 256), perm_heads(cols(o_gg, 256), 2), cols(o_dg, 256), cols(o_hg, 256),
    ], axis=2).astype(BF16)

    wq = p["mla_wq_b"].reshape(DEPTH, MLA_Q_RANK, MLA_HEADS, MLA_NOPE + MLA_ROPE)
    wqb = jnp.concatenate([wq[..., :MLA_NOPE].reshape(DEPTH, MLA_Q_RANK, 256),
                           wq[..., MLA_NOPE:].reshape(DEPTH, MLA_Q_RANK, 128)], axis=2).astype(BF16)
    wk = p["mla_wkv_b"].reshape(DEPTH, MLA_KV_RANK, MLA_HEADS, MLA_NOPE + MLA_V)
    wkvb = jnp.concatenate([wk[..., :MLA_NOPE].reshape(DEPTH, MLA_KV_RANK, 256),
                            wk[..., MLA_NOPE:].reshape(DEPTH, MLA_KV_RANK, 256)], axis=2).astype(BF16)

    wo = p["w_out"]
    w_out = jnp.concatenate([wo[:, 0:256], perm_heads(wo[:, 256:512], 1), wo[:, 512:]], axis=1).astype(BF16)

    def pad_to(a, shape):
        return jnp.pad(a, [(0, s - d) for d, s in zip(a.shape, shape)])

    return {
        "norm_pre": p["norm_pre"].reshape(DEPTH, 1, D_MODEL),
        "norm_post": p["norm_post"].reshape(DEPTH, 1, D_MODEL),
        "w_in": w_in, "wqb": wqb, "wkvb": wkvb, "w_out": w_out,
        "mla_q_norm": p["mla_q_norm"].reshape(DEPTH, 1, MLA_Q_RANK),
        "mla_kv_norm": p["mla_kv_norm"].reshape(DEPTH, 1, MLA_KV_RANK),
        "gqa_q_norm": jnp.tile(p["gqa_q_norm"], (1, 4)).reshape(DEPTH, 1, 256),
        "gqa_k_norm": jnp.tile(p["gqa_k_norm"], (1, 2)).reshape(DEPTH, 1, 128),
        "diff_subln": jnp.tile(p["diff_subln"], (1, 4)).reshape(DEPTH, 1, 256),
        "diff_lam": jnp.stack([p["diff_lq1"], p["diff_lk1"], p["diff_lq2"], p["diff_lk2"]], axis=1),
        "hy_conv_w": p["hy_conv_w"],
        "hy_conv_b": p["hy_conv_b"].reshape(DEPTH, 1, 3 * HY_W),
        "hy_fw1": pad_to(p["hy_fw1"], (DEPTH, LANES, LANES)),
        "hy_fb1": pad_to(p["hy_fb1"].reshape(DEPTH, 1, HY_FFN), (DEPTH, 1, LANES)),
        "hy_fw2": pad_to(p["hy_fw2"], (DEPTH, LANES, LANES)),
        "hy_fb2": pad_to(p["hy_fb2"].reshape(DEPTH, 1, HY_FFN), (DEPTH, 1, LANES)),
        "hy_fw3": pad_to(p["hy_fw3"], (DEPTH, LANES, 4 * HY_W)),
        "hy_freq": pad_to(p["hy_freq"], (DEPTH, 2, LANES)),
        "hy_decay": p["hy_decay"].reshape(DEPTH, 1, HY_W),
        "hy_bias": p["hy_bias"],
    }


def kernel(x_prompt, x_sample, cache_mla_ckv, cache_mla_krope, cache_gqa_k, cache_gqa_v, cache_diff_k, cache_diff_v, c, c_ctx, norm_pre, norm_post, ada_w, ada_b, w_in, w_out, mla_q_norm, mla_wq_b, mla_kv_norm, mla_wkv_b, gqa_q_norm, gqa_k_norm, diff_lq1, diff_lk1, diff_lq2, diff_lk2, diff_subln, hy_conv_w, hy_conv_b, hy_fw1, hy_fb1, hy_fw2, hy_fb2, hy_fw3, hy_freq, hy_decay, hy_bias):
    nb_p, n_p, _ = x_prompt.shape
    nb_s, n_s, _ = x_sample.shape
    past = cache_mla_ckv.shape[2]

    prep = _prep_weights(dict(
        norm_pre=norm_pre, norm_post=norm_post, w_in=w_in, w_out=w_out, mla_q_norm=mla_q_norm,
        mla_wq_b=mla_wq_b, mla_kv_norm=mla_kv_norm, mla_wkv_b=mla_wkv_b, gqa_q_norm=gqa_q_norm,
        gqa_k_norm=gqa_k_norm, diff_lq1=diff_lq1, diff_lk1=diff_lk1, diff_lq2=diff_lq2, diff_lk2=diff_lk2,
        diff_subln=diff_subln, hy_conv_w=hy_conv_w, hy_conv_b=hy_conv_b, hy_fw1=hy_fw1, hy_fb1=hy_fb1,
        hy_fw2=hy_fw2, hy_fb2=hy_fb2, hy_fw3=hy_fw3, hy_freq=hy_freq, hy_decay=hy_decay, hy_bias=hy_bias))
    tabs = _rope_tables(n_s)
    dft_p = _dft_tables(n_p)
    dft_s = _dft_tables(n_s)

    cc8 = jnp.concatenate([c_ctx[None, :], c, jnp.zeros((8 - 1 - nb_s, D_MODEL), F32)], axis=0)
    mod = _ada_call(cc8, ada_w, ada_b)

    ctx = _ctx_call(
        cache_mla_ckv, jnp.tile(cache_mla_krope, (1, 1, 1, 4)),
        cache_gqa_k.reshape(nb_s, DEPTH, past, 128), cache_gqa_v.reshape(nb_s, DEPTH, past, 128),
        cache_diff_k.reshape(nb_s, DEPTH, past, 256), cache_diff_v.reshape(nb_s, DEPTH, past, 256),
        prep["wkvb"])

    y_p, y_s = x_prompt, x_sample
    st = [[] for _ in range(6)]
    for i in range(DEPTH):
        mod_p = mod[i, 0:1].reshape(1, 1, 3 * D_MODEL)
        mod_s = mod[i, 1:1 + nb_s].reshape(nb_s, 1, 3 * D_MODEL)
        outs = _inproj_call(i, y_p, mod_p, prep, None, rope=False, states=True, tm=n_p)
        q, kv, hy, gates = outs[:4]
        for lst, s in zip(st, outs[4:]):
            lst.append(s)
        ohy = _hyena_call(i, hy, gates, prep, dft_p, bt=4)
        y_p = _attn_call(i, q, kv, None, gates, ohy, y_p, mod_p, prep, tq=n_p)
        q, kv, hy, gates = _inproj_call(i, y_s, mod_s, prep, tabs, rope=True, states=False, tm=256)
        ohy = _hyena_call(i, hy, gates, prep, dft_s, bt=1)
        y_s = _attn_call(i, q, kv, ctx, gates, ohy, y_s, mod_s, prep, tq=128)

    new_ckv = jnp.stack(st[0], axis=1)
    new_kr = jnp.stack(st[1], axis=1)
    new_gk = jnp.stack(st[2], axis=1).reshape(nb_p, DEPTH, n_p, GQA_KV_HEADS, GQA_HD)
    new_gv = jnp.stack(st[3], axis=1).reshape(nb_p, DEPTH, n_p, GQA_KV_HEADS, GQA_HD)
    new_dk = jnp.stack(st[4], axis=1).reshape(nb_p, DEPTH, n_p, DIFF_HEADS, 2 * DIFF_HD)
    new_dv = jnp.stack(st[5], axis=1).reshape(nb_p, DEPTH, n_p, DIFF_HEADS, DIFF_VD)
    return (y_p, y_s, new_ckv, new_kr, new_gk, new_gv, new_dk, new_dv)
```

```python
import functools
import math

import jax
import jax.numpy as jnp
from jax import lax
from jax.experimental import pallas as pl
from jax.experimental.pallas import tpu as pltpu

F32 = jnp.float32
BF16 = jnp.bfloat16

D_MODEL = 1024
DEPTH = 2
GRID_W = 64
ROPE_BASE = 10000.0
EPS = 1e-6
GROUP_W = 256
MLA_HEADS = 4
MLA_NOPE = 64
MLA_ROPE = 32
MLA_V = 64
MLA_Q_RANK = 256
MLA_KV_RANK = 128
GQA_HEADS = 4
GQA_KV_HEADS = 2
GQA_HD = 64
DIFF_HEADS = 4
DIFF_VD = 64
DIFF_HD = 32
HY_W = 256
HY_EMB = 33
HY_BANDS = 16
HY_FFN = 64
SPLIT_SIZES = (256, 128, 32, 256, 256, 128, 128, 256, 256, 256, 256, 256, 768, 256)
P_IN = sum(SPLIT_SIZES)

LANES = 128
MXU_W = 256
P_PAD = 14 * MXU_W
N_QBLK = 4
N_KVBLK = 6
HY_CHUNK = 512
HY_HALO = 16
NT_DIMS = (((1,), (1,)), ((), ()))
VMEM_LIMIT = 56 * 1024 * 1024


def _dot(a, b):
    return jnp.dot(a, b, preferred_element_type=F32)


def _dot_nt(a, b):
    return lax.dot_general(a, b, NT_DIMS, preferred_element_type=F32)


def _rms_full(v, g):
    ms = jnp.mean(v * v, axis=-1, keepdims=True)
    return v * lax.rsqrt(ms + EPS) * g


def _rms_heads64(v, g):
    outs = []
    lane = lax.broadcasted_iota(jnp.int32, (v.shape[0], LANES), 1)
    lo = lane < 64
    for b in range(v.shape[1] // LANES):
        vb = v[:, b * LANES:(b + 1) * LANES]
        v2 = vb * vb
        s_lo = jnp.sum(jnp.where(lo, v2, 0.0), axis=-1, keepdims=True)
        s_hi = jnp.sum(jnp.where(lo, 0.0, v2), axis=-1, keepdims=True)
        r = jnp.where(lo, lax.rsqrt(s_lo * (1.0 / 64) + EPS), lax.rsqrt(s_hi * (1.0 / 64) + EPS))
        outs.append(vb * r)
    out = outs[0] if len(outs) == 1 else jnp.concatenate(outs, axis=1)
    return out * g


def _rope(v, cos_t, sin_t, half):
    lane = lax.broadcasted_iota(jnp.int32, (v.shape[0], LANES), 1)
    first = (lane % (2 * half)) < half
    outs = []
    for b in range(v.shape[1] // LANES):
        vb = v[:, b * LANES:(b + 1) * LANES]
        up = pltpu.roll(vb, LANES - half, axis=1)
        dn = pltpu.roll(vb, half, axis=1)
        outs.append(vb * cos_t + jnp.where(first, up, dn) * sin_t)
    return outs[0] if len(outs) == 1 else jnp.concatenate(outs, axis=1)


def _ada_kernel(cc_ref, w_ref, b_ref, o_ref):
    a = cc_ref[...]
    a = a * jax.nn.sigmoid(a)
    o_ref[0] = _dot(a.astype(BF16), w_ref[0].astype(BF16)) + b_ref[0]


def _ada_call(cc8, ada_w, ada_b):
    tn = 768
    return pl.pallas_call(
        _ada_kernel,
        grid=(DEPTH, 3 * D_MODEL // tn),
        in_specs=[
            pl.BlockSpec((8, D_MODEL), lambda l, j: (0, 0)),
            pl.BlockSpec((1, D_MODEL, tn), lambda l, j: (l, 0, j)),
            pl.BlockSpec((1, 1, tn), lambda l, j: (l, 0, j)),
        ],
        out_specs=pl.BlockSpec((1, 8, tn), lambda l, j: (l, 0, j)),
        out_shape=jax.ShapeDtypeStruct((DEPTH, 8, 3 * D_MODEL), F32),
        compiler_params=pltpu.CompilerParams(dimension_semantics=("parallel", "parallel")),
        name="ada_mod",
    )(cc8, ada_w, ada_b.reshape(DEPTH, 1, 3 * D_MODEL))


def _ctx_kernel(ckv_ref, kr4_ref, gk_ref, gv_ref, dk_ref, dv_ref, wkvb_ref, o_ref):
    kvp = _dot(ckv_ref[0, 0].astype(BF16), wkvb_ref[0])
    kr4 = kr4_ref[0, 0].astype(BF16)
    o_ref[0, 0, 0] = jnp.concatenate([kvp[:, 0:128].astype(BF16), kr4], axis=1)
    o_ref[0, 0, 1] = jnp.concatenate([kvp[:, 128:256].astype(BF16), kr4], axis=1)
    o_ref[0, 0, 2] = kvp[:, 256:512].astype(BF16)
    o_ref[0, 0, 3] = jnp.concatenate([gk_ref[0, 0].astype(BF16), gv_ref[0, 0].astype(BF16)], axis=1)
    o_ref[0, 0, 4] = dk_ref[0, 0].astype(BF16)
    o_ref[0, 0, 5] = dv_ref[0, 0].astype(BF16)


def _ctx_call(ckv, kr4, gk, gv, dk, dv, wkvb):
    nb, _, p, _ = ckv.shape

    def spec(w):
        return pl.BlockSpec((1, 1, p, w), lambda l, b: (b, l, 0, 0))

    return pl.pallas_call(
        _ctx_kernel,
        grid=(DEPTH, nb),
        in_specs=[spec(128), spec(128), spec(128), spec(128), spec(256), spec(256),
                  pl.BlockSpec((1, MLA_KV_RANK, 512), lambda l, b: (l, 0, 0))],
        out_specs=pl.BlockSpec((1, 1, N_KVBLK, p, MXU_W), lambda l, b: (l, b, 0, 0, 0)),
        out_shape=jax.ShapeDtypeStruct((DEPTH, nb, N_KVBLK, p, MXU_W), BF16),
        compiler_params=pltpu.CompilerParams(dimension_semantics=("parallel", "parallel")),
        name="ctx_kv",
    )(ckv, kr4, gk, gv, dk, dv, wkvb)


def _inproj_kernel(*refs, rope, states, n_alias):
    it = iter(refs)
    x_ref, mod_ref, gpre_ref, w_ref, wqb_ref, wkvb_ref = (next(it) for _ in range(6))
    gq_mla_ref, gkv_mla_ref, gqn_ref, gkn_ref = (next(it) for _ in range(4))
    if rope:
        cos64_ref, sin64_ref, cos32_ref, sin32_ref = (next(it) for _ in range(4))
    for _ in range(n_alias):
        next(it)
    q_ref, kv_ref, hy_ref, gate_ref = (next(it) for _ in range(4))
    if states:
        s_ckv_ref, s_kr_ref, s_gk_ref, s_gv_ref, s_dk_ref, s_dv_ref = (next(it) for _ in range(6))
        if n_alias == 0:
            for s_ref in (s_ckv_ref, s_kr_ref, s_gk_ref, s_gv_ref, s_dk_ref, s_dv_ref):
                for l in range(1, s_ref.shape[1]):
                    s_ref[0, l] = jnp.zeros(s_ref.shape[2:], F32)
    h_scr = next(it)

    x = x_ref[0]
    mod = mod_ref[0]
    shift = mod[:, 0:D_MODEL]
    scale = mod[:, D_MODEL:2 * D_MODEL]
    ms = jnp.mean(x * x, axis=-1, keepdims=True)
    h = (x * lax.rsqrt(ms + EPS) * gpre_ref[0]) * (1.0 + scale) + shift
    h_scr[...] = h.astype(BF16)

    def proj(g, n=1):
        return _dot(h_scr[...], w_ref[0, :, g * MXU_W:(g + n) * MXU_W])

    def rope32(v):
        return _rope(v, cos32_ref[...], sin32_ref[...], 8) if rope else v

    def rope64(v):
        return _rope(v, cos64_ref[...], sin64_ref[...], 16) if rope else v

    cq = _rms_full(proj(0), gq_mla_ref[0])
    q3 = _dot(cq.astype(BF16), wqb_ref[0]) * ((MLA_NOPE + MLA_ROPE) ** -0.5)
    qr = rope32(q3[:, 256:384]).astype(BF16)
    q_ref[0, 0] = jnp.concatenate([q3[:, 0:128].astype(BF16), qr], axis=1)
    q_ref[0, 1] = jnp.concatenate([q3[:, 128:256].astype(BF16), qr], axis=1)

    z1 = proj(1)
    ckv = _rms_full(z1[:, 0:128], gkv_mla_ref[0])
    kr4 = z1[:, 128:256]
    if states:
        s_ckv_ref[0, 0] = ckv
        s_kr_ref[0, 0] = kr4[:, 0:MLA_ROPE]
    kvp = _dot(ckv.astype(BF16), wkvb_ref[0])
    kr4 = rope32(kr4).astype(BF16)
    kv_ref[0, 0] = jnp.concatenate([kvp[:, 0:128].astype(BF16), kr4], axis=1)
    kv_ref[0, 1] = jnp.concatenate([kvp[:, 128:256].astype(BF16), kr4], axis=1)
    kv_ref[0, 2] = kvp[:, 256:512].astype(BF16)

    gq = rope64(_rms_heads64(proj(2), gqn_ref[0])) * (GQA_HD ** -0.5)
    q_ref[0, 2] = gq.astype(BF16)
    z3 = proj(3)
    gk = _rms_heads64(z3[:, 0:128], gkn_ref[0])
    gv = z3[:, 128:256]
    if states:
        s_gk_ref[0, 0] = gk
        s_gv_ref[0, 0] = gv
    kv_ref[0, 3] = jnp.concatenate([rope64(gk).astype(BF16), gv.astype(BF16)], axis=1)

    q_ref[0, 3] = (rope32(proj(4)) * (DIFF_HD ** -0.5)).astype(BF16)
    dk = proj(5)
    dv = proj(6)
    if states:
        s_dk_ref[0, 0] = dk
        s_dv_ref[0, 0] = dv
    kv_ref[0, 4] = rope32(dk).astype(BF16)
    kv_ref[0, 5] = dv.astype(BF16)

    for j in range(3):
        hy_ref[0, :, j * MXU_W:(j + 1) * MXU_W] = proj(7 + j).astype(BF16)
    for j in range(4):
        g = proj(10 + j)
        gate_ref[0, :, j * MXU_W:(j + 1) * MXU_W] = (g * jax.nn.sigmoid(g)).astype(BF16)


def _inproj_call(layer, x, mod, prep, tabs, *, rope, states, tm, prev_states=None):
    nb, n, _ = x.shape
    per_batch_mod = mod.shape[0] > 1
    tiles = n // tm

    def lspec(shape):
        nd = len(shape)
        return pl.BlockSpec((1,) + tuple(shape[1:]), lambda b, i: (layer,) + (0,) * (nd - 1))

    in_specs = [
        pl.BlockSpec((1, tm, D_MODEL), lambda b, i: (b, i, 0)),
        pl.BlockSpec((1, 1, 3 * D_MODEL), (lambda b, i: (b, 0, 0)) if per_batch_mod else (lambda b, i: (0, 0, 0))),
    ]
    args = [x, mod]
    for name in ("norm_pre", "w_in", "wqb", "wkvb", "mla_q_norm", "mla_kv_norm", "gqa_q_norm", "gqa_k_norm"):
        a = prep[name]
        in_specs.append(lspec(a.shape))
        args.append(a)
    if rope:
        for t in tabs:
            in_specs.append(pl.BlockSpec((tm, LANES), lambda b, i: (i, 0)))
            args.append(t)

    out_shape = [
        jax.ShapeDtypeStruct((nb, N_QBLK, n, MXU_W), BF16),
        jax.ShapeDtypeStruct((nb, N_KVBLK, n, MXU_W), BF16),
        jax.ShapeDtypeStruct((nb, n, 3 * HY_W), BF16),
        jax.ShapeDtypeStruct((nb, n, 4 * GROUP_W), BF16),
    ]
    out_specs = [
        pl.BlockSpec((1, N_QBLK, tm, MXU_W), lambda b, i: (b, 0, i, 0)),
        pl.BlockSpec((1, N_KVBLK, tm, MXU_W), lambda b, i: (b, 0, i, 0)),
        pl.BlockSpec((1, tm, 3 * HY_W), lambda b, i: (b, i, 0)),
        pl.BlockSpec((1, tm, 4 * GROUP_W), lambda b, i: (b, i, 0)),
    ]
    aliases = {}
    if states:
        for w in (MLA_KV_RANK, MLA_ROPE, 128, 128, 256, 256):
            out_shape.append(jax.ShapeDtypeStruct((nb, DEPTH, n, w), F32))
            if prev_states is None:
                out_specs.append(pl.BlockSpec((1, DEPTH, tm, w), lambda b, i: (b, 0, i, 0)))
            else:
                out_specs.append(pl.BlockSpec((1, 1, tm, w), lambda b, i: (b, layer, i, 0)))
        if prev_states is not None:
            for j, a in enumerate(prev_states):
                aliases[len(args)] = 4 + j
                in_specs.append(pl.BlockSpec(memory_space=pl.ANY))
                args.append(a)

    return pl.pallas_call(
        functools.partial(_inproj_kernel, rope=rope, states=states, n_alias=len(aliases)),
        grid=(nb, tiles),
        in_specs=in_specs,
        out_specs=out_specs,
        out_shape=out_shape,
        scratch_shapes=[pltpu.VMEM((tm, D_MODEL), BF16)],
        input_output_aliases=aliases,
        compiler_params=pltpu.CompilerParams(
            dimension_semantics=("parallel", "parallel"), vmem_limit_bytes=VMEM_LIMIT),
        name="inproj_rope" if rope else "inproj_ctx",
    )(*args)


def _chunk_loop(nchunk, body):
    if nchunk == 1:
        body(0)
    else:
        def wrapped(k, carry):
            body(k)
            return carry
        lax.fori_loop(0, nchunk, wrapped, 0)


def _hyena_kernel(hy_ref, gate_ref, cw_ref, cb_ref, feats_ref, fw1_ref, fb1_ref, fw2_ref, fb2_ref,
                  fw3_ref, freq_ref, decay_ref, bias_ref, cs_ref, ss_ref,
                  o_ref, fr_scr, fi_scr, work, hy_pad, *, n, bt):
    hp = lax.Precision.HIGHEST
    rc = min(n, HY_CHUNK)
    nchunk = n // rc
    rowc = lax.broadcasted_iota(jnp.int32, (rc, HY_W), 0)

    def rows(k):
        if nchunk == 1:
            return slice(0, rc)
        return pl.ds(pl.multiple_of(k * rc, rc), rc)

    @pl.when(pl.program_id(0) == 0)
    def _filters():
        freq = freq_ref[0]
        bias = bias_ref[0]

        def taps(k):
            ft = feats_ref[rows(k), :]
            t = ft[:, 0:1]
            h1 = jnp.sin(freq[0:1, :] * (jnp.dot(ft, fw1_ref[0], precision=hp, preferred_element_type=F32) + fb1_ref[0]))
            h2 = jnp.sin(freq[1:2, :] * (jnp.dot(h1, fw2_ref[0], precision=hp, preferred_element_type=F32) + fb2_ref[0]))
            h3 = jnp.dot(h2, fw3_ref[0], precision=hp, preferred_element_type=F32)
            dec = jnp.exp(-t * jnp.abs(decay_ref[0]))
            lag0 = (rowc + k * rc) == 0
            for o in range(2):
                hf = h3[:, o * 512:o * 512 + 256] * dec
                hb = h3[:, o * 512 + 256:o * 512 + 512] * dec
                hf = hf + jnp.where(lag0, bias[o:o + 1, :], 0.0)
                hb = jnp.where(lag0, 0.0, hb)
                work[rows(k), o * 512:o * 512 + 256] = hf.astype(BF16)
                work[rows(k), o * 512 + 256:o * 512 + 512] = hb.astype(BF16)

        _chunk_loop(nchunk, taps)

        def spectra(k):
            ft = feats_ref[rows(k), :]
            c = ft[:, HY_EMB + 7:HY_EMB + 8]
            s = ft[:, HY_EMB + 8:HY_EMB + 9]
            p = _dot(cs_ref[rows(k), :], work[...])
            q = _dot(ss_ref[rows(k), :], work[...])
            for o in range(2):
                pf, pb = p[:, o * 512:o * 512 + 256], p[:, o * 512 + 256:o * 512 + 512]
                qf, qb = q[:, o * 512:o * 512 + 256], q[:, o * 512 + 256:o * 512 + 512]
                fr_scr[o, rows(k), :] = (c * (pf + pb) + s * (qf + qb)) * (1.0 / n)
                fi_scr[o, rows(k), :] = (s * (pf - pb) - c * (qf - qb)) * (1.0 / n)

        _chunk_loop(nchunk, spectra)

    cw = cw_ref[0]
    cb = cb_ref[0]
    hy_pad[0:HY_HALO, :] = jnp.zeros((HY_HALO, 3 * HY_W), BF16)
    hy_pad[HY_HALO + n:2 * HY_HALO + n, :] = jnp.zeros((HY_HALO, 3 * HY_W), BF16)

    def short_conv(k, j):
        sl = slice(j * HY_W, (j + 1) * HY_W)
        if nchunk == 1:
            ext = hy_pad[:, sl].astype(F32)
        else:
            ext = hy_pad[pl.ds(pl.multiple_of(k * rc, rc), rc + 2 * HY_HALO), sl].astype(F32)
        um = pltpu.roll(ext, 1, axis=0)[HY_HALO:HY_HALO + rc]
        up = pltpu.roll(ext, rc + 2 * HY_HALO - 1, axis=0)[HY_HALO:HY_HALO + rc]
        u = ext[HY_HALO:HY_HALO + rc]
        return um * cw[0:1, sl] + u * cw[1:2, sl] + up * cw[2:3, sl] + cb[:, sl]

    zc, yrc, yic = slice(0, 256), slice(256, 512), slice(512, 768)

    def one(bi):
        hy_pad[HY_HALO:HY_HALO + n, :] = hy_ref[bi]

        def conv_v(k):
            work[rows(k), zc] = short_conv(k, 2).astype(BF16)

        _chunk_loop(nchunk, conv_v)
        for o in range(2):
            def forward(k, o=o):
                a = _dot(cs_ref[rows(k), :], work[:, zc])
                b = _dot(ss_ref[rows(k), :], work[:, zc])
                fr = fr_scr[o, rows(k), :]
                fi = fi_scr[o, rows(k), :]
                work[rows(k), yrc] = (a * fr + b * fi).astype(BF16)
                work[rows(k), yic] = (a * fi - b * fr).astype(BF16)

            def inverse(k, o=o):
                y = _dot(cs_ref[rows(k), :], work[:, yrc]) - _dot(ss_ref[rows(k), :], work[:, yic])
                z = short_conv(k, o) * y
                if o == 0:
                    work[rows(k), zc] = z.astype(BF16)
                else:
                    o_ref[bi, rows(k), :] = (z * gate_ref[bi, rows(k), :].astype(F32)).astype(BF16)

            _chunk_loop(nchunk, forward)
            _chunk_loop(nchunk, inverse)

    if bt == 1:
        one(0)
    else:
        def body(bi, carry):
            one(bi)
            return carry
        lax.fori_loop(0, bt, body, 0)


def _hyena_call(layer, hy, gates, prep, dft, *, bt):
    nb, n, _ = hy.shape
    cs, ss, feats = dft

    def lspec(shape):
        nd = len(shape)
        return pl.BlockSpec((1,) + tuple(shape[1:]), lambda b: (layer,) + (0,) * (nd - 1))

    def cspec(shape):
        nd = len(shape)
        return pl.BlockSpec(tuple(shape), lambda b: (0,) * nd, pipeline_mode=pl.Buffered(1))

    big = n >= 1024
    in_specs = [
        pl.BlockSpec((bt, n, 3 * HY_W), lambda b: (b, 0, 0)),
        pl.BlockSpec((bt, n, GROUP_W), lambda b: (b, 0, 3)),
    ]
    args = [hy, gates]
    for name in ("hy_conv_w", "hy_conv_b"):
        in_specs.append(lspec(prep[name].shape))
        args.append(prep[name])
    in_specs.append(cspec(feats.shape))
    args.append(feats)
    for name in ("hy_fw1", "hy_fb1", "hy_fw2", "hy_fb2", "hy_fw3", "hy_freq", "hy_decay", "hy_bias"):
        in_specs.append(lspec(prep[name].shape))
        args.append(prep[name])
    in_specs += [cspec(cs.shape), cspec(ss.shape)]
    args += [cs, ss]

    return pl.pallas_call(
        functools.partial(_hyena_kernel, n=n, bt=bt),
        grid=(nb // bt,),
        in_specs=in_specs,
        out_specs=pl.BlockSpec((bt, n, HY_W), lambda b: (b, 0, 0)),
        out_shape=jax.ShapeDtypeStruct((nb, n, HY_W), BF16),
        scratch_shapes=[pltpu.VMEM((2, n, HY_W), F32), pltpu.VMEM((2, n, HY_W), F32),
                        pltpu.VMEM((n, 4 * HY_W), BF16),
                        pltpu.VMEM((n + 2 * HY_HALO, 3 * HY_W), BF16)],
        compiler_params=pltpu.CompilerParams(
            dimension_semantics=("arbitrary",), vmem_limit_bytes=VMEM_LIMIT),
        name="hyena_long" if big else "hyena_short",
    )(*args)


def _attn_kernel(*refs, has_ctx, lam_init, tq):
    it = iter(refs)
    q_ref, kv_ref = next(it), next(it)
    ctx_ref = next(it) if has_ctx else None
    gate_ref, ohy_ref, x_ref, mod_ref, gpost_ref, wout_ref, subln_ref, lamp_ref = (next(it) for _ in range(8))
    y_ref = next(it)

    lane = lax.broadcasted_iota(jnp.int32, (tq, MXU_W), 1)

    def band(lo, width):
        return (lane >= lo) & (lane < lo + width)

    def scores(qs, kblk):
        s_l = _dot_nt(qs, kv_ref[0, kblk])
        s_c = _dot_nt(qs, ctx_ref[0, kblk]) if has_ctx else None
        return s_l, s_c

    def softmax_rows(s_l, s_c):
        m = jnp.max(s_l, axis=-1, keepdims=True)
        if has_ctx:
            m = jnp.maximum(m, jnp.max(s_c, axis=-1, keepdims=True))
            p_c = jnp.exp(s_c - m)
        p_l = jnp.exp(s_l - m)
        l = jnp.sum(p_l, axis=-1, keepdims=True)
        if has_ctx:
            l = l + jnp.sum(p_c, axis=-1, keepdims=True)
            return p_l, p_c, l
        return p_l, None, l

    def pv(p_l, p_c, vblk):
        o = _dot(p_l.astype(BF16), kv_ref[0, vblk])
        if has_ctx:
            o = o + _dot(p_c.astype(BF16), ctx_ref[0, vblk])
        return o

    def stack(parts):
        return jnp.concatenate(parts, axis=0)

    def pick_heads(blocks, width):
        out = blocks[0]
        for h in range(1, 4):
            out = jnp.where(band(width * h, width), blocks[h], out)
        return out

    p_ls, p_cs, ls = [], [], []
    for slab in range(2):
        qf = q_ref[0, slab].astype(F32)
        qs = stack([jnp.where(band(64 * j, MLA_NOPE) | band(128 + MLA_ROPE * (2 * slab + j), MLA_ROPE), qf, 0.0)
                    for j in range(2)]).astype(BF16)
        p_l, p_c, l = softmax_rows(*scores(qs, slab))
        p_ls.append(p_l)
        p_cs.append(p_c)
        ls.append(l)
    o = pv(stack(p_ls), stack(p_cs) if has_ctx else None, 2)
    o01 = o[0:2 * tq] / ls[0]
    o23 = o[2 * tq:4 * tq] / ls[1]
    o_mla = pick_heads([o01[0:tq], o01[tq:2 * tq], o23[0:tq], o23[tq:2 * tq]], MLA_V)

    qf = q_ref[0, 2].astype(F32)
    qparts = []
    for j in range(2):
        for g in range(2):
            qsel = jnp.where(band(128 * j + GQA_HD * g, GQA_HD), qf, 0.0)
            q128 = qsel[:, 0:128] + qsel[:, 128:256]
            qparts.append(jnp.concatenate([q128, jnp.zeros_like(q128)], axis=1))
    p_l, p_c, l = softmax_rows(*scores(stack(qparts).astype(BF16), 3))
    o = (pv(p_l, p_c, 3) / l)[:, 128:256]
    lo_half = lax.broadcasted_iota(jnp.int32, (tq, LANES), 1) < GQA_HD
    o_gqa = jnp.concatenate([jnp.where(lo_half, o[(2 * j) * tq:(2 * j + 1) * tq], o[(2 * j + 1) * tq:(2 * j + 2) * tq])
                             for j in range(2)], axis=1)

    lamp = lamp_ref[0]
    lam = (jnp.exp(jnp.sum(lamp[0:1, :] * lamp[1:2, :], axis=-1, keepdims=True))
           - jnp.exp(jnp.sum(lamp[2:3, :] * lamp[3:4, :], axis=-1, keepdims=True)) + lam_init)
    qf = q_ref[0, 3].astype(F32)
    qs = stack([jnp.where(band(64 * h + DIFF_HD * s, DIFF_HD), qf, 0.0)
                for s in range(2) for h in range(DIFF_HEADS)]).astype(BF16)
    s_l, s_c = scores(qs, 4)
    half = DIFF_HEADS * tq
    p1_l, p1_c, l1 = softmax_rows(s_l[0:half], s_c[0:half] if has_ctx else None)
    p2_l, p2_c, l2 = softmax_rows(s_l[half:2 * half], s_c[half:2 * half] if has_ctx else None)
    w1 = 1.0 / l1
    w2 = lam / l2
    a_l = p1_l * w1 - p2_l * w2
    a_c = (p1_c * w1 - p2_c * w2) if has_ctx else None
    o = pv(a_l, a_c, 5)
    od = pick_heads([o[h * tq:(h + 1) * tq] for h in range(DIFF_HEADS)], DIFF_VD)

    od2 = od * od
    r = jnp.zeros_like(od)
    for h in range(DIFF_HEADS):
        mh = band(DIFF_VD * h, DIFF_VD)
        sh = jnp.sum(jnp.where(mh, od2, 0.0), axis=-1, keepdims=True)
        r = jnp.where(mh, lax.rsqrt(sh * (1.0 / DIFF_VD) + EPS), r)
    od = (od * r * subln_ref[0]) * (1.0 - lam_init)

    g = gate_ref[0].astype(F32)
    ob = jnp.concatenate([
        (o_mla * g[:, 0:256]).astype(BF16),
        (o_gqa * g[:, 256:512]).astype(BF16),
        (od * g[:, 512:768]).astype(BF16),
        ohy_ref[0],
    ], axis=1)
    out = _dot(ob, wout_ref[0])
    ms = jnp.mean(out * out, axis=-1, keepdims=True)
    gate_mod = mod_ref[0][:, 2 * D_MODEL:3 * D_MODEL]
    y_ref[0] = x_ref[0] + gate_mod * (out * lax.rsqrt(ms + EPS) * gpost_ref[0])


def _attn_call(layer, q, kv, ctx, gates, ohy, x, mod, prep, *, tq):
    nb, n, _ = x.shape
    has_ctx = ctx is not None
    per_batch_mod = mod.shape[0] > 1
    lam_init = 0.8 - 0.6 * math.exp(-0.3 * layer)

    def lspec(shape):
        nd = len(shape)
        return pl.BlockSpec((1,) + tuple(shape[1:]), lambda b, i: (layer,) + (0,) * (nd - 1))

    in_specs = [
        pl.BlockSpec((1, N_QBLK, tq, MXU_W), lambda b, i: (b, 0, i, 0)),
        pl.BlockSpec((1, N_KVBLK, n, MXU_W), lambda b, i: (b, 0, 0, 0)),
    ]
    args = [q, kv]
    if has_ctx:
        p = ctx.shape[3]
        in_specs.append(pl.BlockSpec((None, 1, N_KVBLK, p, MXU_W), lambda b, i: (layer, b, 0, 0, 0)))
        args.append(ctx)
    in_specs += [
        pl.BlockSpec((1, tq, 4 * GROUP_W), lambda b, i: (b, i, 0)),
        pl.BlockSpec((1, tq, HY_W), lambda b, i: (b, i, 0)),
        pl.BlockSpec((1, tq, D_MODEL), lambda b, i: (b, i, 0)),
        pl.BlockSpec((1, 1, 3 * D_MODEL), (lambda b, i: (b, 0, 0)) if per_batch_mod else (lambda b, i: (0, 0, 0))),
    ]
    args += [gates, ohy, x, mod]
    for name in ("norm_post", "w_out", "diff_subln", "diff_lam"):
        in_specs.append(lspec(prep[name].shape))
        args.append(prep[name])

    return pl.pallas_call(
        functools.partial(_attn_kernel, has_ctx=has_ctx, lam_init=lam_init, tq=tq),
        grid=(nb, n // tq),
        in_specs=in_specs,
        out_specs=pl.BlockSpec((1, tq, D_MODEL), lambda b, i: (b, i, 0)),
        out_shape=jax.ShapeDtypeStruct((nb, n, D_MODEL), F32),
        compiler_params=pltpu.CompilerParams(
            dimension_semantics=("parallel", "parallel"), vmem_limit_bytes=VMEM_LIMIT),
        name="attn_lat" if has_ctx else "attn_ctx",
    )(*args)


def _rope_tables(n):
    tok = jnp.arange(n, dtype=jnp.int32)
    row = (tok // GRID_W).astype(F32)
    col = (tok % GRID_W).astype(F32)
    lane = jnp.arange(LANES, dtype=jnp.int32)
    tabs = []
    for d in (64, 32):
        m = d // 2
        i = lane % d
        ii = i % m
        f = ii % (m // 2)
        inv = ROPE_BASE ** (-(2 * f).astype(F32) / m)
        pos = jnp.where((i // m)[None, :] == 0, row[:, None], col[:, None])
        ang = pos * inv[None, :]
        sign = jnp.where(ii < m // 2, -1.0, 1.0).astype(F32)
        tabs += [jnp.cos(ang), jnp.sin(ang) * sign[None, :]]
    return tabs


def _dft_tables(n):
    k = jnp.arange(n, dtype=jnp.int32)
    unit = math.pi / (4 * n)
    kk = 2 * k[:, None] + 1
    ang_a = ((kk * (2 * LANES * jnp.arange(n // LANES, dtype=jnp.int32))[None, :]) % (8 * n)).astype(F32) * unit
    ang_b = ((kk * (2 * jnp.arange(LANES, dtype=jnp.int32) + 1)[None, :]) % (8 * n)).astype(F32) * unit
    ca, sa = jnp.cos(ang_a)[:, :, None], jnp.sin(ang_a)[:, :, None]
    cb, sb = jnp.cos(ang_b)[:, None, :], jnp.sin(ang_b)[:, None, :]
    cs = (ca * cb - sa * sb).reshape(n, n).astype(BF16)
    ss = (sa * cb + ca * sb).reshape(n, n).astype(BF16)
    th = (2 * k + 1).astype(F32) * (math.pi / (4 * n))
    ch = jnp.cos(th)[:, None]
    sh = jnp.sin(th)[:, None]
    t = jnp.linspace(0.0, 1.0, n, dtype=F32)[:, None]
    w = 2.0 * math.pi * jnp.arange(n, dtype=F32)[:, None] / n
    f = jnp.linspace(1e-4, HY_BANDS - 1, HY_BANDS, dtype=F32)[None, :]
    feats = jnp.concatenate([t, jnp.cos(f * w), -jnp.sin(f * w), jnp.zeros((n, 7), F32), ch, sh], axis=-1)
    feats = jnp.pad(feats, ((0, 0), (0, LANES - feats.shape[1])))
    return cs, ss, feats


def _prep_weights(p):
    offs = [0]
    for s in SPLIT_SIZES:
        offs.append(offs[-1] + s)
    (o_cq, o_ckv, o_kr, o_mg, o_gq, o_gk, o_gv, o_gg, o_dq, o_dk, o_dv, o_dg, o_hy, o_hg) = offs[:-1]
    w = p["w_in"]

    def cols(o, n):
        return w[:, :, o:o + n]

    def perm_heads(a, axis):
        parts = jnp.split(a, 4, axis=axis)
        return jnp.concatenate([parts[0], parts[2], parts[1], parts[3]], axis=axis)

    kr = cols(o_kr, 32)
    w_in = jnp.concatenate([
        cols(o_cq, 256), cols(o_ckv, 128), kr, kr, kr, kr,
        perm_heads(cols(o_gq, 256), 2), cols(o_gk, 128), cols(o_gv, 128),
        cols(o_dq, 256), cols(o_dk, 256), cols(o_dv, 256), cols(o_hy, 768),
        cols(o_mg, 256), perm_heads(cols(o_gg, 256), 2), cols(o_dg, 256), cols(o_hg, 256),
    ], axis=2).astype(BF16)

    wq = p["mla_wq_b"].reshape(DEPTH, MLA_Q_RANK, MLA_HEADS, MLA_NOPE + MLA_ROPE)
    wqb = jnp.concatenate([wq[..., :MLA_NOPE].reshape(DEPTH, MLA_Q_RANK, 256),
                           wq[..., MLA_NOPE:].reshape(DEPTH, MLA_Q_RANK, 128)], axis=2).astype(BF16)
    wk = p["mla_wkv_b"].reshape(DEPTH, MLA_KV_RANK, MLA_HEADS, MLA_NOPE + MLA_V)
    wkvb = jnp.concatenate([wk[..., :MLA_NOPE].reshape(DEPTH, MLA_KV_RANK, 256),
                            wk[..., MLA_NOPE:].reshape(DEPTH, MLA_KV_RANK, 256)], axis=2).astype(BF16)

    wo = p["w_out"]
    w_out = jnp.concatenate([wo[:, 0:256], perm_heads(wo[:, 256:512], 1), wo[:, 512:]], axis=1).astype(BF16)

    def pad_to(a, shape):
        return jnp.pad(a, [(0, s - d) for d, s in zip(a.shape, shape)])

    return {
        "norm_pre": p["norm_pre"].reshape(DEPTH, 1, D_MODEL),
        "norm_post": p["norm_post"].reshape(DEPTH, 1, D_MODEL),
        "w_in": w_in, "wqb": wqb, "wkvb": wkvb, "w_out": w_out,
        "mla_q_norm": p["mla_q_norm"].reshape(DEPTH, 1, MLA_Q_RANK),
        "mla_kv_norm": p["mla_kv_norm"].reshape(DEPTH, 1, MLA_KV_RANK),
        "gqa_q_norm": jnp.tile(p["gqa_q_norm"], (1, 4)).reshape(DEPTH, 1, 256),
        "gqa_k_norm": jnp.tile(p["gqa_k_norm"], (1, 2)).reshape(DEPTH, 1, 128),
        "diff_subln": jnp.tile(p["diff_subln"], (1, 4)).reshape(DEPTH, 1, 256),
        "diff_lam": jnp.stack([p["diff_lq1"], p["diff_lk1"], p["diff_lq2"], p["diff_lk2"]], axis=1),
        "hy_conv_w": p["hy_conv_w"],
        "hy_conv_b": p["hy_conv_b"].reshape(DEPTH, 1, 3 * HY_W),
        "hy_fw1": pad_to(p["hy_fw1"], (DEPTH, LANES, LANES)),
        "hy_fb1": pad_to(p["hy_fb1"].reshape(DEPTH, 1, HY_FFN), (DEPTH, 1, LANES)),
        "hy_fw2": pad_to(p["hy_fw2"], (DEPTH, LANES, LANES)),
        "hy_fb2": pad_to(p["hy_fb2"].reshape(DEPTH, 1, HY_FFN), (DEPTH, 1, LANES)),
        "hy_fw3": pad_to(p["hy_fw3"], (DEPTH, LANES, 4 * HY_W)),
        "hy_freq": pad_to(p["hy_freq"], (DEPTH, 2, LANES)),
        "hy_decay": p["hy_decay"].reshape(DEPTH, 1, HY_W),
        "hy_bias": p["hy_bias"],
    }


def kernel(x_prompt, x_sample, cache_mla_ckv, cache_mla_krope, cache_gqa_k, cache_gqa_v, cache_diff_k, cache_diff_v, c, c_ctx, norm_pre, norm_post, ada_w, ada_b, w_in, w_out, mla_q_norm, mla_wq_b, mla_kv_norm, mla_wkv_b, gqa_q_norm, gqa_k_norm, diff_lq1, diff_lk1, diff_lq2, diff_lk2, diff_subln, hy_conv_w, hy_conv_b, hy_fw1, hy_fb1, hy_fw2, hy_fb2, hy_fw3, hy_freq, hy_decay, hy_bias):
    nb_p, n_p, _ = x_prompt.shape
    nb_s, n_s, _ = x_sample.shape
    past = cache_mla_ckv.shape[2]

    prep = _prep_weights(dict(
        norm_pre=norm_pre, norm_post=norm_post, w_in=w_in, w_out=w_out, mla_q_norm=mla_q_norm,
        mla_wq_b=mla_wq_b, mla_kv_norm=mla_kv_norm, mla_wkv_b=mla_wkv_b, gqa_q_norm=gqa_q_norm,
        gqa_k_norm=gqa_k_norm, diff_lq1=diff_lq1, diff_lk1=diff_lk1, diff_lq2=diff_lq2, diff_lk2=diff_lk2,
        diff_subln=diff_subln, hy_conv_w=hy_conv_w, hy_conv_b=hy_conv_b, hy_fw1=hy_fw1, hy_fb1=hy_fb1,
        hy_fw2=hy_fw2, hy_fb2=hy_fb2, hy_fw3=hy_fw3, hy_freq=hy_freq, hy_decay=hy_decay, hy_bias=hy_bias))
    tabs = _rope_tables(n_s)
    dft_p = _dft_tables(n_p)
    dft_s = _dft_tables(n_s)

    cc8 = jnp.concatenate([c_ctx[None, :], c, jnp.zeros((8 - 1 - nb_s, D_MODEL), F32)], axis=0)
    mod = _ada_call(cc8, ada_w, ada_b)

    ctx = _ctx_call(
        cache_mla_ckv, jnp.tile(cache_mla_krope, (1, 1, 1, 4)),
        cache_gqa_k.reshape(nb_s, DEPTH, past, 128), cache_gqa_v.reshape(nb_s, DEPTH, past, 128),
        cache_diff_k.reshape(nb_s, DEPTH, past, 256), cache_diff_v.reshape(nb_s, DEPTH, past, 256),
        prep["wkvb"])

    y_p, y_s = x_prompt, x_sample
    st = None
    for i in range(DEPTH):
        mod_p = mod[i, 0:1].reshape(1, 1, 3 * D_MODEL)
        mod_s = mod[i, 1:1 + nb_s].reshape(nb_s, 1, 3 * D_MODEL)
        outs = _inproj_call(i, y_p, mod_p, prep, None, rope=False, states=True, tm=n_p, prev_states=st)
        q, kv, hy, gates = outs[:4]
        st = outs[4:]
        ohy = _hyena_call(i, hy, gates, prep, dft_p, bt=4)
        y_p = _attn_call(i, q, kv, None, gates, ohy, y_p, mod_p, prep, tq=n_p)
        q, kv, hy, gates = _inproj_call(i, y_s, mod_s, prep, tabs, rope=True, states=False, tm=256)
        ohy = _hyena_call(i, hy, gates, prep, dft_s, bt=1)
        y_s = _attn_call(i, q, kv, ctx, gates, ohy, y_s, mod_s, prep, tq=128)

    new_ckv, new_kr = st[0], st[1]
    new_gk = st[2].reshape(nb_p, DEPTH, n_p, GQA_KV_HEADS, GQA_HD)
    new_gv = st[3].reshape(nb_p, DEPTH, n_p, GQA_KV_HEADS, GQA_HD)
    new_dk = st[4].reshape(nb_p, DEPTH, n_p, DIFF_HEADS, 2 * DIFF_HD)
    new_dv = st[5].reshape(nb_p, DEPTH, n_p, DIFF_HEADS, DIFF_VD)
    return (y_p, y_s, new_ckv, new_kr, new_gk, new_gv, new_dk, new_dv)
```

```python
import functools
import math

import jax
import jax.numpy as jnp
from jax import lax
from jax.experimental import pallas as pl
from jax.experimental.pallas import tpu as pltpu

F32 = jnp.float32
BF16 = jnp.bfloat16

D_MODEL = 1024
DEPTH = 2
GRID_W = 64
ROPE_BASE = 10000.0
EPS = 1e-6
GROUP_W = 256
MLA_HEADS = 4
MLA_NOPE = 64
MLA_ROPE = 32
MLA_V = 64
MLA_Q_RANK = 256
MLA_KV_RANK = 128
GQA_HEADS = 4
GQA_KV_HEADS = 2
GQA_HD = 64
DIFF_HEADS = 4
DIFF_VD = 64
DIFF_HD = 32
HY_W = 256
HY_EMB = 33
HY_BANDS = 16
HY_FFN = 64
SPLIT_SIZES = (256, 128, 32, 256, 256, 128, 128, 256, 256, 256, 256, 256, 768, 256)
P_IN = sum(SPLIT_SIZES)

LANES = 128
MXU_W = 256
P_PAD = 14 * MXU_W
N_QBLK = 5
N_KVBLK = 6
LOG2E = math.log2(math.e)
INPROJ_CHUNK = 256
HY_CHUNK = 512
HY_HALO = 16
NT_DIMS = (((1,), (1,)), ((), ()))
VMEM_LIMIT = 56 * 1024 * 1024


def _dot(a, b):
    return jnp.dot(a, b, preferred_element_type=F32)


def _dot_nt(a, b):
    return lax.dot_general(a, b, NT_DIMS, preferred_element_type=F32)


def _rms_full(v, g):
    ms = jnp.mean(v * v, axis=-1, keepdims=True)
    return v * lax.rsqrt(ms + EPS) * g


def _rms_heads64(v, g):
    outs = []
    lane = lax.broadcasted_iota(jnp.int32, (v.shape[0], LANES), 1)
    lo = lane < 64
    for b in range(v.shape[1] // LANES):
        vb = v[:, b * LANES:(b + 1) * LANES]
        v2 = vb * vb
        s_lo = jnp.sum(jnp.where(lo, v2, 0.0), axis=-1, keepdims=True)
        s_hi = jnp.sum(jnp.where(lo, 0.0, v2), axis=-1, keepdims=True)
        r = jnp.where(lo, lax.rsqrt(s_lo * (1.0 / 64) + EPS), lax.rsqrt(s_hi * (1.0 / 64) + EPS))
        outs.append(vb * r)
    out = outs[0] if len(outs) == 1 else jnp.concatenate(outs, axis=1)
    return out * g


def _rope(v, cos_t, sin_t, half):
    lane = lax.broadcasted_iota(jnp.int32, (v.shape[0], LANES), 1)
    first = (lane % (2 * half)) < half
    outs = []
    for b in range(v.shape[1] // LANES):
        vb = v[:, b * LANES:(b + 1) * LANES]
        up = pltpu.roll(vb, LANES - half, axis=1)
        dn = pltpu.roll(vb, half, axis=1)
        outs.append(vb * cos_t + jnp.where(first, up, dn) * sin_t)
    return outs[0] if len(outs) == 1 else jnp.concatenate(outs, axis=1)


def _ada_kernel(cc_ref, w_ref, b_ref, o_ref):
    a = cc_ref[...]
    a = a * jax.nn.sigmoid(a)
    o_ref[0] = _dot(a.astype(BF16), w_ref[0].astype(BF16)) + b_ref[0]


def _ada_call(cc8, ada_w, ada_b):
    tn = 768
    return pl.pallas_call(
        _ada_kernel,
        grid=(DEPTH, 3 * D_MODEL // tn),
        in_specs=[
            pl.BlockSpec((8, D_MODEL), lambda l, j: (0, 0)),
            pl.BlockSpec((1, D_MODEL, tn), lambda l, j: (l, 0, j)),
            pl.BlockSpec((1, 1, tn), lambda l, j: (l, 0, j)),
        ],
        out_specs=pl.BlockSpec((1, 8, tn), lambda l, j: (l, 0, j)),
        out_shape=jax.ShapeDtypeStruct((DEPTH, 8, 3 * D_MODEL), F32),
        compiler_params=pltpu.CompilerParams(dimension_semantics=("parallel", "parallel")),
        name="ada_mod",
    )(cc8, ada_w, ada_b.reshape(DEPTH, 1, 3 * D_MODEL))


def _ctx_kernel(ckv_ref, kr4_ref, gk_ref, gv_ref, dk_ref, dv_ref, wkvb_ref, o_ref):
    kvp = _dot(ckv_ref[0, 0].astype(BF16), wkvb_ref[0])
    kr4 = kr4_ref[0, 0].astype(BF16)
    o_ref[0, 0, 0] = jnp.concatenate([kvp[:, 0:128].astype(BF16), kr4], axis=1)
    o_ref[0, 0, 1] = jnp.concatenate([kvp[:, 128:256].astype(BF16), kr4], axis=1)
    o_ref[0, 0, 2] = kvp[:, 256:512].astype(BF16)
    gv = gv_ref[0, 0].astype(BF16)
    o_ref[0, 0, 3] = jnp.concatenate([gk_ref[0, 0].astype(BF16), gv], axis=1)
    o_ref[0, 0, 4] = dk_ref[0, 0].astype(BF16)
    o_ref[0, 0, 5] = dv_ref[0, 0].astype(BF16)


def _ctx_call(ckv, kr4, gk, gv, dk, dv, wkvb):
    nb, _, p, _ = ckv.shape

    def spec(w):
        return pl.BlockSpec((1, 1, p, w), lambda l, b: (b, l, 0, 0))

    return pl.pallas_call(
        _ctx_kernel,
        grid=(DEPTH, nb),
        in_specs=[spec(128), spec(128), spec(128), spec(128), spec(256), spec(256),
                  pl.BlockSpec((1, MLA_KV_RANK, 512), lambda l, b: (l, 0, 0))],
        out_specs=pl.BlockSpec((1, 1, N_KVBLK, p, MXU_W), lambda l, b: (l, b, 0, 0, 0)),
        out_shape=jax.ShapeDtypeStruct((DEPTH, nb, N_KVBLK, p, MXU_W), BF16),
        compiler_params=pltpu.CompilerParams(dimension_semantics=("parallel", "parallel")),
        name="ctx_kv",
    )(ckv, kr4, gk, gv, dk, dv, wkvb)


def _inproj_kernel(*refs, rope, states, n_alias):
    it = iter(refs)
    x_ref, mod_ref, gpre_ref, w_ref, wqb_ref, wkvb_ref = (next(it) for _ in range(6))
    gq_mla_ref, gkv_mla_ref, gqn_ref, gkn_ref = (next(it) for _ in range(4))
    if rope:
        cos64_ref, sin64_ref, cos32_ref, sin32_ref = (next(it) for _ in range(4))
    for _ in range(n_alias):
        next(it)
    q_ref, kv_ref, hy_ref, gate_ref = (next(it) for _ in range(4))
    if states:
        s_ckv_ref, s_kr_ref, s_gk_ref, s_gv_ref, s_dk_ref, s_dv_ref = (next(it) for _ in range(6))
        if n_alias == 0:
            for s_ref in (s_ckv_ref, s_kr_ref, s_gk_ref, s_gv_ref, s_dk_ref, s_dv_ref):
                s_ref[:, 1:] = jnp.zeros((s_ref.shape[0], s_ref.shape[1] - 1) + s_ref.shape[2:], F32)
    h_scrs = list(it)

    mod = mod_ref[0]
    shift = mod[:, 0:D_MODEL]
    scale = mod[:, D_MODEL:2 * D_MODEL]
    bb, tm = x_ref.shape[0], x_ref.shape[1]
    chunks = [(bi, slice(r0, r0 + INPROJ_CHUNK)) for bi in range(bb) for r0 in range(0, tm, INPROJ_CHUNK)]

    for (bi, rs), h_scr in zip(chunks, h_scrs):
        x = x_ref[bi, rs, :]
        ms = jnp.mean(x * x, axis=-1, keepdims=True)
        h = (x * lax.rsqrt(ms + EPS) * gpre_ref[0]) * (1.0 + scale) + shift
        h_scr[...] = h.astype(BF16)

    def proj(g):
        w = w_ref[0, :, g * MXU_W:(g + 1) * MXU_W]
        parts = [_dot(h_scr[...], w) for h_scr in h_scrs]
        return parts[0] if len(parts) == 1 else jnp.concatenate(parts, axis=0)

    def put(ref, lead, v, cols=slice(None)):
        for c, (bi, rs) in enumerate(chunks):
            ref[(bi,) + lead + (rs, cols)] = v[c * INPROJ_CHUNK:(c + 1) * INPROJ_CHUNK]

    def table(ref):
        parts = [ref[rs, :] for _, rs in chunks]
        return parts[0] if len(parts) == 1 else jnp.concatenate(parts, axis=0)

    def rope32(v):
        return _rope(v, table(cos32_ref), table(sin32_ref), 8) if rope else v

    def rope64(v):
        return _rope(v, table(cos64_ref), table(sin64_ref), 16) if rope else v

    cq = _rms_full(proj(0), gq_mla_ref[0])
    q3 = _dot(cq.astype(BF16), wqb_ref[0]) * (LOG2E * (MLA_NOPE + MLA_ROPE) ** -0.5)
    qr = rope32(q3[:, 256:384]).astype(BF16)
    put(q_ref, (0,), jnp.concatenate([q3[:, 0:128].astype(BF16), qr], axis=1))
    put(q_ref, (1,), jnp.concatenate([q3[:, 128:256].astype(BF16), qr], axis=1))

    z1 = proj(1)
    ckv = _rms_full(z1[:, 0:128], gkv_mla_ref[0])
    kr4 = z1[:, 128:256]
    if states:
        put(s_ckv_ref, (0,), ckv)
        put(s_kr_ref, (0,), kr4[:, 0:MLA_ROPE])
    kvp = _dot(ckv.astype(BF16), wkvb_ref[0])
    kr4 = rope32(kr4).astype(BF16)
    put(kv_ref, (0,), jnp.concatenate([kvp[:, 0:128].astype(BF16), kr4], axis=1))
    put(kv_ref, (1,), jnp.concatenate([kvp[:, 128:256].astype(BF16), kr4], axis=1))
    put(kv_ref, (2,), kvp[:, 256:512].astype(BF16))

    gq = (rope64(_rms_heads64(proj(2), gqn_ref[0])) * (LOG2E * GQA_HD ** -0.5)).astype(BF16)
    zq = jnp.zeros((gq.shape[0], LANES), BF16)
    put(q_ref, (2,), jnp.concatenate([gq[:, 0:128], zq], axis=1))
    put(q_ref, (3,), jnp.concatenate([gq[:, 128:256], zq], axis=1))
    z3 = proj(3)
    gk = _rms_heads64(z3[:, 0:128], gkn_ref[0])
    gv = z3[:, 128:256]
    if states:
        put(s_gk_ref, (0,), gk)
        put(s_gv_ref, (0,), gv)
    put(kv_ref, (3,), jnp.concatenate([rope64(gk).astype(BF16), gv.astype(BF16)], axis=1))

    put(q_ref, (4,), (rope32(proj(4)) * (LOG2E * DIFF_HD ** -0.5)).astype(BF16))
    dk = proj(5)
    dv = proj(6)
    if states:
        put(s_dk_ref, (0,), dk)
        put(s_dv_ref, (0,), dv)
    put(kv_ref, (4,), rope32(dk).astype(BF16))
    put(kv_ref, (5,), dv.astype(BF16))

    for j in range(3):
        put(hy_ref, (), proj(7 + j).astype(BF16), slice(j * MXU_W, (j + 1) * MXU_W))
    for j in range(4):
        g = proj(10 + j)
        put(gate_ref, (), (g * jax.nn.sigmoid(g)).astype(BF16), slice(j * MXU_W, (j + 1) * MXU_W))


def _inproj_call(layer, x, mod, prep, tabs, *, rope, states, tm, bb=1, prev_states=None):
    nb, n, _ = x.shape
    per_batch_mod = mod.shape[0] > 1
    assert bb == 1 or not per_batch_mod
    tiles = n // tm

    def lspec(shape):
        nd = len(shape)
        return pl.BlockSpec((1,) + tuple(shape[1:]), lambda b, i: (layer,) + (0,) * (nd - 1))

    in_specs = [
        pl.BlockSpec((bb, tm, D_MODEL), lambda b, i: (b, i, 0)),
        pl.BlockSpec((1, 1, 3 * D_MODEL), (lambda b, i: (b, 0, 0)) if per_batch_mod else (lambda b, i: (0, 0, 0))),
    ]
    args = [x, mod]
    for name in ("norm_pre", "w_in", "wqb", "wkvb", "mla_q_norm", "mla_kv_norm", "gqa_q_norm", "gqa_k_norm"):
        a = prep[name]
        in_specs.append(lspec(a.shape))
        args.append(a)
    if rope:
        for t in tabs:
            in_specs.append(pl.BlockSpec((tm, LANES), lambda b, i: (i, 0)))
            args.append(t)

    out_shape = [
        jax.ShapeDtypeStruct((nb, N_QBLK, n, MXU_W), BF16),
        jax.ShapeDtypeStruct((nb, N_KVBLK, n, MXU_W), BF16),
        jax.ShapeDtypeStruct((nb, n, 3 * HY_W), BF16),
        jax.ShapeDtypeStruct((nb, n, 4 * GROUP_W), BF16),
    ]
    out_specs = [
        pl.BlockSpec((bb, N_QBLK, tm, MXU_W), lambda b, i: (b, 0, i, 0)),
        pl.BlockSpec((bb, N_KVBLK, tm, MXU_W), lambda b, i: (b, 0, i, 0)),
        pl.BlockSpec((bb, tm, 3 * HY_W), lambda b, i: (b, i, 0)),
        pl.BlockSpec((bb, tm, 4 * GROUP_W), lambda b, i: (b, i, 0)),
    ]
    aliases = {}
    if states:
        for w in (MLA_KV_RANK, MLA_ROPE, 128, 128, 256, 256):
            out_shape.append(jax.ShapeDtypeStruct((nb, DEPTH, n, w), F32))
            if prev_states is None:
                out_specs.append(pl.BlockSpec((bb, DEPTH, tm, w), lambda b, i: (b, 0, i, 0)))
            else:
                out_specs.append(pl.BlockSpec((bb, 1, tm, w), lambda b, i: (b, layer, i, 0)))
        if prev_states is not None:
            for j, a in enumerate(prev_states):
                aliases[len(args)] = 4 + j
                in_specs.append(pl.BlockSpec(memory_space=pl.ANY))
                args.append(a)

    return pl.pallas_call(
        functools.partial(_inproj_kernel, rope=rope, states=states, n_alias=len(aliases)),
        grid=(nb // bb, tiles),
        in_specs=in_specs,
        out_specs=out_specs,
        out_shape=out_shape,
        scratch_shapes=[pltpu.VMEM((INPROJ_CHUNK, D_MODEL), BF16)] * (bb * tm // INPROJ_CHUNK),
        input_output_aliases=aliases,
        compiler_params=pltpu.CompilerParams(
            dimension_semantics=("parallel", "parallel"), vmem_limit_bytes=VMEM_LIMIT),
        name="inproj_rope" if rope else "inproj_ctx",
    )(*args)


def _chunk_loop(nchunk, body):
    if nchunk == 1:
        body(0)
    else:
        def wrapped(k, carry):
            body(k)
            return carry
        lax.fori_loop(0, nchunk, wrapped, 0)


def _hyena_kernel(hy_ref, gate_ref, cw_ref, cb_ref, feats_ref, fw1_ref, fb1_ref, fw2_ref, fb2_ref,
                  fw3_ref, freq_ref, decay_ref, bias_ref, cs_ref, ss_ref,
                  o_ref, fr_scr, fi_scr, work, hy_pad, *, n, bt):
    hp = lax.Precision.HIGHEST
    rc = min(n, HY_CHUNK)
    nchunk = n // rc
    rowc = lax.broadcasted_iota(jnp.int32, (rc, HY_W), 0)

    def rows(k):
        if nchunk == 1:
            return slice(0, rc)
        return pl.ds(pl.multiple_of(k * rc, rc), rc)

    @pl.when(pl.program_id(0) == 0)
    def _filters():
        freq = freq_ref[0]
        bias = bias_ref[0]

        def taps(k):
            ft = feats_ref[rows(k), :]
            t = ft[:, 0:1]
            h1 = jnp.sin(freq[0:1, :] * (jnp.dot(ft, fw1_ref[0], precision=hp, preferred_element_type=F32) + fb1_ref[0]))
            h2 = jnp.sin(freq[1:2, :] * (jnp.dot(h1, fw2_ref[0], precision=hp, preferred_element_type=F32) + fb2_ref[0]))
            h3 = jnp.dot(h2, fw3_ref[0], precision=hp, preferred_element_type=F32)
            dec = jnp.exp(-t * jnp.abs(decay_ref[0]))
            lag0 = (rowc + k * rc) == 0
            for o in range(2):
                hf = h3[:, o * 512:o * 512 + 256] * dec
                hb = h3[:, o * 512 + 256:o * 512 + 512] * dec
                hf = hf + jnp.where(lag0, bias[o:o + 1, :], 0.0)
                hb = jnp.where(lag0, 0.0, hb)
                work[rows(k), o * 512:o * 512 + 256] = hf.astype(BF16)
                work[rows(k), o * 512 + 256:o * 512 + 512] = hb.astype(BF16)

        _chunk_loop(nchunk, taps)

        def spectra(k):
            ft = feats_ref[rows(k), :]
            c = ft[:, HY_EMB + 7:HY_EMB + 8]
            s = ft[:, HY_EMB + 8:HY_EMB + 9]
            p = _dot(cs_ref[rows(k), :], work[...])
            q = _dot(ss_ref[rows(k), :], work[...])
            for o in range(2):
                pf, pb = p[:, o * 512:o * 512 + 256], p[:, o * 512 + 256:o * 512 + 512]
                qf, qb = q[:, o * 512:o * 512 + 256], q[:, o * 512 + 256:o * 512 + 512]
                fr_scr[o, rows(k), :] = (c * (pf + pb) + s * (qf + qb)) * (1.0 / n)
                fi_scr[o, rows(k), :] = (s * (pf - pb) - c * (qf - qb)) * (1.0 / n)

        _chunk_loop(nchunk, spectra)

    cw = cw_ref[0]
    cb = cb_ref[0]
    hy_pad[0:HY_HALO, :] = jnp.zeros((HY_HALO, 3 * HY_W), BF16)
    hy_pad[HY_HALO + n:2 * HY_HALO + n, :] = jnp.zeros((HY_HALO, 3 * HY_W), BF16)

    def short_conv(k, j):
        sl = slice(j * HY_W, (j + 1) * HY_W)
        if nchunk == 1:
            ext = hy_pad[:, sl].astype(F32)
        else:
            ext = hy_pad[pl.ds(pl.multiple_of(k * rc, rc), rc + 2 * HY_HALO), sl].astype(F32)
        um = pltpu.roll(ext, 1, axis=0)[HY_HALO:HY_HALO + rc]
        up = pltpu.roll(ext, rc + 2 * HY_HALO - 1, axis=0)[HY_HALO:HY_HALO + rc]
        u = ext[HY_HALO:HY_HALO + rc]
        return um * cw[0:1, sl] + u * cw[1:2, sl] + up * cw[2:3, sl] + cb[:, sl]

    zc, yrc, yic = slice(0, 256), slice(256, 512), slice(512, 768)

    def one(bi):
        hy_pad[HY_HALO:HY_HALO + n, :] = hy_ref[bi]

        def conv_v(k):
            work[rows(k), zc] = short_conv(k, 2).astype(BF16)

        _chunk_loop(nchunk, conv_v)
        for o in range(2):
            def forward(k, o=o):
                a = _dot(cs_ref[rows(k), :], work[:, zc])
                b = _dot(ss_ref[rows(k), :], work[:, zc])
                fr = fr_scr[o, rows(k), :]
                fi = fi_scr[o, rows(k), :]
                work[rows(k), yrc] = (a * fr + b * fi).astype(BF16)
                work[rows(k), yic] = (a * fi - b * fr).astype(BF16)

            def inverse(k, o=o):
                y = _dot(cs_ref[rows(k), :], work[:, yrc]) - _dot(ss_ref[rows(k), :], work[:, yic])
                z = short_conv(k, o) * y
                if o == 0:
                    work[rows(k), zc] = z.astype(BF16)
                else:
                    o_ref[bi, rows(k), :] = (z * gate_ref[bi, rows(k), :].astype(F32)).astype(BF16)

            _chunk_loop(nchunk, forward)
            _chunk_loop(nchunk, inverse)

    if bt == 1:
        one(0)
    else:
        def body(bi, carry):
            one(bi)
            return carry
        lax.fori_loop(0, bt, body, 0)


def _hyena_call(layer, hy, gates, prep, dft, *, bt):
    nb, n, _ = hy.shape
    cs, ss, feats = dft

    def lspec(shape):
        nd = len(shape)
        return pl.BlockSpec((1,) + tuple(shape[1:]), lambda b: (layer,) + (0,) * (nd - 1))

    def cspec(shape):
        nd = len(shape)
        return pl.BlockSpec(tuple(shape), lambda b: (0,) * nd, pipeline_mode=pl.Buffered(1))

    big = n >= 1024
    in_specs = [
        pl.BlockSpec((bt, n, 3 * HY_W), lambda b: (b, 0, 0)),
        pl.BlockSpec((bt, n, GROUP_W), lambda b: (b, 0, 3)),
    ]
    args = [hy, gates]
    for name in ("hy_conv_w", "hy_conv_b"):
        in_specs.append(lspec(prep[name].shape))
        args.append(prep[name])
    in_specs.append(cspec(feats.shape))
    args.append(feats)
    for name in ("hy_fw1", "hy_fb1", "hy_fw2", "hy_fb2", "hy_fw3", "hy_freq", "hy_decay", "hy_bias"):
        in_specs.append(lspec(prep[name].shape))
        args.append(prep[name])
    in_specs += [cspec(cs.shape), cspec(ss.shape)]
    args += [cs, ss]

    return pl.pallas_call(
        functools.partial(_hyena_kernel, n=n, bt=bt),
        grid=(nb // bt,),
        in_specs=in_specs,
        out_specs=pl.BlockSpec((bt, n, HY_W), lambda b: (b, 0, 0)),
        out_shape=jax.ShapeDtypeStruct((nb, n, HY_W), BF16),
        scratch_shapes=[pltpu.VMEM((2, n, HY_W), F32), pltpu.VMEM((2, n, HY_W), F32),
                        pltpu.VMEM((n, 4 * HY_W), BF16),
                        pltpu.VMEM((n + 2 * HY_HALO, 3 * HY_W), BF16)],
        compiler_params=pltpu.CompilerParams(
            dimension_semantics=("arbitrary",), vmem_limit_bytes=VMEM_LIMIT),
        name="hyena_long" if big else "hyena_short",
    )(*args)


def _attn_kernel(*refs, has_ctx, lam_init, tq):
    it = iter(refs)
    q_ref, kv_ref = next(it), next(it)
    ctx_ref = next(it) if has_ctx else None
    gate_ref, ohy_ref, x_ref, mod_ref, gpost_ref, wout_ref, subln_ref, lamp_ref = (next(it) for _ in range(8))
    y_ref = next(it)

    lane = lax.broadcasted_iota(jnp.int32, (tq, MXU_W), 1)

    def band(lo, width):
        return (lane >= lo) & (lane < lo + width)

    def scores(qs, kblk):
        s_l = _dot_nt(qs, kv_ref[0, kblk])
        s_c = _dot_nt(qs, ctx_ref[0, kblk]) if has_ctx else None
        return s_l, s_c

    def softmax_rows(s_l, s_c):
        m = jnp.max(s_l, axis=-1, keepdims=True)
        if has_ctx:
            m = jnp.maximum(m, jnp.max(s_c, axis=-1, keepdims=True))
            p_c = jnp.exp2(s_c - m)
        p_l = jnp.exp2(s_l - m)
        l = jnp.sum(p_l, axis=-1, keepdims=True)
        if has_ctx:
            l = l + jnp.sum(p_c, axis=-1, keepdims=True)
            return p_l, p_c, l
        return p_l, None, l

    def pv(p_l, p_c, vblk):
        o = _dot(p_l.astype(BF16), kv_ref[0, vblk])
        if has_ctx:
            o = o + _dot(p_c.astype(BF16), ctx_ref[0, vblk])
        return o

    def stack(parts):
        return jnp.concatenate(parts, axis=0)

    def pick_heads(blocks, width):
        out = blocks[0]
        for h in range(1, 4):
            out = jnp.where(band(width * h, width), blocks[h], out)
        return out

    p_ls, p_cs, ls = [], [], []
    for slab in range(2):
        qf = q_ref[0, slab].astype(F32)
        qs = stack([jnp.where(band(64 * j, MLA_NOPE) | band(128 + MLA_ROPE * (2 * slab + j), MLA_ROPE), qf, 0.0)
                    for j in range(2)]).astype(BF16)
        p_l, p_c, l = softmax_rows(*scores(qs, slab))
        p_ls.append(p_l)
        p_cs.append(p_c)
        ls.append(l)
    o = pv(stack(p_ls), stack(p_cs) if has_ctx else None, 2)
    o01 = o[0:2 * tq] / ls[0]
    o23 = o[2 * tq:4 * tq] / ls[1]
    o_mla = pick_heads([o01[0:tq], o01[tq:2 * tq], o23[0:tq], o23[tq:2 * tq]], MLA_V)

    qparts = []
    for j in range(2):
        qf = q_ref[0, 2 + j].astype(F32)
        for g in range(2):
            qparts.append(jnp.where(band(GQA_HD * g, GQA_HD), qf, 0.0))
    p_l, p_c, l = softmax_rows(*scores(stack(qparts).astype(BF16), 3))
    o = (pv(p_l, p_c, 3) / l)[:, 128:256]
    lo_half = lax.broadcasted_iota(jnp.int32, (tq, LANES), 1) < GQA_HD
    o_gqa = jnp.concatenate([jnp.where(lo_half, o[(2 * j) * tq:(2 * j + 1) * tq], o[(2 * j + 1) * tq:(2 * j + 2) * tq])
                             for j in range(2)], axis=1)

    lamp = lamp_ref[0]
    lam = (jnp.exp(jnp.sum(lamp[0:1, :] * lamp[1:2, :], axis=-1, keepdims=True))
           - jnp.exp(jnp.sum(lamp[2:3, :] * lamp[3:4, :], axis=-1, keepdims=True)) + lam_init)
    qf = q_ref[0, 4].astype(F32)
    qs = stack([jnp.where(band(64 * h + DIFF_HD * s, DIFF_HD), qf, 0.0)
                for s in range(2) for h in range(DIFF_HEADS)]).astype(BF16)
    s_l, s_c = scores(qs, 4)
    half = DIFF_HEADS * tq
    p1_l, p1_c, l1 = softmax_rows(s_l[0:half], s_c[0:half] if has_ctx else None)
    p2_l, p2_c, l2 = softmax_rows(s_l[half:2 * half], s_c[half:2 * half] if has_ctx else None)
    w1 = 1.0 / l1
    w2 = lam / l2
    a_l = p1_l * w1 - p2_l * w2
    a_c = (p1_c * w1 - p2_c * w2) if has_ctx else None
    o = pv(a_l, a_c, 5)
    od = pick_heads([o[h * tq:(h + 1) * tq] for h in range(DIFF_HEADS)], DIFF_VD)

    y_ref[0] = _mix_out(o_mla, o_gqa, od, gate_ref[0], ohy_ref[0], x_ref[0], mod_ref[0], gpost_ref[0],
                        wout_ref[0], subln_ref[0], lam_init)


def _mix_out(o_mla, o_gqa, od, gate, ohy, x, mod, gpost, wout, subln, lam_init):
    rows = od.shape[0]
    lane = lax.broadcasted_iota(jnp.int32, (rows, MXU_W), 1)
    od2 = od * od
    r = jnp.zeros_like(od)
    for h in range(DIFF_HEADS):
        mh = (lane >= DIFF_VD * h) & (lane < DIFF_VD * (h + 1))
        sh = jnp.sum(jnp.where(mh, od2, 0.0), axis=-1, keepdims=True)
        r = jnp.where(mh, lax.rsqrt(sh * (1.0 / DIFF_VD) + EPS), r)
    od = (od * r * subln) * (1.0 - lam_init)

    g = gate.astype(F32)
    ob = jnp.concatenate([
        (o_mla * g[:, 0:256]).astype(BF16),
        (o_gqa * g[:, 256:512]).astype(BF16),
        (od * g[:, 512:768]).astype(BF16),
        ohy,
    ], axis=1)
    out = _dot(ob, wout)
    ms = jnp.mean(out * out, axis=-1, keepdims=True)
    gate_mod = mod[:, 2 * D_MODEL:3 * D_MODEL]
    return x + gate_mod * (out * lax.rsqrt(ms + EPS) * gpost)


def _attn_call(layer, q, kv, ctx, gates, ohy, x, mod, prep, *, tq):
    nb, n, _ = x.shape
    has_ctx = ctx is not None
    per_batch_mod = mod.shape[0] > 1
    lam_init = 0.8 - 0.6 * math.exp(-0.3 * layer)

    def lspec(shape):
        nd = len(shape)
        return pl.BlockSpec((1,) + tuple(shape[1:]), lambda b, i: (layer,) + (0,) * (nd - 1))

    in_specs = [
        pl.BlockSpec((1, N_QBLK, tq, MXU_W), lambda b, i: (b, 0, i, 0)),
        pl.BlockSpec((1, N_KVBLK, n, MXU_W), lambda b, i: (b, 0, 0, 0)),
    ]
    args = [q, kv]
    if has_ctx:
        p = ctx.shape[3]
        in_specs.append(pl.BlockSpec((None, 1, N_KVBLK, p, MXU_W), lambda b, i: (layer, b, 0, 0, 0)))
        args.append(ctx)
    in_specs += [
        pl.BlockSpec((1, tq, 4 * GROUP_W), lambda b, i: (b, i, 0)),
        pl.BlockSpec((1, tq, HY_W), lambda b, i: (b, i, 0)),
        pl.BlockSpec((1, tq, D_MODEL), lambda b, i: (b, i, 0)),
        pl.BlockSpec((1, 1, 3 * D_MODEL), (lambda b, i: (b, 0, 0)) if per_batch_mod else (lambda b, i: (0, 0, 0))),
    ]
    args += [gates, ohy, x, mod]
    for name in ("norm_post", "w_out", "diff_subln", "diff_lam"):
        in_specs.append(lspec(prep[name].shape))
        args.append(prep[name])

    return pl.pallas_call(
        functools.partial(_attn_kernel, has_ctx=has_ctx, lam_init=lam_init, tq=tq),
        grid=(nb, n // tq),
        in_specs=in_specs,
        out_specs=pl.BlockSpec((1, tq, D_MODEL), lambda b, i: (b, i, 0)),
        out_shape=jax.ShapeDtypeStruct((nb, n, D_MODEL), F32),
        compiler_params=pltpu.CompilerParams(
            dimension_semantics=("parallel", "parallel"), vmem_limit_bytes=VMEM_LIMIT),
        name="attn_lat" if has_ctx else "attn_ctx",
    )(*args)


def _rope_tables(n):
    tok = jnp.arange(n, dtype=jnp.int32)
    row = (tok // GRID_W).astype(F32)
    col = (tok % GRID_W).astype(F32)
    lane = jnp.arange(LANES, dtype=jnp.int32)
    tabs = []
    for d in (64, 32):
        m = d // 2
        i = lane % d
        ii = i % m
        f = ii % (m // 2)
        inv = ROPE_BASE ** (-(2 * f).astype(F32) / m)
        pos = jnp.where((i // m)[None, :] == 0, row[:, None], col[:, None])
        ang = pos * inv[None, :]
        sign = jnp.where(ii < m // 2, -1.0, 1.0).astype(F32)
        tabs += [jnp.cos(ang), jnp.sin(ang) * sign[None, :]]
    return tabs


def _dft_tables(n):
    k = jnp.arange(n, dtype=jnp.int32)
    unit = math.pi / (4 * n)
    kk = 2 * k[:, None] + 1
    ang_a = ((kk * (2 * LANES * jnp.arange(n // LANES, dtype=jnp.int32))[None, :]) % (8 * n)).astype(F32) * unit
    ang_b = ((kk * (2 * jnp.arange(LANES, dtype=jnp.int32) + 1)[None, :]) % (8 * n)).astype(F32) * unit
    ca, sa = jnp.cos(ang_a)[:, :, None], jnp.sin(ang_a)[:, :, None]
    cb, sb = jnp.cos(ang_b)[:, None, :], jnp.sin(ang_b)[:, None, :]
    cs = (ca * cb - sa * sb).reshape(n, n).astype(BF16)
    ss = (sa * cb + ca * sb).reshape(n, n).astype(BF16)
    th = (2 * k + 1).astype(F32) * (math.pi / (4 * n))
    ch = jnp.cos(th)[:, None]
    sh = jnp.sin(th)[:, None]
    t = jnp.linspace(0.0, 1.0, n, dtype=F32)[:, None]
    w = 2.0 * math.pi * jnp.arange(n, dtype=F32)[:, None] / n
    f = jnp.linspace(1e-4, HY_BANDS - 1, HY_BANDS, dtype=F32)[None, :]
    feats = jnp.concatenate([t, jnp.cos(f * w), -jnp.sin(f * w), jnp.zeros((n, 7), F32), ch, sh], axis=-1)
    feats = jnp.pad(feats, ((0, 0), (0, LANES - feats.shape[1])))
    return cs, ss, feats


def _prep_weights(p):
    offs = [0]
    for s in SPLIT_SIZES:
        offs.append(offs[-1] + s)
    (o_cq, o_ckv, o_kr, o_mg, o_gq, o_gk, o_gv, o_gg, o_dq, o_dk, o_dv, o_dg, o_hy, o_hg) = offs[:-1]
    w = p["w_in"]

    def cols(o, n):
        return w[:, :, o:o + n]

    def perm_heads(a, axis):
        parts = jnp.split(a, 4, axis=axis)
        return jnp.concatenate([parts[0], parts[2], parts[1], parts[3]], axis=axis)

    kr = cols(o_kr, 32)
    w_in = jnp.concatenate([
        cols(o_cq, 256), cols(o_ckv, 128), kr, kr, kr, kr,
        perm_heads(cols(o_gq, 256), 2), cols(o_gk, 128), cols(o_gv, 128),
        cols(o_dq, 256), cols(o_dk, 256), cols(o_dv, 256), cols(o_hy, 768),
        cols(o_mg, 256), perm_heads(cols(o_gg, 256), 2), cols(o_dg, 256), cols(o_hg, 256),
    ], axis=2).astype(BF16)

    wq = p["mla_wq_b"].reshape(DEPTH, MLA_Q_RANK, MLA_HEADS, MLA_NOPE + MLA_ROPE)
    wqb = jnp.concatenate([wq[..., :MLA_NOPE].reshape(DEPTH, MLA_Q_RANK, 256),
                           wq[..., MLA_NOPE:].reshape(DEPTH, MLA_Q_RANK, 128)], axis=2).astype(BF16)
    wk = p["mla_wkv_b"].reshape(DEPTH, MLA_KV_RANK, MLA_HEADS, MLA_NOPE + MLA_V)
    wkvb = jnp.concatenate([wk[..., :MLA_NOPE].reshape(DEPTH, MLA_KV_RANK, 256),
                            wk[..., MLA_NOPE:].reshape(DEPTH, MLA_KV_RANK, 256)], axis=2).astype(BF16)

    wo = p["w_out"]
    w_out = jnp.concatenate([wo[:, 0:256], perm_heads(wo[:, 256:512], 1), wo[:, 512:]], axis=1).astype(BF16)

    def pad_to(a, shape):
        return jnp.pad(a, [(0, s - d) for d, s in zip(a.shape, shape)])

    return {
        "norm_pre": p["norm_pre"].reshape(DEPTH, 1, D_MODEL),
        "norm_post": p["norm_post"].reshape(DEPTH, 1, D_MODEL),
        "w_in": w_in, "wqb": wqb, "wkvb": wkvb, "w_out": w_out,
        "mla_q_norm": p["mla_q_norm"].reshape(DEPTH, 1, MLA_Q_RANK),
        "mla_kv_norm": p["mla_kv_norm"].reshape(DEPTH, 1, MLA_KV_RANK),
        "gqa_q_norm": jnp.tile(p["gqa_q_norm"], (1, 4)).reshape(DEPTH, 1, 256),
        "gqa_k_norm": jnp.tile(p["gqa_k_norm"], (1, 2)).reshape(DEPTH, 1, 128),
        "diff_subln": jnp.tile(p["diff_subln"], (1, 4)).reshape(DEPTH, 1, 256),
        "diff_lam": jnp.stack([p["diff_lq1"], p["diff_lk1"], p["diff_lq2"], p["diff_lk2"]], axis=1),
        "hy_conv_w": p["hy_conv_w"],
        "hy_conv_b": p["hy_conv_b"].reshape(DEPTH, 1, 3 * HY_W),
        "hy_fw1": pad_to(p["hy_fw1"], (DEPTH, LANES, LANES)),
        "hy_fb1": pad_to(p["hy_fb1"].reshape(DEPTH, 1, HY_FFN), (DEPTH, 1, LANES)),
        "hy_fw2": pad_to(p["hy_fw2"], (DEPTH, LANES, LANES)),
        "hy_fb2": pad_to(p["hy_fb2"].reshape(DEPTH, 1, HY_FFN), (DEPTH, 1, LANES)),
        "hy_fw3": pad_to(p["hy_fw3"], (DEPTH, LANES, 4 * HY_W)),
        "hy_freq": pad_to(p["hy_freq"], (DEPTH, 2, LANES)),
        "hy_decay": p["hy_decay"].reshape(DEPTH, 1, HY_W),
        "hy_bias": p["hy_bias"],
    }


def kernel(x_prompt, x_sample, cache_mla_ckv, cache_mla_krope, cache_gqa_k, cache_gqa_v, cache_diff_k, cache_diff_v, c, c_ctx, norm_pre, norm_post, ada_w, ada_b, w_in, w_out, mla_q_norm, mla_wq_b, mla_kv_norm, mla_wkv_b, gqa_q_norm, gqa_k_norm, diff_lq1, diff_lk1, diff_lq2, diff_lk2, diff_subln, hy_conv_w, hy_conv_b, hy_fw1, hy_fb1, hy_fw2, hy_fb2, hy_fw3, hy_freq, hy_decay, hy_bias):
    nb_p, n_p, _ = x_prompt.shape
    nb_s, n_s, _ = x_sample.shape
    past = cache_mla_ckv.shape[2]

    prep = _prep_weights(dict(
        norm_pre=norm_pre, norm_post=norm_post, w_in=w_in, w_out=w_out, mla_q_norm=mla_q_norm,
        mla_wq_b=mla_wq_b, mla_kv_norm=mla_kv_norm, mla_wkv_b=mla_wkv_b, gqa_q_norm=gqa_q_norm,
        gqa_k_norm=gqa_k_norm, diff_lq1=diff_lq1, diff_lk1=diff_lk1, diff_lq2=diff_lq2, diff_lk2=diff_lk2,
        diff_subln=diff_subln, hy_conv_w=hy_conv_w, hy_conv_b=hy_conv_b, hy_fw1=hy_fw1, hy_fb1=hy_fb1,
        hy_fw2=hy_fw2, hy_fb2=hy_fb2, hy_fw3=hy_fw3, hy_freq=hy_freq, hy_decay=hy_decay, hy_bias=hy_bias))
    tabs = _rope_tables(n_s)
    dft_p = _dft_tables(n_p)
    dft_s = _dft_tables(n_s)

    cc8 = jnp.concatenate([c_ctx[None, :], c, jnp.zeros((8 - 1 - nb_s, D_MODEL), F32)], axis=0)
    mod = _ada_call(cc8, ada_w, ada_b)

    ctx = _ctx_call(
        cache_mla_ckv, jnp.tile(cache_mla_krope, (1, 1, 1, 4)),
        cache_gqa_k.reshape(nb_s, DEPTH, past, 128), cache_gqa_v.reshape(nb_s, DEPTH, past, 128),
        cache_diff_k.reshape(nb_s, DEPTH, past, 256), cache_diff_v.reshape(nb_s, DEPTH, past, 256),
        prep["wkvb"])

    y_p, y_s = x_prompt, x_sample
    st = None
    for i in range(DEPTH):
        mod_p = mod[i, 0:1].reshape(1, 1, 3 * D_MODEL)
        mod_s = mod[i, 1:1 + nb_s].reshape(nb_s, 1, 3 * D_MODEL)
        outs = _inproj_call(i, y_p, mod_p, prep, None, rope=False, states=True, tm=n_p, bb=2, prev_states=st)
        q, kv, hy, gates = outs[:4]
        st = outs[4:]
        ohy = _hyena_call(i, hy, gates, prep, dft_p, bt=4)
        y_p = _attn_call(i, q, kv, None, gates, ohy, y_p, mod_p, prep, tq=n_p)
        q, kv, hy, gates = _inproj_call(i, y_s, mod_s, prep, tabs, rope=True, states=False, tm=512)
        ohy = _hyena_call(i, hy, gates, prep, dft_s, bt=1)
        y_s = _attn_call(i, q, kv, ctx, gates, ohy, y_s, mod_s, prep, tq=128)

    new_ckv, new_kr = st[0], st[1]
    new_gk = st[2].reshape(nb_p, DEPTH, n_p, GQA_KV_HEADS, GQA_HD)
    new_gv = st[3].reshape(nb_p, DEPTH, n_p, GQA_KV_HEADS, GQA_HD)
    new_dk = st[4].reshape(nb_p, DEPTH, n_p, DIFF_HEADS, 2 * DIFF_HD)
    new_dv = st[5].reshape(nb_p, DEPTH, n_p, DIFF_HEADS, DIFF_VD)
    return (y_p, y_s, new_ckv, new_kr, new_gk, new_gv, new_dk, new_dv)
```

```python
import functools
import math

import jax
import jax.numpy as jnp
from jax import lax
from jax.experimental import pallas as pl
from jax.experimental.pallas import tpu as pltpu

F32 = jnp.float32
BF16 = jnp.bfloat16

D_MODEL = 1024
DEPTH = 2
GRID_W = 64
ROPE_BASE = 10000.0
EPS = 1e-6
GROUP_W = 256
MLA_HEADS = 4
MLA_NOPE = 64
MLA_ROPE = 32
MLA_V = 64
MLA_Q_RANK = 256
MLA_KV_RANK = 128
GQA_HEADS = 4
GQA_KV_HEADS = 2
GQA_HD = 64
DIFF_HEADS = 4
DIFF_VD = 64
DIFF_HD = 32
HY_W = 256
HY_EMB = 33
HY_BANDS = 16
HY_FFN = 64
SPLIT_SIZES = (256, 128, 32, 256, 256, 128, 128, 256, 256, 256, 256, 256, 768, 256)
P_IN = sum(SPLIT_SIZES)

LANES = 128
MXU_W = 256
P_PAD = 14 * MXU_W
N_QBLK = 5
N_KVBLK = 6
LOG2E = math.log2(math.e)
INPROJ_CHUNK = 256
HY_CHUNK = 512
HY_HALO = 16
NT_DIMS = (((1,), (1,)), ((), ()))
VMEM_LIMIT = 56 * 1024 * 1024


def _dot(a, b):
    return jnp.dot(a, b, preferred_element_type=F32)


def _dot_nt(a, b):
    return lax.dot_general(a, b, NT_DIMS, preferred_element_type=F32)


def _rms_full(v, g):
    ms = jnp.mean(v * v, axis=-1, keepdims=True)
    return v * lax.rsqrt(ms + EPS) * g


def _rms_heads64(v, g):
    outs = []
    lane = lax.broadcasted_iota(jnp.int32, (v.shape[0], LANES), 1)
    lo = lane < 64
    for b in range(v.shape[1] // LANES):
        vb = v[:, b * LANES:(b + 1) * LANES]
        v2 = vb * vb
        s_lo = jnp.sum(jnp.where(lo, v2, 0.0), axis=-1, keepdims=True)
        s_hi = jnp.sum(jnp.where(lo, 0.0, v2), axis=-1, keepdims=True)
        r = jnp.where(lo, lax.rsqrt(s_lo * (1.0 / 64) + EPS), lax.rsqrt(s_hi * (1.0 / 64) + EPS))
        outs.append(vb * r)
    out = outs[0] if len(outs) == 1 else jnp.concatenate(outs, axis=1)
    return out * g


def _rope(v, cos_t, sin_t, half):
    lane = lax.broadcasted_iota(jnp.int32, (v.shape[0], LANES), 1)
    first = (lane % (2 * half)) < half
    outs = []
    for b in range(v.shape[1] // LANES):
        vb = v[:, b * LANES:(b + 1) * LANES]
        up = pltpu.roll(vb, LANES - half, axis=1)
        dn = pltpu.roll(vb, half, axis=1)
        outs.append(vb * cos_t + jnp.where(first, up, dn) * sin_t)
    return outs[0] if len(outs) == 1 else jnp.concatenate(outs, axis=1)


def _ada_kernel(cc_ref, w_ref, b_ref, o_ref):
    a = cc_ref[...]
    a = a * jax.nn.sigmoid(a)
    o_ref[0] = _dot(a.astype(BF16), w_ref[0].astype(BF16)) + b_ref[0]


def _ada_call(cc8, ada_w, ada_b):
    tn = 768
    return pl.pallas_call(
        _ada_kernel,
        grid=(DEPTH, 3 * D_MODEL // tn),
        in_specs=[
            pl.BlockSpec((8, D_MODEL), lambda l, j: (0, 0)),
            pl.BlockSpec((1, D_MODEL, tn), lambda l, j: (l, 0, j)),
            pl.BlockSpec((1, 1, tn), lambda l, j: (l, 0, j)),
        ],
        out_specs=pl.BlockSpec((1, 8, tn), lambda l, j: (l, 0, j)),
        out_shape=jax.ShapeDtypeStruct((DEPTH, 8, 3 * D_MODEL), F32),
        compiler_params=pltpu.CompilerParams(dimension_semantics=("parallel", "parallel")),
        name="ada_mod",
    )(cc8, ada_w, ada_b.reshape(DEPTH, 1, 3 * D_MODEL))


def _ctx_kernel(ckv_ref, kr4_ref, gk_ref, gv_ref, dk_ref, dv_ref, wkvb_ref, o_ref):
    kvp = _dot(ckv_ref[0, 0].astype(BF16), wkvb_ref[0])
    kr4 = kr4_ref[0, 0].astype(BF16)
    o_ref[0, 0, 0] = jnp.concatenate([kvp[:, 0:128].astype(BF16), kr4], axis=1)
    o_ref[0, 0, 1] = jnp.concatenate([kvp[:, 128:256].astype(BF16), kr4], axis=1)
    o_ref[0, 0, 2] = kvp[:, 256:512].astype(BF16)
    gv = gv_ref[0, 0].astype(BF16)
    o_ref[0, 0, 3] = jnp.concatenate([gk_ref[0, 0].astype(BF16), gv], axis=1)
    o_ref[0, 0, 4] = dk_ref[0, 0].astype(BF16)
    o_ref[0, 0, 5] = dv_ref[0, 0].astype(BF16)


def _ctx_call(ckv, kr4, gk, gv, dk, dv, wkvb):
    nb, _, p, _ = ckv.shape

    def spec(w):
        return pl.BlockSpec((1, 1, p, w), lambda l, b: (b, l, 0, 0))

    return pl.pallas_call(
        _ctx_kernel,
        grid=(DEPTH, nb),
        in_specs=[spec(128), spec(128), spec(128), spec(128), spec(256), spec(256),
                  pl.BlockSpec((1, MLA_KV_RANK, 512), lambda l, b: (l, 0, 0))],
        out_specs=pl.BlockSpec((1, 1, N_KVBLK, p, MXU_W), lambda l, b: (l, b, 0, 0, 0)),
        out_shape=jax.ShapeDtypeStruct((DEPTH, nb, N_KVBLK, p, MXU_W), BF16),
        compiler_params=pltpu.CompilerParams(dimension_semantics=("parallel", "parallel")),
        name="ctx_kv",
    )(ckv, kr4, gk, gv, dk, dv, wkvb)


def _inproj_kernel(*refs, rope, states, n_alias):
    it = iter(refs)
    x_ref, mod_ref, gpre_ref, w_ref, wqb_ref, wkvb_ref = (next(it) for _ in range(6))
    gq_mla_ref, gkv_mla_ref, gqn_ref, gkn_ref = (next(it) for _ in range(4))
    if rope:
        cos64_ref, sin64_ref, cos32_ref, sin32_ref = (next(it) for _ in range(4))
    for _ in range(n_alias):
        next(it)
    q_ref, kv_ref, hy_ref, gate_ref = (next(it) for _ in range(4))
    if states:
        s_ckv_ref, s_kr_ref, s_gk_ref, s_gv_ref, s_dk_ref, s_dv_ref = (next(it) for _ in range(6))
        if n_alias == 0:
            for s_ref in (s_ckv_ref, s_kr_ref, s_gk_ref, s_gv_ref, s_dk_ref, s_dv_ref):
                s_ref[:, 1:] = jnp.zeros((s_ref.shape[0], s_ref.shape[1] - 1) + s_ref.shape[2:], F32)
    h_scrs = list(it)

    mod = mod_ref[0]
    shift = mod[:, 0:D_MODEL]
    scale = mod[:, D_MODEL:2 * D_MODEL]
    bb, tm = x_ref.shape[0], x_ref.shape[1]
    chunks = [(bi, slice(r0, r0 + INPROJ_CHUNK)) for bi in range(bb) for r0 in range(0, tm, INPROJ_CHUNK)]

    for (bi, rs), h_scr in zip(chunks, h_scrs):
        x = x_ref[bi, rs, :]
        ms = jnp.mean(x * x, axis=-1, keepdims=True)
        h = (x * lax.rsqrt(ms + EPS) * gpre_ref[0]) * (1.0 + scale) + shift
        h_scr[...] = h.astype(BF16)

    def proj(g):
        w = w_ref[0, :, g * MXU_W:(g + 1) * MXU_W]
        parts = [_dot(h_scr[...], w) for h_scr in h_scrs]
        return parts[0] if len(parts) == 1 else jnp.concatenate(parts, axis=0)

    def put(ref, lead, v, cols=slice(None)):
        for c, (bi, rs) in enumerate(chunks):
            ref[(bi,) + lead + (rs, cols)] = v[c * INPROJ_CHUNK:(c + 1) * INPROJ_CHUNK]

    def table(ref):
        parts = [ref[rs, :] for _, rs in chunks]
        return parts[0] if len(parts) == 1 else jnp.concatenate(parts, axis=0)

    def rope32(v):
        return _rope(v, table(cos32_ref), table(sin32_ref), 8) if rope else v

    def rope64(v):
        return _rope(v, table(cos64_ref), table(sin64_ref), 16) if rope else v

    cq = _rms_full(proj(0), gq_mla_ref[0])
    q3 = _dot(cq.astype(BF16), wqb_ref[0]) * (LOG2E * (MLA_NOPE + MLA_ROPE) ** -0.5)
    qr = rope32(q3[:, 256:384]).astype(BF16)
    put(q_ref, (0,), jnp.concatenate([q3[:, 0:128].astype(BF16), qr], axis=1))
    put(q_ref, (1,), jnp.concatenate([q3[:, 128:256].astype(BF16), qr], axis=1))

    z1 = proj(1)
    ckv = _rms_full(z1[:, 0:128], gkv_mla_ref[0])
    kr4 = z1[:, 128:256]
    if states:
        put(s_ckv_ref, (0,), ckv)
        put(s_kr_ref, (0,), kr4[:, 0:MLA_ROPE])
    kvp = _dot(ckv.astype(BF16), wkvb_ref[0])
    kr4 = rope32(kr4).astype(BF16)
    put(kv_ref, (0,), jnp.concatenate([kvp[:, 0:128].astype(BF16), kr4], axis=1))
    put(kv_ref, (1,), jnp.concatenate([kvp[:, 128:256].astype(BF16), kr4], axis=1))
    put(kv_ref, (2,), kvp[:, 256:512].astype(BF16))

    gq = (rope64(_rms_heads64(proj(2), gqn_ref[0])) * (LOG2E * GQA_HD ** -0.5)).astype(BF16)
    zq = jnp.zeros((gq.shape[0], LANES), BF16)
    put(q_ref, (2,), jnp.concatenate([gq[:, 0:128], zq], axis=1))
    put(q_ref, (3,), jnp.concatenate([gq[:, 128:256], zq], axis=1))
    z3 = proj(3)
    gk = _rms_heads64(z3[:, 0:128], gkn_ref[0])
    gv = z3[:, 128:256]
    if states:
        put(s_gk_ref, (0,), gk)
        put(s_gv_ref, (0,), gv)
    put(kv_ref, (3,), jnp.concatenate([rope64(gk).astype(BF16), gv.astype(BF16)], axis=1))

    put(q_ref, (4,), (rope32(proj(4)) * (LOG2E * DIFF_HD ** -0.5)).astype(BF16))
    dk = proj(5)
    dv = proj(6)
    if states:
        put(s_dk_ref, (0,), dk)
        put(s_dv_ref, (0,), dv)
    put(kv_ref, (4,), rope32(dk).astype(BF16))
    put(kv_ref, (5,), dv.astype(BF16))

    for j in range(3):
        put(hy_ref, (), proj(7 + j).astype(BF16), slice(j * MXU_W, (j + 1) * MXU_W))
    for j in range(4):
        g = proj(10 + j)
        put(gate_ref, (), (g * jax.nn.sigmoid(g)).astype(BF16), slice(j * MXU_W, (j + 1) * MXU_W))


def _inproj_call(layer, x, mod, prep, tabs, *, rope, states, tm, bb=1, prev_states=None):
    nb, n, _ = x.shape
    per_batch_mod = mod.shape[0] > 1
    assert bb == 1 or not per_batch_mod
    tiles = n // tm

    def lspec(shape):
        nd = len(shape)
        return pl.BlockSpec((1,) + tuple(shape[1:]), lambda b, i: (layer,) + (0,) * (nd - 1))

    in_specs = [
        pl.BlockSpec((bb, tm, D_MODEL), lambda b, i: (b, i, 0)),
        pl.BlockSpec((1, 1, 3 * D_MODEL), (lambda b, i: (b, 0, 0)) if per_batch_mod else (lambda b, i: (0, 0, 0))),
    ]
    args = [x, mod]
    for name in ("norm_pre", "w_in", "wqb", "wkvb", "mla_q_norm", "mla_kv_norm", "gqa_q_norm", "gqa_k_norm"):
        a = prep[name]
        in_specs.append(lspec(a.shape))
        args.append(a)
    if rope:
        for t in tabs:
            in_specs.append(pl.BlockSpec((tm, LANES), lambda b, i: (i, 0)))
            args.append(t)

    out_shape = [
        jax.ShapeDtypeStruct((nb, N_QBLK, n, MXU_W), BF16),
        jax.ShapeDtypeStruct((nb, N_KVBLK, n, MXU_W), BF16),
        jax.ShapeDtypeStruct((nb, n, 3 * HY_W), BF16),
        jax.ShapeDtypeStruct((nb, n, 4 * GROUP_W), BF16),
    ]
    out_specs = [
        pl.BlockSpec((bb, N_QBLK, tm, MXU_W), lambda b, i: (b, 0, i, 0)),
        pl.BlockSpec((bb, N_KVBLK, tm, MXU_W), lambda b, i: (b, 0, i, 0)),
        pl.BlockSpec((bb, tm, 3 * HY_W), lambda b, i: (b, i, 0)),
        pl.BlockSpec((bb, tm, 4 * GROUP_W), lambda b, i: (b, i, 0)),
    ]
    aliases = {}
    if states:
        for w in (MLA_KV_RANK, MLA_ROPE, 128, 128, 256, 256):
            out_shape.append(jax.ShapeDtypeStruct((nb, DEPTH, n, w), F32))
            if prev_states is None:
                out_specs.append(pl.BlockSpec((bb, DEPTH, tm, w), lambda b, i: (b, 0, i, 0)))
            else:
                out_specs.append(pl.BlockSpec((bb, 1, tm, w), lambda b, i: (b, layer, i, 0)))
        if prev_states is not None:
            for j, a in enumerate(prev_states):
                aliases[len(args)] = 4 + j
                in_specs.append(pl.BlockSpec(memory_space=pl.ANY))
                args.append(a)

    return pl.pallas_call(
        functools.partial(_inproj_kernel, rope=rope, states=states, n_alias=len(aliases)),
        grid=(nb // bb, tiles),
        in_specs=in_specs,
        out_specs=out_specs,
        out_shape=out_shape,
        scratch_shapes=[pltpu.VMEM((INPROJ_CHUNK, D_MODEL), BF16)] * (bb * tm // INPROJ_CHUNK),
        input_output_aliases=aliases,
        compiler_params=pltpu.CompilerParams(
            dimension_semantics=("parallel", "parallel"), vmem_limit_bytes=VMEM_LIMIT),
        name="inproj_rope" if rope else "inproj_ctx",
    )(*args)


def _chunk_loop(nchunk, body):
    if nchunk == 1:
        body(0)
    else:
        def wrapped(k, carry):
            body(k)
            return carry
        lax.fori_loop(0, nchunk, wrapped, 0)


def _hyena_kernel(hy_ref, gate_ref, cw_ref, cb_ref, feats_ref, fw1_ref, fb1_ref, fw2_ref, fb2_ref,
                  fw3_ref, freq_ref, decay_ref, bias_ref, cs_ref, ss_ref,
                  o_ref, fr_scr, fi_scr, work, hy_pad, *, n, bt):
    hp = lax.Precision.HIGHEST
    rc = min(n, HY_CHUNK)
    nchunk = n // rc
    rowc = lax.broadcasted_iota(jnp.int32, (rc, HY_W), 0)

    def rows(k):
        if nchunk == 1:
            return slice(0, rc)
        return pl.ds(pl.multiple_of(k * rc, rc), rc)

    @pl.when(pl.program_id(0) == 0)
    def _filters():
        freq = freq_ref[0]
        bias = bias_ref[0]

        def taps(k):
            ft = feats_ref[rows(k), :]
            t = ft[:, 0:1]
            h1 = jnp.sin(freq[0:1, :] * (jnp.dot(ft, fw1_ref[0], precision=hp, preferred_element_type=F32) + fb1_ref[0]))
            h2 = jnp.sin(freq[1:2, :] * (jnp.dot(h1, fw2_ref[0], precision=hp, preferred_element_type=F32) + fb2_ref[0]))
            h3 = jnp.dot(h2, fw3_ref[0], precision=hp, preferred_element_type=F32)
            dec = jnp.exp(-t * jnp.abs(decay_ref[0]))
            lag0 = (rowc + k * rc) == 0
            for o in range(2):
                hf = h3[:, o * 512:o * 512 + 256] * dec
                hb = h3[:, o * 512 + 256:o * 512 + 512] * dec
                hf = hf + jnp.where(lag0, bias[o:o + 1, :], 0.0)
                hb = jnp.where(lag0, 0.0, hb)
                work[rows(k), o * 512:o * 512 + 256] = hf.astype(BF16)
                work[rows(k), o * 512 + 256:o * 512 + 512] = hb.astype(BF16)

        _chunk_loop(nchunk, taps)

        def spectra(k):
            ft = feats_ref[rows(k), :]
            c = ft[:, HY_EMB + 7:HY_EMB + 8]
            s = ft[:, HY_EMB + 8:HY_EMB + 9]
            p = _dot(cs_ref[rows(k), :], work[...])
            q = _dot(ss_ref[rows(k), :], work[...])
            for o in range(2):
                pf, pb = p[:, o * 512:o * 512 + 256], p[:, o * 512 + 256:o * 512 + 512]
                qf, qb = q[:, o * 512:o * 512 + 256], q[:, o * 512 + 256:o * 512 + 512]
                fr_scr[o, rows(k), :] = (c * (pf + pb) + s * (qf + qb)) * (1.0 / n)
                fi_scr[o, rows(k), :] = (s * (pf - pb) - c * (qf - qb)) * (1.0 / n)

        _chunk_loop(nchunk, spectra)

    cw = cw_ref[0]
    cb = cb_ref[0]
    hy_pad[0:HY_HALO, :] = jnp.zeros((HY_HALO, 3 * HY_W), BF16)
    hy_pad[HY_HALO + n:2 * HY_HALO + n, :] = jnp.zeros((HY_HALO, 3 * HY_W), BF16)

    def short_conv(k, j):
        sl = slice(j * HY_W, (j + 1) * HY_W)
        if nchunk == 1:
            ext = hy_pad[:, sl].astype(F32)
        else:
            ext = hy_pad[pl.ds(pl.multiple_of(k * rc, rc), rc + 2 * HY_HALO), sl].astype(F32)
        um = pltpu.roll(ext, 1, axis=0)[HY_HALO:HY_HALO + rc]
        up = pltpu.roll(ext, rc + 2 * HY_HALO - 1, axis=0)[HY_HALO:HY_HALO + rc]
        u = ext[HY_HALO:HY_HALO + rc]
        return um * cw[0:1, sl] + u * cw[1:2, sl] + up * cw[2:3, sl] + cb[:, sl]

    zc, yrc, yic = slice(0, 256), slice(256, 512), slice(512, 768)

    def one(bi):
        hy_pad[HY_HALO:HY_HALO + n, :] = hy_ref[bi]

        def conv_v(k):
            work[rows(k), zc] = short_conv(k, 2).astype(BF16)

        _chunk_loop(nchunk, conv_v)
        for o in range(2):
            def forward(k, o=o):
                a = _dot(cs_ref[rows(k), :], work[:, zc])
                b = _dot(ss_ref[rows(k), :], work[:, zc])
                fr = fr_scr[o, rows(k), :]
                fi = fi_scr[o, rows(k), :]
                work[rows(k), yrc] = (a * fr + b * fi).astype(BF16)
                work[rows(k), yic] = (a * fi - b * fr).astype(BF16)

            def inverse(k, o=o):
                y = _dot(cs_ref[rows(k), :], work[:, yrc]) - _dot(ss_ref[rows(k), :], work[:, yic])
                z = short_conv(k, o) * y
                if o == 0:
                    work[rows(k), zc] = z.astype(BF16)
                else:
                    o_ref[bi, rows(k), :] = (z * gate_ref[bi, rows(k), :].astype(F32)).astype(BF16)

            _chunk_loop(nchunk, forward)
            _chunk_loop(nchunk, inverse)

    if bt == 1:
        one(0)
    else:
        def body(bi, carry):
            one(bi)
            return carry
        lax.fori_loop(0, bt, body, 0)


def _hyena_call(layer, hy, gates, prep, dft, *, bt):
    nb, n, _ = hy.shape
    cs, ss, feats = dft

    def lspec(shape):
        nd = len(shape)
        return pl.BlockSpec((1,) + tuple(shape[1:]), lambda b: (layer,) + (0,) * (nd - 1))

    def cspec(shape):
        nd = len(shape)
        return pl.BlockSpec(tuple(shape), lambda b: (0,) * nd, pipeline_mode=pl.Buffered(1))

    big = n >= 1024
    in_specs = [
        pl.BlockSpec((bt, n, 3 * HY_W), lambda b: (b, 0, 0)),
        pl.BlockSpec((bt, n, GROUP_W), lambda b: (b, 0, 3)),
    ]
    args = [hy, gates]
    for name in ("hy_conv_w", "hy_conv_b"):
        in_specs.append(lspec(prep[name].shape))
        args.append(prep[name])
    in_specs.append(cspec(feats.shape))
    args.append(feats)
    for name in ("hy_fw1", "hy_fb1", "hy_fw2", "hy_fb2", "hy_fw3", "hy_freq", "hy_decay", "hy_bias"):
        in_specs.append(lspec(prep[name].shape))
        args.append(prep[name])
    in_specs += [cspec(cs.shape), cspec(ss.shape)]
    args += [cs, ss]

    return pl.pallas_call(
        functools.partial(_hyena_kernel, n=n, bt=bt),
        grid=(nb // bt,),
        in_specs=in_specs,
        out_specs=pl.BlockSpec((bt, n, HY_W), lambda b: (b, 0, 0)),
        out_shape=jax.ShapeDtypeStruct((nb, n, HY_W), BF16),
        scratch_shapes=[pltpu.VMEM((2, n, HY_W), F32), pltpu.VMEM((2, n, HY_W), F32),
                        pltpu.VMEM((n, 4 * HY_W), BF16),
                        pltpu.VMEM((n + 2 * HY_HALO, 3 * HY_W), BF16)],
        compiler_params=pltpu.CompilerParams(
            dimension_semantics=("arbitrary",), vmem_limit_bytes=VMEM_LIMIT),
        name="hyena_long" if big else "hyena_short",
    )(*args)


def _attn_kernel(*refs, has_ctx, lam_init, tq):
    it = iter(refs)
    q_ref, kv_ref = next(it), next(it)
    ctx_ref = next(it) if has_ctx else None
    gate_ref, ohy_ref, x_ref, mod_ref, gpost_ref, wout_ref, subln_ref, lamp_ref = (next(it) for _ in range(8))
    y_ref = next(it)

    lane = lax.broadcasted_iota(jnp.int32, (tq, MXU_W), 1)

    def band(lo, width):
        return (lane >= lo) & (lane < lo + width)

    def scores(qs, kblk):
        s_l = _dot_nt(qs, kv_ref[0, kblk])
        s_c = _dot_nt(qs, ctx_ref[0, kblk]) if has_ctx else None
        return s_l, s_c

    def softmax_rows(s_l, s_c):
        m = jnp.max(s_l, axis=-1, keepdims=True)
        if has_ctx:
            m = jnp.maximum(m, jnp.max(s_c, axis=-1, keepdims=True))
            p_c = jnp.exp2(s_c - m)
        p_l = jnp.exp2(s_l - m)
        l = jnp.sum(p_l, axis=-1, keepdims=True)
        if has_ctx:
            l = l + jnp.sum(p_c, axis=-1, keepdims=True)
            return p_l, p_c, l
        return p_l, None, l

    def pv(p_l, p_c, vblk):
        o = _dot(p_l.astype(BF16), kv_ref[0, vblk])
        if has_ctx:
            o = o + _dot(p_c.astype(BF16), ctx_ref[0, vblk])
        return o

    def stack(parts):
        return jnp.concatenate(parts, axis=0)

    lamp = lamp_ref[0]
    lam = (jnp.exp(jnp.sum(lamp[0:1, :] * lamp[1:2, :], axis=-1, keepdims=True))
           - jnp.exp(jnp.sum(lamp[2:3, :] * lamp[3:4, :], axis=-1, keepdims=True)) + lam_init)
    lo_half = lax.broadcasted_iota(jnp.int32, (tq, LANES), 1) < GQA_HD

    units = [("mla", 0), ("mla", 1), ("gqa", 0), ("gqa", 1), ("diff", 0), ("diff", 1), ("diff", 2), ("diff", 3)]

    def unit_scores(kind, idx):
        if kind == "mla":
            qf = q_ref[0, idx].astype(F32)
            masks = [band(64 * j, MLA_NOPE) | band(128 + MLA_ROPE * (2 * idx + j), MLA_ROPE) for j in range(2)]
            kblk = idx
        elif kind == "gqa":
            qf = q_ref[0, 2 + idx].astype(F32)
            masks = [band(GQA_HD * g, GQA_HD) for g in range(2)]
            kblk = 3
        else:
            qf = q_ref[0, 4].astype(F32)
            masks = [band(64 * idx + DIFF_HD * s, DIFF_HD) for s in range(2)]
            kblk = 4
        return scores(stack([jnp.where(m, qf, 0.0) for m in masks]).astype(BF16), kblk)

    def unit_softmax(kind, s_l, s_c):
        if kind == "diff":
            p1_l, p1_c, l1 = softmax_rows(s_l[0:tq], s_c[0:tq] if has_ctx else None)
            p2_l, p2_c, l2 = softmax_rows(s_l[tq:2 * tq], s_c[tq:2 * tq] if has_ctx else None)
            w1 = 1.0 / l1
            w2 = lam / l2
            return p1_l * w1 - p2_l * w2, (p1_c * w1 - p2_c * w2) if has_ctx else None, None
        return softmax_rows(s_l, s_c)

    def unit_pv(kind, idx, p_l, p_c, l):
        if kind == "diff":
            return jnp.where(band(DIFF_VD * idx, DIFF_VD), pv(p_l, p_c, 5), 0.0)
        if kind == "mla":
            o = pv(p_l, p_c, 2) / l
            return (jnp.where(band(128 * idx, MLA_V), o[0:tq], 0.0)
                    + jnp.where(band(128 * idx + MLA_V, MLA_V), o[tq:2 * tq], 0.0))
        o = (pv(p_l, p_c, 3) / l)[:, 128:256]
        part = jnp.where(lo_half, o[0:tq], o[tq:2 * tq])
        zero = jnp.zeros_like(part)
        return jnp.concatenate([part, zero] if idx == 0 else [zero, part], axis=1)

    acc = {"mla": None, "gqa": None, "diff": None}
    sc_vals, sm_vals = {}, {}
    for t in range(len(units) + 2):
        if t < len(units):
            sc_vals[t] = unit_scores(*units[t])
        if 1 <= t <= len(units):
            sm_vals[t - 1] = unit_softmax(units[t - 1][0], *sc_vals.pop(t - 1))
        if t >= 2:
            kind, idx = units[t - 2]
            c = unit_pv(kind, idx, *sm_vals.pop(t - 2))
            acc[kind] = c if acc[kind] is None else acc[kind] + c
    o_mla, o_gqa, od = acc["mla"], acc["gqa"], acc["diff"]

    y_ref[0] = _mix_out(o_mla, o_gqa, od, gate_ref[0], ohy_ref[0], x_ref[0], mod_ref[0], gpost_ref[0],
                        wout_ref[0], subln_ref[0], lam_init)


def _mix_out(o_mla, o_gqa, od, gate, ohy, x, mod, gpost, wout, subln, lam_init):
    rows = od.shape[0]
    lane = lax.broadcasted_iota(jnp.int32, (rows, MXU_W), 1)
    od2 = od * od
    r = jnp.zeros_like(od)
    for h in range(DIFF_HEADS):
        mh = (lane >= DIFF_VD * h) & (lane < DIFF_VD * (h + 1))
        sh = jnp.sum(jnp.where(mh, od2, 0.0), axis=-1, keepdims=True)
        r = jnp.where(mh, lax.rsqrt(sh * (1.0 / DIFF_VD) + EPS), r)
    od = (od * r * subln) * (1.0 - lam_init)

    g = gate.astype(F32)
    ob = jnp.concatenate([
        (o_mla * g[:, 0:256]).astype(BF16),
        (o_gqa * g[:, 256:512]).astype(BF16),
        (od * g[:, 512:768]).astype(BF16),
        ohy,
    ], axis=1)
    out = _dot(ob, wout)
    ms = jnp.mean(out * out, axis=-1, keepdims=True)
    gate_mod = mod[:, 2 * D_MODEL:3 * D_MODEL]
    return x + gate_mod * (out * lax.rsqrt(ms + EPS) * gpost)


def _attn_call(layer, q, kv, ctx, gates, ohy, x, mod, prep, *, tq):
    nb, n, _ = x.shape
    has_ctx = ctx is not None
    per_batch_mod = mod.shape[0] > 1
    lam_init = 0.8 - 0.6 * math.exp(-0.3 * layer)

    def lspec(shape):
        nd = len(shape)
        return pl.BlockSpec((1,) + tuple(shape[1:]), lambda b, i: (layer,) + (0,) * (nd - 1))

    in_specs = [
        pl.BlockSpec((1, N_QBLK, tq, MXU_W), lambda b, i: (b, 0, i, 0)),
        pl.BlockSpec((1, N_KVBLK, n, MXU_W), lambda b, i: (b, 0, 0, 0)),
    ]
    args = [q, kv]
    if has_ctx:
        p = ctx.shape[3]
        in_specs.append(pl.BlockSpec((None, 1, N_KVBLK, p, MXU_W), lambda b, i: (layer, b, 0, 0, 0)))
        args.append(ctx)
    in_specs += [
        pl.BlockSpec((1, tq, 4 * GROUP_W), lambda b, i: (b, i, 0)),
        pl.BlockSpec((1, tq, HY_W), lambda b, i: (b, i, 0)),
        pl.BlockSpec((1, tq, D_MODEL), lambda b, i: (b, i, 0)),
        pl.BlockSpec((1, 1, 3 * D_MODEL), (lambda b, i: (b, 0, 0)) if per_batch_mod else (lambda b, i: (0, 0, 0))),
    ]
    args += [gates, ohy, x, mod]
    for name in ("norm_post", "w_out", "diff_subln", "diff_lam"):
        in_specs.append(lspec(prep[name].shape))
        args.append(prep[name])

    return pl.pallas_call(
        functools.partial(_attn_kernel, has_ctx=has_ctx, lam_init=lam_init, tq=tq),
        grid=(nb, n // tq),
        in_specs=in_specs,
        out_specs=pl.BlockSpec((1, tq, D_MODEL), lambda b, i: (b, i, 0)),
        out_shape=jax.ShapeDtypeStruct((nb, n, D_MODEL), F32),
        compiler_params=pltpu.CompilerParams(
            dimension_semantics=("parallel", "parallel"), vmem_limit_bytes=VMEM_LIMIT),
        name="attn_lat" if has_ctx else "attn_ctx",
    )(*args)


def _rope_tables(n):
    tok = jnp.arange(n, dtype=jnp.int32)
    row = (tok // GRID_W).astype(F32)
    col = (tok % GRID_W).astype(F32)
    lane = jnp.arange(LANES, dtype=jnp.int32)
    tabs = []
    for d in (64, 32):
        m = d // 2
        i = lane % d
        ii = i % m
        f = ii % (m // 2)
        inv = ROPE_BASE ** (-(2 * f).astype(F32) / m)
        pos = jnp.where((i // m)[None, :] == 0, row[:, None], col[:, None])
        ang = pos * inv[None, :]
        sign = jnp.where(ii < m // 2, -1.0, 1.0).astype(F32)
        tabs += [jnp.cos(ang), jnp.sin(ang) * sign[None, :]]
    return tabs


def _dft_tables(n):
    k = jnp.arange(n, dtype=jnp.int32)
    unit = math.pi / (4 * n)
    kk = 2 * k[:, None] + 1
    ang_a = ((kk * (2 * LANES * jnp.arange(n // LANES, dtype=jnp.int32))[None, :]) % (8 * n)).astype(F32) * unit
    ang_b = ((kk * (2 * jnp.arange(LANES, dtype=jnp.int32) + 1)[None, :]) % (8 * n)).astype(F32) * unit
    ca, sa = jnp.cos(ang_a)[:, :, None], jnp.sin(ang_a)[:, :, None]
    cb, sb = jnp.cos(ang_b)[:, None, :], jnp.sin(ang_b)[:, None, :]
    cs = (ca * cb - sa * sb).reshape(n, n).astype(BF16)
    ss = (sa * cb + ca * sb).reshape(n, n).astype(BF16)
    th = (2 * k + 1).astype(F32) * (math.pi / (4 * n))
    ch = jnp.cos(th)[:, None]
    sh = jnp.sin(th)[:, None]
    t = jnp.linspace(0.0, 1.0, n, dtype=F32)[:, None]
    w = 2.0 * math.pi * jnp.arange(n, dtype=F32)[:, None] / n
    f = jnp.linspace(1e-4, HY_BANDS - 1, HY_BANDS, dtype=F32)[None, :]
    feats = jnp.concatenate([t, jnp.cos(f * w), -jnp.sin(f * w), jnp.zeros((n, 7), F32), ch, sh], axis=-1)
    feats = jnp.pad(feats, ((0, 0), (0, LANES - feats.shape[1])))
    return cs, ss, feats


def _prep_weights(p):
    offs = [0]
    for s in SPLIT_SIZES:
        offs.append(offs[-1] + s)
    (o_cq, o_ckv, o_kr, o_mg, o_gq, o_gk, o_gv, o_gg, o_dq, o_dk, o_dv, o_dg, o_hy, o_hg) = offs[:-1]
    w = p["w_in"]

    def cols(o, n):
        return w[:, :, o:o + n]

    def perm_heads(a, axis):
        parts = jnp.split(a, 4, axis=axis)
        return jnp.concatenate([parts[0], parts[2], parts[1], parts[3]], axis=axis)

    kr = cols(o_kr, 32)
    w_in = jnp.concatenate([
        cols(o_cq, 256), cols(o_ckv, 128), kr, kr, kr, kr,
        perm_heads(cols(o_gq, 256), 2), cols(o_gk, 128), cols(o_gv, 128),
        cols(o_dq, 256), cols(o_dk, 256), cols(o_dv, 256), cols(o_hy, 768),
        cols(o_mg, 256), perm_heads(cols(o_gg, 256), 2), cols(o_dg, 256), cols(o_hg, 256),
    ], axis=2).astype(BF16)

    wq = p["mla_wq_b"].reshape(DEPTH, MLA_Q_RANK, MLA_HEADS, MLA_NOPE + MLA_ROPE)
    wqb = jnp.concatenate([wq[..., :MLA_NOPE].reshape(DEPTH, MLA_Q_RANK, 256),
                           wq[..., MLA_NOPE:].reshape(DEPTH, MLA_Q_RANK, 128)], axis=2).astype(BF16)
    wk = p["mla_wkv_b"].reshape(DEPTH, MLA_KV_RANK, MLA_HEADS, MLA_NOPE + MLA_V)
    wkvb = jnp.concatenate([wk[..., :MLA_NOPE].reshape(DEPTH, MLA_KV_RANK, 256),
                            wk[..., MLA_NOPE:].reshape(DEPTH, MLA_KV_RANK, 256)], axis=2).astype(BF16)

    wo = p["w_out"]
    w_out = jnp.concatenate([wo[:, 0:256], perm_heads(wo[:, 256:512], 1), wo[:, 512:]], axis=1).astype(BF16)

    def pad_to(a, shape):
        return jnp.pad(a, [(0, s - d) for d, s in zip(a.shape, shape)])

    return {
        "norm_pre": p["norm_pre"].reshape(DEPTH, 1, D_MODEL),
        "norm_post": p["norm_post"].reshape(DEPTH, 1, D_MODEL),
        "w_in": w_in, "wqb": wqb, "wkvb": wkvb, "w_out": w_out,
        "mla_q_norm": p["mla_q_norm"].reshape(DEPTH, 1, MLA_Q_RANK),
        "mla_kv_norm": p["mla_kv_norm"].reshape(DEPTH, 1, MLA_KV_RANK),
        "gqa_q_norm": jnp.tile(p["gqa_q_norm"], (1, 4)).reshape(DEPTH, 1, 256),
        "gqa_k_norm": jnp.tile(p["gqa_k_norm"], (1, 2)).reshape(DEPTH, 1, 128),
        "diff_subln": jnp.tile(p["diff_subln"], (1, 4)).reshape(DEPTH, 1, 256),
        "diff_lam": jnp.stack([p["diff_lq1"], p["diff_lk1"], p["diff_lq2"], p["diff_lk2"]], axis=1),
        "hy_conv_w": p["hy_conv_w"],
        "hy_conv_b": p["hy_conv_b"].reshape(DEPTH, 1, 3 * HY_W),
        "hy_fw1": pad_to(p["hy_fw1"], (DEPTH, LANES, LANES)),
        "hy_fb1": pad_to(p["hy_fb1"].reshape(DEPTH, 1, HY_FFN), (DEPTH, 1, LANES)),
        "hy_fw2": pad_to(p["hy_fw2"], (DEPTH, LANES, LANES)),
        "hy_fb2": pad_to(p["hy_fb2"].reshape(DEPTH, 1, HY_FFN), (DEPTH, 1, LANES)),
        "hy_fw3": pad_to(p["hy_fw3"], (DEPTH, LANES, 4 * HY_W)),
        "hy_freq": pad_to(p["hy_freq"], (DEPTH, 2, LANES)),
        "hy_decay": p["hy_decay"].reshape(DEPTH, 1, HY_W),
        "hy_bias": p["hy_bias"],
    }


def kernel(x_prompt, x_sample, cache_mla_ckv, cache_mla_krope, cache_gqa_k, cache_gqa_v, cache_diff_k, cache_diff_v, c, c_ctx, norm_pre, norm_post, ada_w, ada_b, w_in, w_out, mla_q_norm, mla_wq_b, mla_kv_norm, mla_wkv_b, gqa_q_norm, gqa_k_norm, diff_lq1, diff_lk1, diff_lq2, diff_lk2, diff_subln, hy_conv_w, hy_conv_b, hy_fw1, hy_fb1, hy_fw2, hy_fb2, hy_fw3, hy_freq, hy_decay, hy_bias):
    nb_p, n_p, _ = x_prompt.shape
    nb_s, n_s, _ = x_sample.shape
    past = cache_mla_ckv.shape[2]

    prep = _prep_weights(dict(
        norm_pre=norm_pre, norm_post=norm_post, w_in=w_in, w_out=w_out, mla_q_norm=mla_q_norm,
        mla_wq_b=mla_wq_b, mla_kv_norm=mla_kv_norm, mla_wkv_b=mla_wkv_b, gqa_q_norm=gqa_q_norm,
        gqa_k_norm=gqa_k_norm, diff_lq1=diff_lq1, diff_lk1=diff_lk1, diff_lq2=diff_lq2, diff_lk2=diff_lk2,
        diff_subln=diff_subln, hy_conv_w=hy_conv_w, hy_conv_b=hy_conv_b, hy_fw1=hy_fw1, hy_fb1=hy_fb1,
        hy_fw2=hy_fw2, hy_fb2=hy_fb2, hy_fw3=hy_fw3, hy_freq=hy_freq, hy_decay=hy_decay, hy_bias=hy_bias))
    tabs = _rope_tables(n_s)
    dft_p = _dft_tables(n_p)
    dft_s = _dft_tables(n_s)

    cc8 = jnp.concatenate([c_ctx[None, :], c, jnp.zeros((8 - 1 - nb_s, D_MODEL), F32)], axis=0)
    mod = _ada_call(cc8, ada_w, ada_b)

    ctx = _ctx_call(
        cache_mla_ckv, jnp.tile(cache_mla_krope, (1, 1, 1, 4)),
        cache_gqa_k.reshape(nb_s, DEPTH, past, 128), cache_gqa_v.reshape(nb_s, DEPTH, past, 128),
        cache_diff_k.reshape(nb_s, DEPTH, past, 256), cache_diff_v.reshape(nb_s, DEPTH, past, 256),
        prep["wkvb"])

    y_p, y_s = x_prompt, x_sample
    st = None
    for i in range(DEPTH):
        mod_p = mod[i, 0:1].reshape(1, 1, 3 * D_MODEL)
        mod_s = mod[i, 1:1 + nb_s].reshape(nb_s, 1, 3 * D_MODEL)
        outs = _inproj_call(i, y_p, mod_p, prep, None, rope=False, states=True, tm=n_p, bb=2, prev_states=st)
        q, kv, hy, gates = outs[:4]
        st = outs[4:]
        ohy = _hyena_call(i, hy, gates, prep, dft_p, bt=4)
        y_p = _attn_call(i, q, kv, None, gates, ohy, y_p, mod_p, prep, tq=n_p)
        q, kv, hy, gates = _inproj_call(i, y_s, mod_s, prep, tabs, rope=True, states=False, tm=512)
        ohy = _hyena_call(i, hy, gates, prep, dft_s, bt=1)
        y_s = _attn_call(i, q, kv, ctx, gates, ohy, y_s, mod_s, prep, tq=128)

    new_ckv, new_kr = st[0], st[1]
    new_gk = st[2].reshape(nb_p, DEPTH, n_p, GQA_KV_HEADS, GQA_HD)
    new_gv = st[3].reshape(nb_p, DEPTH, n_p, GQA_KV_HEADS, GQA_HD)
    new_dk = st[4].reshape(nb_p, DEPTH, n_p, DIFF_HEADS, 2 * DIFF_HD)
    new_dv = st[5].reshape(nb_p, DEPTH, n_p, DIFF_HEADS, DIFF_VD)
    return (y_p, y_s, new_ckv, new_kr, new_gk, new_gv, new_dk, new_dv)
```

```python
import functools
import math

import jax
import jax.numpy as jnp
from jax import lax
from jax.experimental import pallas as pl
from jax.experimental.pallas import tpu as pltpu

F32 = jnp.float32
BF16 = jnp.bfloat16

D_MODEL = 1024
DEPTH = 2
GRID_W = 64
ROPE_BASE = 10000.0
EPS = 1e-6
GROUP_W = 256
MLA_HEADS = 4
MLA_NOPE = 64
MLA_ROPE = 32
MLA_V = 64
MLA_Q_RANK = 256
MLA_KV_RANK = 128
GQA_HEADS = 4
GQA_KV_HEADS = 2
GQA_HD = 64
DIFF_HEADS = 4
DIFF_VD = 64
DIFF_HD = 32
HY_W = 256
HY_EMB = 33
HY_BANDS = 16
HY_FFN = 64
SPLIT_SIZES = (256, 128, 32, 256, 256, 128, 128, 256, 256, 256, 256, 256, 768, 256)
P_IN = sum(SPLIT_SIZES)

LANES = 128
MXU_W = 256
P_PAD = 14 * MXU_W
N_QBLK = 5
N_KVBLK = 6
LOG2E = math.log2(math.e)
INPROJ_CHUNK = 256
HY_CHUNK = 512
HY_HALO = 16
ATT_SCORE_LEAD = 2
ATT_PV_LAG = 2
NT_DIMS = (((1,), (1,)), ((), ()))
VMEM_LIMIT = 56 * 1024 * 1024


def _dot(a, b):
    return jnp.dot(a, b, preferred_element_type=F32)


def _dot_nt(a, b):
    return lax.dot_general(a, b, NT_DIMS, preferred_element_type=F32)


def _dot_split(a, b):
    a_hi = a.astype(BF16)
    b_hi = b.astype(BF16)
    a_lo = (a - a_hi.astype(F32)).astype(BF16)
    b_lo = (b - b_hi.astype(F32)).astype(BF16)
    return _dot(a_hi, b_hi) + _dot(a_hi, b_lo) + _dot(a_lo, b_hi)


def _rms_full(v, g):
    ms = jnp.mean(v * v, axis=-1, keepdims=True)
    return v * lax.rsqrt(ms + EPS) * g


def _rms_heads64(v, g):
    outs = []
    lane = lax.broadcasted_iota(jnp.int32, (v.shape[0], LANES), 1)
    lo = lane < 64
    for b in range(v.shape[1] // LANES):
        vb = v[:, b * LANES:(b + 1) * LANES]
        v2 = vb * vb
        s_lo = jnp.sum(jnp.where(lo, v2, 0.0), axis=-1, keepdims=True)
        s_hi = jnp.sum(jnp.where(lo, 0.0, v2), axis=-1, keepdims=True)
        r = jnp.where(lo, lax.rsqrt(s_lo * (1.0 / 64) + EPS), lax.rsqrt(s_hi * (1.0 / 64) + EPS))
        outs.append(vb * r)
    out = outs[0] if len(outs) == 1 else jnp.concatenate(outs, axis=1)
    return out * g


def _rope(v, cos_t, sin_t, half):
    lane = lax.broadcasted_iota(jnp.int32, (v.shape[0], LANES), 1)
    first = (lane % (2 * half)) < half
    outs = []
    for b in range(v.shape[1] // LANES):
        vb = v[:, b * LANES:(b + 1) * LANES]
        up = pltpu.roll(vb, LANES - half, axis=1)
        dn = pltpu.roll(vb, half, axis=1)
        outs.append(vb * cos_t + jnp.where(first, up, dn) * sin_t)
    return outs[0] if len(outs) == 1 else jnp.concatenate(outs, axis=1)


def _ada_kernel(cc_ref, w_ref, b_ref, o_ref):
    a = cc_ref[...]
    a = a * jax.nn.sigmoid(a)
    o_ref[0] = _dot(a.astype(BF16), w_ref[0].astype(BF16)) + b_ref[0]


def _ada_call(cc8, ada_w, ada_b):
    tn = 768
    return pl.pallas_call(
        _ada_kernel,
        grid=(DEPTH, 3 * D_MODEL // tn),
        in_specs=[
            pl.BlockSpec((8, D_MODEL), lambda l, j: (0, 0)),
            pl.BlockSpec((1, D_MODEL, tn), lambda l, j: (l, 0, j)),
            pl.BlockSpec((1, 1, tn), lambda l, j: (l, 0, j)),
        ],
        out_specs=pl.BlockSpec((1, 8, tn), lambda l, j: (l, 0, j)),
        out_shape=jax.ShapeDtypeStruct((DEPTH, 8, 3 * D_MODEL), F32),
        compiler_params=pltpu.CompilerParams(dimension_semantics=("parallel", "parallel")),
        name="ada_mod",
    )(cc8, ada_w, ada_b.reshape(DEPTH, 1, 3 * D_MODEL))


def _ctx_kernel(ckv_ref, kr4_ref, gk_ref, gv_ref, dk_ref, dv_ref, wkvb_ref, o_ref):
    kvp = _dot(ckv_ref[0, 0].astype(BF16), wkvb_ref[0])
    kr4 = kr4_ref[0, 0].astype(BF16)
    o_ref[0, 0, 0] = jnp.concatenate([kvp[:, 0:128].astype(BF16), kr4], axis=1)
    o_ref[0, 0, 1] = jnp.concatenate([kvp[:, 128:256].astype(BF16), kr4], axis=1)
    o_ref[0, 0, 2] = kvp[:, 256:512].astype(BF16)
    gv = gv_ref[0, 0].astype(BF16)
    o_ref[0, 0, 3] = jnp.concatenate([gk_ref[0, 0].astype(BF16), gv], axis=1)
    o_ref[0, 0, 4] = dk_ref[0, 0].astype(BF16)
    o_ref[0, 0, 5] = dv_ref[0, 0].astype(BF16)


def _ctx_call(ckv, kr4, gk, gv, dk, dv, wkvb):
    nb, _, p, _ = ckv.shape

    def spec(w):
        return pl.BlockSpec((1, 1, p, w), lambda l, b: (b, l, 0, 0))

    return pl.pallas_call(
        _ctx_kernel,
        grid=(DEPTH, nb),
        in_specs=[spec(128), spec(128), spec(128), spec(128), spec(256), spec(256),
                  pl.BlockSpec((1, MLA_KV_RANK, 512), lambda l, b: (l, 0, 0))],
        out_specs=pl.BlockSpec((1, 1, N_KVBLK, p, MXU_W), lambda l, b: (l, b, 0, 0, 0)),
        out_shape=jax.ShapeDtypeStruct((DEPTH, nb, N_KVBLK, p, MXU_W), BF16),
        compiler_params=pltpu.CompilerParams(dimension_semantics=("parallel", "parallel")),
        name="ctx_kv",
    )(ckv, kr4, gk, gv, dk, dv, wkvb)


def _inproj_kernel(*refs, rope, states, n_alias):
    it = iter(refs)
    x_ref, mod_ref, gpre_ref, w_ref, wqb_ref, wkvb_ref = (next(it) for _ in range(6))
    gq_mla_ref, gkv_mla_ref, gqn_ref, gkn_ref = (next(it) for _ in range(4))
    if rope:
        cos64_ref, sin64_ref, cos32_ref, sin32_ref = (next(it) for _ in range(4))
    for _ in range(n_alias):
        next(it)
    q_ref, kv_ref, hy_ref, gate_ref = (next(it) for _ in range(4))
    if states:
        s_ckv_ref, s_kr_ref, s_gk_ref, s_gv_ref, s_dk_ref, s_dv_ref = (next(it) for _ in range(6))
        if n_alias == 0:
            for s_ref in (s_ckv_ref, s_kr_ref, s_gk_ref, s_gv_ref, s_dk_ref, s_dv_ref):
                s_ref[:, 1:] = jnp.zeros((s_ref.shape[0], s_ref.shape[1] - 1) + s_ref.shape[2:], F32)
    h_scrs = list(it)

    mod = mod_ref[0]
    shift = mod[:, 0:D_MODEL]
    scale = mod[:, D_MODEL:2 * D_MODEL]
    bb, tm = x_ref.shape[0], x_ref.shape[1]
    chunks = [(bi, slice(r0, r0 + INPROJ_CHUNK)) for bi in range(bb) for r0 in range(0, tm, INPROJ_CHUNK)]

    for (bi, rs), h_scr in zip(chunks, h_scrs):
        x = x_ref[bi, rs, :]
        ms = jnp.mean(x * x, axis=-1, keepdims=True)
        h = (x * lax.rsqrt(ms + EPS) * gpre_ref[0]) * (1.0 + scale) + shift
        h_scr[...] = h.astype(BF16)

    def proj(g):
        w = w_ref[0, :, g * MXU_W:(g + 1) * MXU_W]
        parts = [_dot(h_scr[...], w) for h_scr in h_scrs]
        return parts[0] if len(parts) == 1 else jnp.concatenate(parts, axis=0)

    def put(ref, lead, v, cols=slice(None)):
        for c, (bi, rs) in enumerate(chunks):
            ref[(bi,) + lead + (rs, cols)] = v[c * INPROJ_CHUNK:(c + 1) * INPROJ_CHUNK]

    def table(ref):
        parts = [ref[rs, :] for _, rs in chunks]
        return parts[0] if len(parts) == 1 else jnp.concatenate(parts, axis=0)

    def rope32(v):
        return _rope(v, table(cos32_ref), table(sin32_ref), 8) if rope else v

    def rope64(v):
        return _rope(v, table(cos64_ref), table(sin64_ref), 16) if rope else v

    cq = _rms_full(proj(0), gq_mla_ref[0])
    q3 = _dot(cq.astype(BF16), wqb_ref[0]) * (LOG2E * (MLA_NOPE + MLA_ROPE) ** -0.5)
    qr = rope32(q3[:, 256:384]).astype(BF16)
    put(q_ref, (0,), jnp.concatenate([q3[:, 0:128].astype(BF16), qr], axis=1))
    put(q_ref, (1,), jnp.concatenate([q3[:, 128:256].astype(BF16), qr], axis=1))

    z1 = proj(1)
    ckv = _rms_full(z1[:, 0:128], gkv_mla_ref[0])
    kr4 = z1[:, 128:256]
    if states:
        put(s_ckv_ref, (0,), ckv)
        put(s_kr_ref, (0,), kr4[:, 0:MLA_ROPE])
    kvp = _dot(ckv.astype(BF16), wkvb_ref[0])
    kr4 = rope32(kr4).astype(BF16)
    put(kv_ref, (0,), jnp.concatenate([kvp[:, 0:128].astype(BF16), kr4], axis=1))
    put(kv_ref, (1,), jnp.concatenate([kvp[:, 128:256].astype(BF16), kr4], axis=1))
    put(kv_ref, (2,), kvp[:, 256:512].astype(BF16))

    gq = (rope64(_rms_heads64(proj(2), gqn_ref[0])) * (LOG2E * GQA_HD ** -0.5)).astype(BF16)
    zq = jnp.zeros((gq.shape[0], LANES), BF16)
    put(q_ref, (2,), jnp.concatenate([gq[:, 0:128], zq], axis=1))
    put(q_ref, (3,), jnp.concatenate([gq[:, 128:256], zq], axis=1))
    z3 = proj(3)
    gk = _rms_heads64(z3[:, 0:128], gkn_ref[0])
    gv = z3[:, 128:256]
    if states:
        put(s_gk_ref, (0,), gk)
        put(s_gv_ref, (0,), gv)
    put(kv_ref, (3,), jnp.concatenate([rope64(gk).astype(BF16), gv.astype(BF16)], axis=1))

    put(q_ref, (4,), (rope32(proj(4)) * (LOG2E * DIFF_HD ** -0.5)).astype(BF16))
    dk = proj(5)
    dv = proj(6)
    if states:
        put(s_dk_ref, (0,), dk)
        put(s_dv_ref, (0,), dv)
    put(kv_ref, (4,), rope32(dk).astype(BF16))
    put(kv_ref, (5,), dv.astype(BF16))

    for j in range(3):
        put(hy_ref, (), proj(7 + j).astype(BF16), slice(j * MXU_W, (j + 1) * MXU_W))
    for j in range(4):
        g = proj(10 + j)
        put(gate_ref, (), (g * jax.nn.sigmoid(g)).astype(BF16), slice(j * MXU_W, (j + 1) * MXU_W))


def _inproj_call(layer, x, mod, prep, tabs, *, rope, states, tm, bb=1, prev_states=None):
    nb, n, _ = x.shape
    per_batch_mod = mod.shape[0] > 1
    assert bb == 1 or not per_batch_mod
    tiles = n // tm

    def lspec(shape):
        nd = len(shape)
        return pl.BlockSpec((1,) + tuple(shape[1:]), lambda b, i: (layer,) + (0,) * (nd - 1))

    in_specs = [
        pl.BlockSpec((bb, tm, D_MODEL), lambda b, i: (b, i, 0)),
        pl.BlockSpec((1, 1, 3 * D_MODEL), (lambda b, i: (b, 0, 0)) if per_batch_mod else (lambda b, i: (0, 0, 0))),
    ]
    args = [x, mod]
    for name in ("norm_pre", "w_in", "wqb", "wkvb", "mla_q_norm", "mla_kv_norm", "gqa_q_norm", "gqa_k_norm"):
        a = prep[name]
        in_specs.append(lspec(a.shape))
        args.append(a)
    if rope:
        for t in tabs:
            in_specs.append(pl.BlockSpec((tm, LANES), lambda b, i: (i, 0)))
            args.append(t)

    out_shape = [
        jax.ShapeDtypeStruct((nb, N_QBLK, n, MXU_W), BF16),
        jax.ShapeDtypeStruct((nb, N_KVBLK, n, MXU_W), BF16),
        jax.ShapeDtypeStruct((nb, n, 3 * HY_W), BF16),
        jax.ShapeDtypeStruct((nb, n, 4 * GROUP_W), BF16),
    ]
    out_specs = [
        pl.BlockSpec((bb, N_QBLK, tm, MXU_W), lambda b, i: (b, 0, i, 0)),
        pl.BlockSpec((bb, N_KVBLK, tm, MXU_W), lambda b, i: (b, 0, i, 0)),
        pl.BlockSpec((bb, tm, 3 * HY_W), lambda b, i: (b, i, 0)),
        pl.BlockSpec((bb, tm, 4 * GROUP_W), lambda b, i: (b, i, 0)),
    ]
    aliases = {}
    if states:
        for w in (MLA_KV_RANK, MLA_ROPE, 128, 128, 256, 256):
            out_shape.append(jax.ShapeDtypeStruct((nb, DEPTH, n, w), F32))
            if prev_states is None:
                out_specs.append(pl.BlockSpec((bb, DEPTH, tm, w), lambda b, i: (b, 0, i, 0)))
            else:
                out_specs.append(pl.BlockSpec((bb, 1, tm, w), lambda b, i: (b, layer, i, 0)))
        if prev_states is not None:
            for j, a in enumerate(prev_states):
                aliases[len(args)] = 4 + j
                in_specs.append(pl.BlockSpec(memory_space=pl.ANY))
                args.append(a)

    return pl.pallas_call(
        functools.partial(_inproj_kernel, rope=rope, states=states, n_alias=len(aliases)),
        grid=(nb // bb, tiles),
        in_specs=in_specs,
        out_specs=out_specs,
        out_shape=out_shape,
        scratch_shapes=[pltpu.VMEM((INPROJ_CHUNK, D_MODEL), BF16)] * (bb * tm // INPROJ_CHUNK),
        input_output_aliases=aliases,
        compiler_params=pltpu.CompilerParams(
            dimension_semantics=("parallel", "parallel"), vmem_limit_bytes=VMEM_LIMIT),
        name="inproj_rope" if rope else "inproj_ctx",
    )(*args)


def _chunk_loop(nchunk, body):
    if nchunk == 1:
        body(0)
    else:
        def wrapped(k, carry):
            body(k)
            return carry
        lax.fori_loop(0, nchunk, wrapped, 0)


def _hyena_kernel(hy_ref, gate_ref, cw_ref, cb_ref, feats_ref, fw1_ref, fb1_ref, fw2_ref, fb2_ref,
                  fw3_ref, freq_ref, decay_ref, bias_ref, cs_ref, ss_ref,
                  o_ref, fr_scr, fi_scr, work, hy_pad, *, n, bt):
    rc = min(n, HY_CHUNK)
    nchunk = n // rc
    rowc = lax.broadcasted_iota(jnp.int32, (rc, HY_W), 0)

    def rows(k):
        if nchunk == 1:
            return slice(0, rc)
        return pl.ds(pl.multiple_of(k * rc, rc), rc)

    @pl.when(pl.program_id(0) == 0)
    def _filters():
        freq = freq_ref[0]
        bias = bias_ref[0]

        def taps(k):
            ft = feats_ref[rows(k), :]
            t = ft[:, 0:1]
            h1 = jnp.sin(freq[0:1, :] * (_dot_split(ft, fw1_ref[0]) + fb1_ref[0]))
            h2 = jnp.sin(freq[1:2, :] * (_dot_split(h1, fw2_ref[0]) + fb2_ref[0]))
            h3 = _dot_split(h2, fw3_ref[0])
            dec = jnp.exp(-t * jnp.abs(decay_ref[0]))
            lag0 = (rowc + k * rc) == 0
            for o in range(2):
                hf = h3[:, o * 512:o * 512 + 256] * dec
                hb = h3[:, o * 512 + 256:o * 512 + 512] * dec
                hf = hf + jnp.where(lag0, bias[o:o + 1, :], 0.0)
                hb = jnp.where(lag0, 0.0, hb)
                work[rows(k), o * 512:o * 512 + 256] = hf.astype(BF16)
                work[rows(k), o * 512 + 256:o * 512 + 512] = hb.astype(BF16)

        _chunk_loop(nchunk, taps)

        def spectra(k):
            ft = feats_ref[rows(k), :]
            c = ft[:, HY_EMB + 7:HY_EMB + 8]
            s = ft[:, HY_EMB + 8:HY_EMB + 9]
            p = _dot(cs_ref[rows(k), :], work[...])
            q = _dot(ss_ref[rows(k), :], work[...])
            for o in range(2):
                pf, pb = p[:, o * 512:o * 512 + 256], p[:, o * 512 + 256:o * 512 + 512]
                qf, qb = q[:, o * 512:o * 512 + 256], q[:, o * 512 + 256:o * 512 + 512]
                fr_scr[o, rows(k), :] = (c * (pf + pb) + s * (qf + qb)) * (1.0 / n)
                fi_scr[o, rows(k), :] = (s * (pf - pb) - c * (qf - qb)) * (1.0 / n)

        _chunk_loop(nchunk, spectra)

    cw = cw_ref[0]
    cb = cb_ref[0]
    hy_pad[0:HY_HALO, :] = jnp.zeros((HY_HALO, 3 * HY_W), BF16)
    hy_pad[HY_HALO + n:2 * HY_HALO + n, :] = jnp.zeros((HY_HALO, 3 * HY_W), BF16)

    def short_conv(k, j):
        sl = slice(j * HY_W, (j + 1) * HY_W)
        if nchunk == 1:
            ext = hy_pad[:, sl].astype(F32)
        else:
            ext = hy_pad[pl.ds(pl.multiple_of(k * rc, rc), rc + 2 * HY_HALO), sl].astype(F32)
        um = pltpu.roll(ext, 1, axis=0)[HY_HALO:HY_HALO + rc]
        up = pltpu.roll(ext, rc + 2 * HY_HALO - 1, axis=0)[HY_HALO:HY_HALO + rc]
        u = ext[HY_HALO:HY_HALO + rc]
        return um * cw[0:1, sl] + u * cw[1:2, sl] + up * cw[2:3, sl] + cb[:, sl]

    zc, yrc, yic = slice(0, 256), slice(256, 512), slice(512, 768)

    def one(bi):
        hy_pad[HY_HALO:HY_HALO + n, :] = hy_ref[bi]

        def conv_v(k):
            work[rows(k), zc] = short_conv(k, 2).astype(BF16)

        _chunk_loop(nchunk, conv_v)
        for o in range(2):
            def forward(k, o=o):
                a = _dot(cs_ref[rows(k), :], work[:, zc])
                b = _dot(ss_ref[rows(k), :], work[:, zc])
                fr = fr_scr[o, rows(k), :]
                fi = fi_scr[o, rows(k), :]
                work[rows(k), yrc] = (a * fr + b * fi).astype(BF16)
                work[rows(k), yic] = (a * fi - b * fr).astype(BF16)

            def inverse(k, o=o):
                y = _dot(cs_ref[rows(k), :], work[:, yrc]) - _dot(ss_ref[rows(k), :], work[:, yic])
                z = short_conv(k, o) * y
                if o == 0:
                    work[rows(k), zc] = z.astype(BF16)
                else:
                    o_ref[bi, rows(k), :] = (z * gate_ref[bi, rows(k), :].astype(F32)).astype(BF16)

            _chunk_loop(nchunk, forward)
            _chunk_loop(nchunk, inverse)

    if bt == 1:
        one(0)
    else:
        def body(bi, carry):
            one(bi)
            return carry
        lax.fori_loop(0, bt, body, 0)


def _hyena_call(layer, hy, gates, prep, dft, *, bt):
    nb, n, _ = hy.shape
    cs, ss, feats = dft

    def lspec(shape):
        nd = len(shape)
        return pl.BlockSpec((1,) + tuple(shape[1:]), lambda b: (layer,) + (0,) * (nd - 1))

    def cspec(shape):
        nd = len(shape)
        return pl.BlockSpec(tuple(shape), lambda b: (0,) * nd, pipeline_mode=pl.Buffered(1))

    big = n >= 1024
    in_specs = [
        pl.BlockSpec((bt, n, 3 * HY_W), lambda b: (b, 0, 0)),
        pl.BlockSpec((bt, n, GROUP_W), lambda b: (b, 0, 3)),
    ]
    args = [hy, gates]
    for name in ("hy_conv_w", "hy_conv_b"):
        in_specs.append(lspec(prep[name].shape))
        args.append(prep[name])
    in_specs.append(cspec(feats.shape))
    args.append(feats)
    for name in ("hy_fw1", "hy_fb1", "hy_fw2", "hy_fb2", "hy_fw3", "hy_freq", "hy_decay", "hy_bias"):
        in_specs.append(lspec(prep[name].shape))
        args.append(prep[name])
    in_specs += [cspec(cs.shape), cspec(ss.shape)]
    args += [cs, ss]

    return pl.pallas_call(
        functools.partial(_hyena_kernel, n=n, bt=bt),
        grid=(nb // bt,),
        in_specs=in_specs,
        out_specs=pl.BlockSpec((bt, n, HY_W), lambda b: (b, 0, 0)),
        out_shape=jax.ShapeDtypeStruct((nb, n, HY_W), BF16),
        scratch_shapes=[pltpu.VMEM((2, n, HY_W), F32), pltpu.VMEM((2, n, HY_W), F32),
                        pltpu.VMEM((n, 4 * HY_W), BF16),
                        pltpu.VMEM((n + 2 * HY_HALO, 3 * HY_W), BF16)],
        compiler_params=pltpu.CompilerParams(
            dimension_semantics=("arbitrary",), vmem_limit_bytes=VMEM_LIMIT),
        name="hyena_long" if big else "hyena_short",
    )(*args)


def _attn_kernel(*refs, has_ctx, lam_init, tq):
    it = iter(refs)
    q_ref, kv_ref = next(it), next(it)
    ctx_ref = next(it) if has_ctx else None
    gate_ref, ohy_ref, x_ref, mod_ref, gpost_ref, wout_ref, subln_ref, lamp_ref = (next(it) for _ in range(8))
    y_ref = next(it)

    lane = lax.broadcasted_iota(jnp.int32, (tq, MXU_W), 1)

    def band(lo, width):
        return (lane >= lo) & (lane < lo + width)

    def scores(qs, kblk):
        s_l = _dot_nt(qs, kv_ref[0, kblk])
        s_c = _dot_nt(qs, ctx_ref[0, kblk]) if has_ctx else None
        return s_l, s_c

    def softmax_rows(s_l, s_c):
        m = jnp.max(s_l, axis=-1, keepdims=True)
        if has_ctx:
            m = jnp.maximum(m, jnp.max(s_c, axis=-1, keepdims=True))
            p_c = jnp.exp2(s_c - m)
        p_l = jnp.exp2(s_l - m)
        l = jnp.sum(p_l, axis=-1, keepdims=True)
        if has_ctx:
            l = l + jnp.sum(p_c, axis=-1, keepdims=True)
            return p_l, p_c, l
        return p_l, None, l

    def pv(p_l, p_c, vblk):
        o = _dot(p_l.astype(BF16), kv_ref[0, vblk])
        if has_ctx:
            o = o + _dot(p_c.astype(BF16), ctx_ref[0, vblk])
        return o

    def stack(parts):
        return jnp.concatenate(parts, axis=0)

    lamp = lamp_ref[0]
    lam = (jnp.exp(jnp.sum(lamp[0:1, :] * lamp[1:2, :], axis=-1, keepdims=True))
           - jnp.exp(jnp.sum(lamp[2:3, :] * lamp[3:4, :], axis=-1, keepdims=True)) + lam_init)
    lo_half = lax.broadcasted_iota(jnp.int32, (tq, LANES), 1) < GQA_HD

    units = [("mla", 0), ("mla", 1), ("gqa", 0), ("gqa", 1), ("diff", 0), ("diff", 1), ("diff", 2), ("diff", 3)]

    def unit_scores(kind, idx):
        if kind == "mla":
            qf = q_ref[0, idx].astype(F32)
            masks = [band(64 * j, MLA_NOPE) | band(128 + MLA_ROPE * (2 * idx + j), MLA_ROPE) for j in range(2)]
            kblk = idx
        elif kind == "gqa":
            qf = q_ref[0, 2 + idx].astype(F32)
            masks = [band(GQA_HD * g, GQA_HD) for g in range(2)]
            kblk = 3
        else:
            qf = q_ref[0, 4].astype(F32)
            masks = [band(64 * idx + DIFF_HD * s, DIFF_HD) for s in range(2)]
            kblk = 4
        return scores(stack([jnp.where(m, qf, 0.0) for m in masks]).astype(BF16), kblk)

    def unit_softmax(kind, s_l, s_c):
        if kind == "diff":
            p1_l, p1_c, l1 = softmax_rows(s_l[0:tq], s_c[0:tq] if has_ctx else None)
            p2_l, p2_c, l2 = softmax_rows(s_l[tq:2 * tq], s_c[tq:2 * tq] if has_ctx else None)
            w1 = 1.0 / l1
            w2 = lam / l2
            return p1_l * w1 - p2_l * w2, (p1_c * w1 - p2_c * w2) if has_ctx else None, None
        return softmax_rows(s_l, s_c)

    def unit_pv(kind, idx, p_l, p_c, l):
        if kind == "diff":
            return jnp.where(band(DIFF_VD * idx, DIFF_VD), pv(p_l, p_c, 5), 0.0)
        if kind == "mla":
            o = pv(p_l, p_c, 2) / l
            return (jnp.where(band(128 * idx, MLA_V), o[0:tq], 0.0)
                    + jnp.where(band(128 * idx + MLA_V, MLA_V), o[tq:2 * tq], 0.0))
        o = (pv(p_l, p_c, 3) / l)[:, 128:256]
        part = jnp.where(lo_half, o[0:tq], o[tq:2 * tq])
        zero = jnp.zeros_like(part)
        return jnp.concatenate([part, zero] if idx == 0 else [zero, part], axis=1)

    acc = {"mla": None, "gqa": None, "diff": None}
    sc_vals, sm_vals = {}, {}
    for t in range(len(units) + ATT_SCORE_LEAD + ATT_PV_LAG):
        if t < len(units):
            sc_vals[t] = unit_scores(*units[t])
        u = t - ATT_SCORE_LEAD
        if 0 <= u < len(units):
            sm_vals[u] = unit_softmax(units[u][0], *sc_vals.pop(u))
        u = t - ATT_SCORE_LEAD - ATT_PV_LAG
        if u >= 0:
            kind, idx = units[u]
            c = unit_pv(kind, idx, *sm_vals.pop(u))
            acc[kind] = c if acc[kind] is None else acc[kind] + c
    o_mla, o_gqa, od = acc["mla"], acc["gqa"], acc["diff"]

    y_ref[0] = _mix_out(o_mla, o_gqa, od, gate_ref[0], ohy_ref[0], x_ref[0], mod_ref[0], gpost_ref[0],
                        wout_ref[0], subln_ref[0], lam_init)


def _mix_out(o_mla, o_gqa, od, gate, ohy, x, mod, gpost, wout, subln, lam_init):
    rows = od.shape[0]
    lane = lax.broadcasted_iota(jnp.int32, (rows, MXU_W), 1)
    od2 = od * od
    r = jnp.zeros_like(od)
    for h in range(DIFF_HEADS):
        mh = (lane >= DIFF_VD * h) & (lane < DIFF_VD * (h + 1))
        sh = jnp.sum(jnp.where(mh, od2, 0.0), axis=-1, keepdims=True)
        r = jnp.where(mh, lax.rsqrt(sh * (1.0 / DIFF_VD) + EPS), r)
    od = (od * r * subln) * (1.0 - lam_init)

    g = gate.astype(F32)
    ob = jnp.concatenate([
        (o_mla * g[:, 0:256]).astype(BF16),
        (o_gqa * g[:, 256:512]).astype(BF16),
        (od * g[:, 512:768]).astype(BF16),
        ohy,
    ], axis=1)
    out = _dot(ob, wout)
    ms = jnp.mean(out * out, axis=-1, keepdims=True)
    gate_mod = mod[:, 2 * D_MODEL:3 * D_MODEL]
    return x + gate_mod * (out * lax.rsqrt(ms + EPS) * gpost)


def _attn_call(layer, q, kv, ctx, gates, ohy, x, mod, prep, *, tq):
    nb, n, _ = x.shape
    has_ctx = ctx is not None
    per_batch_mod = mod.shape[0] > 1
    lam_init = 0.8 - 0.6 * math.exp(-0.3 * layer)

    def lspec(shape):
        nd = len(shape)
        return pl.BlockSpec((1,) + tuple(shape[1:]), lambda b, i: (layer,) + (0,) * (nd - 1))

    in_specs = [
        pl.BlockSpec((1, N_QBLK, tq, MXU_W), lambda b, i: (b, 0, i, 0)),
        pl.BlockSpec((1, N_KVBLK, n, MXU_W), lambda b, i: (b, 0, 0, 0)),
    ]
    args = [q, kv]
    if has_ctx:
        p = ctx.shape[3]
        in_specs.append(pl.BlockSpec((None, 1, N_KVBLK, p, MXU_W), lambda b, i: (layer, b, 0, 0, 0)))
        args.append(ctx)
    in_specs += [
        pl.BlockSpec((1, tq, 4 * GROUP_W), lambda b, i: (b, i, 0)),
        pl.BlockSpec((1, tq, HY_W), lambda b, i: (b, i, 0)),
        pl.BlockSpec((1, tq, D_MODEL), lambda b, i: (b, i, 0)),
        pl.BlockSpec((1, 1, 3 * D_MODEL), (lambda b, i: (b, 0, 0)) if per_batch_mod else (lambda b, i: (0, 0, 0))),
    ]
    args += [gates, ohy, x, mod]
    for name in ("norm_post", "w_out", "diff_subln", "diff_lam"):
        in_specs.append(lspec(prep[name].shape))
        args.append(prep[name])

    return pl.pallas_call(
        functools.partial(_attn_kernel, has_ctx=has_ctx, lam_init=lam_init, tq=tq),
        grid=(nb, n // tq),
        in_specs=in_specs,
        out_specs=pl.BlockSpec((1, tq, D_MODEL), lambda b, i: (b, i, 0)),
        out_shape=jax.ShapeDtypeStruct((nb, n, D_MODEL), F32),
        compiler_params=pltpu.CompilerParams(
            dimension_semantics=("parallel", "parallel"), vmem_limit_bytes=VMEM_LIMIT),
        name="attn_lat" if has_ctx else "attn_ctx",
    )(*args)


def _rope_tables(n):
    tok = jnp.arange(n, dtype=jnp.int32)
    row = (tok // GRID_W).astype(F32)
    col = (tok % GRID_W).astype(F32)
    lane = jnp.arange(LANES, dtype=jnp.int32)
    tabs = []
    for d in (64, 32):
        m = d // 2
        i = lane % d
        ii = i % m
        f = ii % (m // 2)
        inv = ROPE_BASE ** (-(2 * f).astype(F32) / m)
        pos = jnp.where((i // m)[None, :] == 0, row[:, None], col[:, None])
        ang = pos * inv[None, :]
        sign = jnp.where(ii < m // 2, -1.0, 1.0).astype(F32)
        tabs += [jnp.cos(ang), jnp.sin(ang) * sign[None, :]]
    return tabs


def _dft_tables(n):
    k = jnp.arange(n, dtype=jnp.int32)
    unit = math.pi / (4 * n)
    kk = 2 * k[:, None] + 1
    ang_a = ((kk * (2 * LANES * jnp.arange(n // LANES, dtype=jnp.int32))[None, :]) % (8 * n)).astype(F32) * unit
    ang_b = ((kk * (2 * jnp.arange(LANES, dtype=jnp.int32) + 1)[None, :]) % (8 * n)).astype(F32) * unit
    ca, sa = jnp.cos(ang_a)[:, :, None], jnp.sin(ang_a)[:, :, None]
    cb, sb = jnp.cos(ang_b)[:, None, :], jnp.sin(ang_b)[:, None, :]
    cs = (ca * cb - sa * sb).reshape(n, n).astype(BF16)
    ss = (sa * cb + ca * sb).reshape(n, n).astype(BF16)
    th = (2 * k + 1).astype(F32) * (math.pi / (4 * n))
    ch = jnp.cos(th)[:, None]
    sh = jnp.sin(th)[:, None]
    t = jnp.linspace(0.0, 1.0, n, dtype=F32)[:, None]
    w = 2.0 * math.pi * jnp.arange(n, dtype=F32)[:, None] / n
    f = jnp.linspace(1e-4, HY_BANDS - 1, HY_BANDS, dtype=F32)[None, :]
    feats = jnp.concatenate([t, jnp.cos(f * w), -jnp.sin(f * w), jnp.zeros((n, 7), F32), ch, sh], axis=-1)
    feats = jnp.pad(feats, ((0, 0), (0, LANES - feats.shape[1])))
    return cs, ss, feats


def _prep_weights(p):
    offs = [0]
    for s in SPLIT_SIZES:
        offs.append(offs[-1] + s)
    (o_cq, o_ckv, o_kr, o_mg, o_gq, o_gk, o_gv, o_gg, o_dq, o_dk, o_dv, o_dg, o_hy, o_hg) = offs[:-1]
    w = p["w_in"]

    def cols(o, n):
        return w[:, :, o:o + n]

    def perm_heads(a, axis):
        parts = jnp.split(a, 4, axis=axis)
        return jnp.concatenate([parts[0], parts[2], parts[1], parts[3]], axis=axis)

    kr = cols(o_kr, 32)
    w_in = jnp.concatenate([
        cols(o_cq, 256), cols(o_ckv, 128), kr, kr, kr, kr,
        perm_heads(cols(o_gq, 256), 2), cols(o_gk, 128), cols(o_gv, 128),
        cols(o_dq, 256), cols(o_dk, 256), cols(o_dv, 256), cols(o_hy, 768),
        cols(o_mg, 256), perm_heads(cols(o_gg, 256), 2), cols(o_dg, 256), cols(o_hg, 256),
    ], axis=2).astype(BF16)

    wq = p["mla_wq_b"].reshape(DEPTH, MLA_Q_RANK, MLA_HEADS, MLA_NOPE + MLA_ROPE)
    wqb = jnp.concatenate([wq[..., :MLA_NOPE].reshape(DEPTH, MLA_Q_RANK, 256),
                           wq[..., MLA_NOPE:].reshape(DEPTH, MLA_Q_RANK, 128)], axis=2).astype(BF16)
    wk = p["mla_wkv_b"].reshape(DEPTH, MLA_KV_RANK, MLA_HEADS, MLA_NOPE + MLA_V)
    wkvb = jnp.concatenate([wk[..., :MLA_NOPE].reshape(DEPTH, MLA_KV_RANK, 256),
                            wk[..., MLA_NOPE:].reshape(DEPTH, MLA_KV_RANK, 256)], axis=2).astype(BF16)

    wo = p["w_out"]
    w_out = jnp.concatenate([wo[:, 0:256], perm_heads(wo[:, 256:512], 1), wo[:, 512:]], axis=1).astype(BF16)

    def pad_to(a, shape):
        return jnp.pad(a, [(0, s - d) for d, s in zip(a.shape, shape)])

    return {
        "norm_pre": p["norm_pre"].reshape(DEPTH, 1, D_MODEL),
        "norm_post": p["norm_post"].reshape(DEPTH, 1, D_MODEL),
        "w_in": w_in, "wqb": wqb, "wkvb": wkvb, "w_out": w_out,
        "mla_q_norm": p["mla_q_norm"].reshape(DEPTH, 1, MLA_Q_RANK),
        "mla_kv_norm": p["mla_kv_norm"].reshape(DEPTH, 1, MLA_KV_RANK),
        "gqa_q_norm": jnp.tile(p["gqa_q_norm"], (1, 4)).reshape(DEPTH, 1, 256),
        "gqa_k_norm": jnp.tile(p["gqa_k_norm"], (1, 2)).reshape(DEPTH, 1, 128),
        "diff_subln": jnp.tile(p["diff_subln"], (1, 4)).reshape(DEPTH, 1, 256),
        "diff_lam": jnp.stack([p["diff_lq1"], p["diff_lk1"], p["diff_lq2"], p["diff_lk2"]], axis=1),
        "hy_conv_w": p["hy_conv_w"],
        "hy_conv_b": p["hy_conv_b"].reshape(DEPTH, 1, 3 * HY_W),
        "hy_fw1": pad_to(p["hy_fw1"], (DEPTH, LANES, LANES)),
        "hy_fb1": pad_to(p["hy_fb1"].reshape(DEPTH, 1, HY_FFN), (DEPTH, 1, LANES)),
        "hy_fw2": pad_to(p["hy_fw2"], (DEPTH, LANES, LANES)),
        "hy_fb2": pad_to(p["hy_fb2"].reshape(DEPTH, 1, HY_FFN), (DEPTH, 1, LANES)),
        "hy_fw3": pad_to(p["hy_fw3"], (DEPTH, LANES, 4 * HY_W)),
        "hy_freq": pad_to(p["hy_freq"], (DEPTH, 2, LANES)),
        "hy_decay": p["hy_decay"].reshape(DEPTH, 1, HY_W),
        "hy_bias": p["hy_bias"],
    }


def kernel(x_prompt, x_sample, cache_mla_ckv, cache_mla_krope, cache_gqa_k, cache_gqa_v, cache_diff_k, cache_diff_v, c, c_ctx, norm_pre, norm_post, ada_w, ada_b, w_in, w_out, mla_q_norm, mla_wq_b, mla_kv_norm, mla_wkv_b, gqa_q_norm, gqa_k_norm, diff_lq1, diff_lk1, diff_lq2, diff_lk2, diff_subln, hy_conv_w, hy_conv_b, hy_fw1, hy_fb1, hy_fw2, hy_fb2, hy_fw3, hy_freq, hy_decay, hy_bias):
    nb_p, n_p, _ = x_prompt.shape
    nb_s, n_s, _ = x_sample.shape
    past = cache_mla_ckv.shape[2]

    prep = _prep_weights(dict(
        norm_pre=norm_pre, norm_post=norm_post, w_in=w_in, w_out=w_out, mla_q_norm=mla_q_norm,
        mla_wq_b=mla_wq_b, mla_kv_norm=mla_kv_norm, mla_wkv_b=mla_wkv_b, gqa_q_norm=gqa_q_norm,
        gqa_k_norm=gqa_k_norm, diff_lq1=diff_lq1, diff_lk1=diff_lk1, diff_lq2=diff_lq2, diff_lk2=diff_lk2,
        diff_subln=diff_subln, hy_conv_w=hy_conv_w, hy_conv_b=hy_conv_b, hy_fw1=hy_fw1, hy_fb1=hy_fb1,
        hy_fw2=hy_fw2, hy_fb2=hy_fb2, hy_fw3=hy_fw3, hy_freq=hy_freq, hy_decay=hy_decay, hy_bias=hy_bias))
    tabs = _rope_tables(n_s)
    dft_p = _dft_tables(n_p)
    dft_s = _dft_tables(n_s)

    cc8 = jnp.concatenate([c_ctx[None, :], c, jnp.zeros((8 - 1 - nb_s, D_MODEL), F32)], axis=0)
    mod = _ada_call(cc8, ada_w, ada_b)

    ctx = _ctx_call(
        cache_mla_ckv, jnp.tile(cache_mla_krope, (1, 1, 1, 4)),
        cache_gqa_k.reshape(nb_s, DEPTH, past, 128), cache_gqa_v.reshape(nb_s, DEPTH, past, 128),
        cache_diff_k.reshape(nb_s, DEPTH, past, 256), cache_diff_v.reshape(nb_s, DEPTH, past, 256),
        prep["wkvb"])

    y_p, y_s = x_prompt, x_sample
    st = None
    for i in range(DEPTH):
        mod_p = mod[i, 0:1].reshape(1, 1, 3 * D_MODEL)
        mod_s = mod[i, 1:1 + nb_s].reshape(nb_s, 1, 3 * D_MODEL)
        outs = _inproj_call(i, y_p, mod_p, prep, None, rope=False, states=True, tm=n_p, bb=4, prev_states=st)
        q, kv, hy, gates = outs[:4]
        st = outs[4:]
        ohy = _hyena_call(i, hy, gates, prep, dft_p, bt=4)
        y_p = _attn_call(i, q, kv, None, gates, ohy, y_p, mod_p, prep, tq=n_p)
        q, kv, hy, gates = _inproj_call(i, y_s, mod_s, prep, tabs, rope=True, states=False, tm=1024)
        ohy = _hyena_call(i, hy, gates, prep, dft_s, bt=1)
        y_s = _attn_call(i, q, kv, ctx, gates, ohy, y_s, mod_s, prep, tq=128)

    new_ckv, new_kr = st[0], st[1]
    new_gk = st[2].reshape(nb_p, DEPTH, n_p, GQA_KV_HEADS, GQA_HD)
    new_gv = st[3].reshape(nb_p, DEPTH, n_p, GQA_KV_HEADS, GQA_HD)
    new_dk = st[4].reshape(nb_p, DEPTH, n_p, DIFF_HEADS, 2 * DIFF_HD)
    new_dv = st[5].reshape(nb_p, DEPTH, n_p, DIFF_HEADS, DIFF_VD)
    return (y_p, y_s, new_ckv, new_kr, new_gk, new_gv, new_dk, new_dv)
```

```python
import functools
import math

import jax
import jax.numpy as jnp
from jax import lax
from jax.experimental import pallas as pl
from jax.experimental.pallas import tpu as pltpu

F32 = jnp.float32
BF16 = jnp.bfloat16

D_MODEL = 1024
DEPTH = 2
GRID_W = 64
ROPE_BASE = 10000.0
EPS = 1e-6
GROUP_W = 256
MLA_HEADS = 4
MLA_NOPE = 64
MLA_ROPE = 32
MLA_V = 64
MLA_Q_RANK = 256
MLA_KV_RANK = 128
GQA_HEADS = 4
GQA_KV_HEADS = 2
GQA_HD = 64
DIFF_HEADS = 4
DIFF_VD = 64
DIFF_HD = 32
HY_W = 256
HY_EMB = 33
HY_BANDS = 16
HY_FFN = 64
SPLIT_SIZES = (256, 128, 32, 256, 256, 128, 128, 256, 256, 256, 256, 256, 768, 256)
P_IN = sum(SPLIT_SIZES)

LANES = 128
MXU_W = 256
P_PAD = 14 * MXU_W
N_QBLK = 5
N_KVBLK = 6
LOG2E = math.log2(math.e)
INPROJ_CHUNK = 256
HY_CHUNK = 512
HY_HALO = 16
ATT_SCORE_LEAD = 2
ATT_PV_LAG = 2
NT_DIMS = (((1,), (1,)), ((), ()))
VMEM_LIMIT = 56 * 1024 * 1024


def _dot(a, b):
    return jnp.dot(a, b, preferred_element_type=F32)


def _dot_nt(a, b):
    return lax.dot_general(a, b, NT_DIMS, preferred_element_type=F32)


def _dot_split(a, b):
    a_hi = a.astype(BF16)
    b_hi = b.astype(BF16)
    a_lo = (a - a_hi.astype(F32)).astype(BF16)
    b_lo = (b - b_hi.astype(F32)).astype(BF16)
    return _dot(a_hi, b_hi) + _dot(a_hi, b_lo) + _dot(a_lo, b_hi)


def _rms_full(v, g):
    ms = jnp.mean(v * v, axis=-1, keepdims=True)
    return v * lax.rsqrt(ms + EPS) * g


def _rms_heads64(v, g):
    outs = []
    lane = lax.broadcasted_iota(jnp.int32, (v.shape[0], LANES), 1)
    lo = lane < 64
    for b in range(v.shape[1] // LANES):
        vb = v[:, b * LANES:(b + 1) * LANES]
        v2 = vb * vb
        s_lo = jnp.sum(jnp.where(lo, v2, 0.0), axis=-1, keepdims=True)
        s_hi = jnp.sum(jnp.where(lo, 0.0, v2), axis=-1, keepdims=True)
        r = jnp.where(lo, lax.rsqrt(s_lo * (1.0 / 64) + EPS), lax.rsqrt(s_hi * (1.0 / 64) + EPS))
        outs.append(vb * r)
    out = outs[0] if len(outs) == 1 else jnp.concatenate(outs, axis=1)
    return out * g


def _rope(v, cos_t, sin_t, half):
    lane = lax.broadcasted_iota(jnp.int32, (v.shape[0], LANES), 1)
    first = (lane % (2 * half)) < half
    outs = []
    for b in range(v.shape[1] // LANES):
        vb = v[:, b * LANES:(b + 1) * LANES]
        up = pltpu.roll(vb, LANES - half, axis=1)
        dn = pltpu.roll(vb, half, axis=1)
        outs.append(vb * cos_t + jnp.where(first, up, dn) * sin_t)
    return outs[0] if len(outs) == 1 else jnp.concatenate(outs, axis=1)


def _ada_kernel(cc_ref, w_ref, b_ref, o_ref):
    a = cc_ref[...]
    a = a * jax.nn.sigmoid(a)
    o_ref[0] = _dot(a.astype(BF16), w_ref[0].astype(BF16)) + b_ref[0]


def _ada_call(cc8, ada_w, ada_b):
    tn = 768
    return pl.pallas_call(
        _ada_kernel,
        grid=(DEPTH, 3 * D_MODEL // tn),
        in_specs=[
            pl.BlockSpec((8, D_MODEL), lambda l, j: (0, 0)),
            pl.BlockSpec((1, D_MODEL, tn), lambda l, j: (l, 0, j)),
            pl.BlockSpec((1, 1, tn), lambda l, j: (l, 0, j)),
        ],
        out_specs=pl.BlockSpec((1, 8, tn), lambda l, j: (l, 0, j)),
        out_shape=jax.ShapeDtypeStruct((DEPTH, 8, 3 * D_MODEL), F32),
        compiler_params=pltpu.CompilerParams(dimension_semantics=("parallel", "parallel")),
        name="ada_mod",
    )(cc8, ada_w, ada_b.reshape(DEPTH, 1, 3 * D_MODEL))


def _ctx_kernel(ckv_ref, kr4_ref, gk_ref, gv_ref, dk_ref, dv_ref, wkvb_ref, o_ref):
    kvp = _dot(ckv_ref[0, 0].astype(BF16), wkvb_ref[0])
    kr4 = kr4_ref[0, 0].astype(BF16)
    o_ref[0, 0, 0] = jnp.concatenate([kvp[:, 0:128].astype(BF16), kr4], axis=1)
    o_ref[0, 0, 1] = jnp.concatenate([kvp[:, 128:256].astype(BF16), kr4], axis=1)
    o_ref[0, 0, 2] = kvp[:, 256:512].astype(BF16)
    gv = gv_ref[0, 0].astype(BF16)
    o_ref[0, 0, 3] = jnp.concatenate([gk_ref[0, 0].astype(BF16), gv], axis=1)
    o_ref[0, 0, 4] = dk_ref[0, 0].astype(BF16)
    o_ref[0, 0, 5] = dv_ref[0, 0].astype(BF16)


def _ctx_call(ckv, kr4, gk, gv, dk, dv, wkvb):
    nb, _, p, _ = ckv.shape

    def spec(w):
        return pl.BlockSpec((1, 1, p, w), lambda l, b: (b, l, 0, 0))

    return pl.pallas_call(
        _ctx_kernel,
        grid=(DEPTH, nb),
        in_specs=[spec(128), spec(128), spec(128), spec(128), spec(256), spec(256),
                  pl.BlockSpec((1, MLA_KV_RANK, 512), lambda l, b: (l, 0, 0))],
        out_specs=pl.BlockSpec((1, 1, N_KVBLK, p, MXU_W), lambda l, b: (l, b, 0, 0, 0)),
        out_shape=jax.ShapeDtypeStruct((DEPTH, nb, N_KVBLK, p, MXU_W), BF16),
        compiler_params=pltpu.CompilerParams(dimension_semantics=("parallel", "parallel")),
        name="ctx_kv",
    )(ckv, kr4, gk, gv, dk, dv, wkvb)


def _inproj_kernel(*refs, rope, states, n_alias):
    it = iter(refs)
    x_ref, mod_ref, gpre_ref, w_ref, wqb_ref, wkvb_ref = (next(it) for _ in range(6))
    gq_mla_ref, gkv_mla_ref, gqn_ref, gkn_ref = (next(it) for _ in range(4))
    if rope:
        cos64_ref, sin64_ref, cos32_ref, sin32_ref = (next(it) for _ in range(4))
    for _ in range(n_alias):
        next(it)
    q_ref, kv_ref, hy_ref, gate_ref = (next(it) for _ in range(4))
    if states:
        s_ckv_ref, s_kr_ref, s_gk_ref, s_gv_ref, s_dk_ref, s_dv_ref = (next(it) for _ in range(6))
        if n_alias == 0:
            for s_ref in (s_ckv_ref, s_kr_ref, s_gk_ref, s_gv_ref, s_dk_ref, s_dv_ref):
                s_ref[:, 1:] = jnp.zeros((s_ref.shape[0], s_ref.shape[1] - 1) + s_ref.shape[2:], F32)
    h_scrs = list(it)

    mod = mod_ref[0]
    shift = mod[:, 0:D_MODEL]
    scale = mod[:, D_MODEL:2 * D_MODEL]
    bb, tm = x_ref.shape[0], x_ref.shape[1]
    chunks = [(bi, slice(r0, r0 + INPROJ_CHUNK)) for bi in range(bb) for r0 in range(0, tm, INPROJ_CHUNK)]

    for (bi, rs), h_scr in zip(chunks, h_scrs):
        x = x_ref[bi, rs, :]
        ms = jnp.mean(x * x, axis=-1, keepdims=True)
        h = (x * lax.rsqrt(ms + EPS) * gpre_ref[0]) * (1.0 + scale) + shift
        h_scr[...] = h.astype(BF16)

    def proj(g):
        w = w_ref[0, :, g * MXU_W:(g + 1) * MXU_W]
        parts = [_dot(h_scr[...], w) for h_scr in h_scrs]
        return parts[0] if len(parts) == 1 else jnp.concatenate(parts, axis=0)

    def put(ref, lead, v, cols=slice(None)):
        for c, (bi, rs) in enumerate(chunks):
            ref[(bi,) + lead + (rs, cols)] = v[c * INPROJ_CHUNK:(c + 1) * INPROJ_CHUNK]

    def table(ref):
        parts = [ref[rs, :] for _, rs in chunks]
        return parts[0] if len(parts) == 1 else jnp.concatenate(parts, axis=0)

    def rope32(v):
        return _rope(v, table(cos32_ref), table(sin32_ref), 8) if rope else v

    def rope64(v):
        return _rope(v, table(cos64_ref), table(sin64_ref), 16) if rope else v

    cq = _rms_full(proj(0), gq_mla_ref[0])
    q3 = _dot(cq.astype(BF16), wqb_ref[0]) * (LOG2E * (MLA_NOPE + MLA_ROPE) ** -0.5)
    qr = rope32(q3[:, 256:384]).astype(BF16)
    put(q_ref, (0,), jnp.concatenate([q3[:, 0:128].astype(BF16), qr], axis=1))
    put(q_ref, (1,), jnp.concatenate([q3[:, 128:256].astype(BF16), qr], axis=1))

    z1 = proj(1)
    ckv = _rms_full(z1[:, 0:128], gkv_mla_ref[0])
    kr4 = z1[:, 128:256]
    if states:
        put(s_ckv_ref, (0,), ckv)
        put(s_kr_ref, (0,), kr4[:, 0:MLA_ROPE])
    kvp = _dot(ckv.astype(BF16), wkvb_ref[0])
    kr4 = rope32(kr4).astype(BF16)
    put(kv_ref, (0,), jnp.concatenate([kvp[:, 0:128].astype(BF16), kr4], axis=1))
    put(kv_ref, (1,), jnp.concatenate([kvp[:, 128:256].astype(BF16), kr4], axis=1))
    put(kv_ref, (2,), kvp[:, 256:512].astype(BF16))

    gq = (rope64(_rms_heads64(proj(2), gqn_ref[0])) * (LOG2E * GQA_HD ** -0.5)).astype(BF16)
    zq = jnp.zeros((gq.shape[0], LANES), BF16)
    put(q_ref, (2,), jnp.concatenate([gq[:, 0:128], zq], axis=1))
    put(q_ref, (3,), jnp.concatenate([gq[:, 128:256], zq], axis=1))
    z3 = proj(3)
    gk = _rms_heads64(z3[:, 0:128], gkn_ref[0])
    gv = z3[:, 128:256]
    if states:
        put(s_gk_ref, (0,), gk)
        put(s_gv_ref, (0,), gv)
    put(kv_ref, (3,), jnp.concatenate([rope64(gk).astype(BF16), gv.astype(BF16)], axis=1))

    put(q_ref, (4,), (rope32(proj(4)) * (LOG2E * DIFF_HD ** -0.5)).astype(BF16))
    dk = proj(5)
    dv = proj(6)
    if states:
        put(s_dk_ref, (0,), dk)
        put(s_dv_ref, (0,), dv)
    put(kv_ref, (4,), rope32(dk).astype(BF16))
    put(kv_ref, (5,), dv.astype(BF16))

    for j in range(3):
        put(hy_ref, (), proj(7 + j).astype(BF16), slice(j * MXU_W, (j + 1) * MXU_W))
    for j in range(4):
        g = proj(10 + j)
        put(gate_ref, (), (g * jax.nn.sigmoid(g)).astype(BF16), slice(j * MXU_W, (j + 1) * MXU_W))


def _inproj_call(layer, x, mod, prep, tabs, *, rope, states, tm, bb=1, prev_states=None):
    nb, n, _ = x.shape
    per_batch_mod = mod.shape[0] > 1
    assert bb == 1 or not per_batch_mod
    tiles = n // tm

    def lspec(shape):
        nd = len(shape)
        return pl.BlockSpec((1,) + tuple(shape[1:]), lambda b, i: (layer,) + (0,) * (nd - 1))

    in_specs = [
        pl.BlockSpec((bb, tm, D_MODEL), lambda b, i: (b, i, 0)),
        pl.BlockSpec((1, 1, 3 * D_MODEL), (lambda b, i: (b, 0, 0)) if per_batch_mod else (lambda b, i: (0, 0, 0))),
    ]
    args = [x, mod]
    for name in ("norm_pre", "w_in", "wqb", "wkvb", "mla_q_norm", "mla_kv_norm", "gqa_q_norm", "gqa_k_norm"):
        a = prep[name]
        in_specs.append(lspec(a.shape))
        args.append(a)
    if rope:
        for t in tabs:
            in_specs.append(pl.BlockSpec((tm, LANES), lambda b, i: (i, 0)))
            args.append(t)

    out_shape = [
        jax.ShapeDtypeStruct((nb, N_QBLK, n, MXU_W), BF16),
        jax.ShapeDtypeStruct((nb, N_KVBLK, n, MXU_W), BF16),
        jax.ShapeDtypeStruct((nb, n, 3 * HY_W), BF16),
        jax.ShapeDtypeStruct((nb, n, 4 * GROUP_W), BF16),
    ]
    out_specs = [
        pl.BlockSpec((bb, N_QBLK, tm, MXU_W), lambda b, i: (b, 0, i, 0)),
        pl.BlockSpec((bb, N_KVBLK, tm, MXU_W), lambda b, i: (b, 0, i, 0)),
        pl.BlockSpec((bb, tm, 3 * HY_W), lambda b, i: (b, i, 0)),
        pl.BlockSpec((bb, tm, 4 * GROUP_W), lambda b, i: (b, i, 0)),
    ]
    aliases = {}
    if states:
        for w in (MLA_KV_RANK, MLA_ROPE, 128, 128, 256, 256):
            out_shape.append(jax.ShapeDtypeStruct((nb, DEPTH, n, w), F32))
            if prev_states is None:
                out_specs.append(pl.BlockSpec((bb, DEPTH, tm, w), lambda b, i: (b, 0, i, 0)))
            else:
                out_specs.append(pl.BlockSpec((bb, 1, tm, w), lambda b, i: (b, layer, i, 0)))
        if prev_states is not None:
            for j, a in enumerate(prev_states):
                aliases[len(args)] = 4 + j
                in_specs.append(pl.BlockSpec(memory_space=pl.ANY))
                args.append(a)

    return pl.pallas_call(
        functools.partial(_inproj_kernel, rope=rope, states=states, n_alias=len(aliases)),
        grid=(nb // bb, tiles),
        in_specs=in_specs,
        out_specs=out_specs,
        out_shape=out_shape,
        scratch_shapes=[pltpu.VMEM((INPROJ_CHUNK, D_MODEL), BF16)] * (bb * tm // INPROJ_CHUNK),
        input_output_aliases=aliases,
        compiler_params=pltpu.CompilerParams(
            dimension_semantics=("parallel", "parallel"), vmem_limit_bytes=VMEM_LIMIT),
        name="inproj_rope" if rope else "inproj_ctx",
    )(*args)


def _chunk_loop(nchunk, body):
    if nchunk == 1:
        body(0)
    else:
        def wrapped(k, carry):
            body(k)
            return carry
        lax.fori_loop(0, nchunk, wrapped, 0)


def _hyena_kernel(hy_ref, gate_ref, cw_ref, cb_ref, feats_ref, fw1_ref, fb1_ref, fw2_ref, fb2_ref,
                  fw3_ref, freq_ref, decay_ref, bias_ref, cs_ref, ss_ref,
                  o_ref, fr_scr, fi_scr, work, hy_pad, *, n, bt):
    rc = min(n, HY_CHUNK)
    nchunk = n // rc
    rowc = lax.broadcasted_iota(jnp.int32, (rc, HY_W), 0)

    def rows(k):
        if nchunk == 1:
            return slice(0, rc)
        return pl.ds(pl.multiple_of(k * rc, rc), rc)

    @pl.when(pl.program_id(0) == 0)
    def _filters():
        freq = freq_ref[0]
        bias = bias_ref[0]

        def taps(k):
            ft = feats_ref[rows(k), :]
            t = ft[:, 0:1]
            h1 = jnp.sin(freq[0:1, :] * (_dot_split(ft, fw1_ref[0]) + fb1_ref[0]))
            h2 = jnp.sin(freq[1:2, :] * (_dot_split(h1, fw2_ref[0]) + fb2_ref[0]))
            h3 = _dot_split(h2, fw3_ref[0])
            dec = jnp.exp(-t * jnp.abs(decay_ref[0]))
            lag0 = (rowc + k * rc) == 0
            for o in range(2):
                hf = h3[:, o * 512:o * 512 + 256] * dec
                hb = h3[:, o * 512 + 256:o * 512 + 512] * dec
                hf = hf + jnp.where(lag0, bias[o:o + 1, :], 0.0)
                hb = jnp.where(lag0, 0.0, hb)
                work[rows(k), o * 512:o * 512 + 256] = hf.astype(BF16)
                work[rows(k), o * 512 + 256:o * 512 + 512] = hb.astype(BF16)

        _chunk_loop(nchunk, taps)

        def spectra(k):
            ft = feats_ref[rows(k), :]
            c = ft[:, HY_EMB + 7:HY_EMB + 8]
            s = ft[:, HY_EMB + 8:HY_EMB + 9]
            p = _dot(cs_ref[rows(k), :], work[...])
            q = _dot(ss_ref[rows(k), :], work[...])
            for o in range(2):
                pf, pb = p[:, o * 512:o * 512 + 256], p[:, o * 512 + 256:o * 512 + 512]
                qf, qb = q[:, o * 512:o * 512 + 256], q[:, o * 512 + 256:o * 512 + 512]
                fr_scr[o, rows(k), :] = (c * (pf + pb) + s * (qf + qb)) * (1.0 / n)
                fi_scr[o, rows(k), :] = (s * (pf - pb) - c * (qf - qb)) * (1.0 / n)

        _chunk_loop(nchunk, spectra)

    cw = cw_ref[0]
    cb = cb_ref[0]
    hy_pad[0:HY_HALO, :] = jnp.zeros((HY_HALO, 3 * HY_W), BF16)
    hy_pad[HY_HALO + n:2 * HY_HALO + n, :] = jnp.zeros((HY_HALO, 3 * HY_W), BF16)

    def short_conv(k, j):
        sl = slice(j * HY_W, (j + 1) * HY_W)
        if nchunk == 1:
            ext = hy_pad[:, sl].astype(F32)
        else:
            ext = hy_pad[pl.ds(pl.multiple_of(k * rc, rc), rc + 2 * HY_HALO), sl].astype(F32)
        um = pltpu.roll(ext, 1, axis=0)[HY_HALO:HY_HALO + rc]
        up = pltpu.roll(ext, rc + 2 * HY_HALO - 1, axis=0)[HY_HALO:HY_HALO + rc]
        u = ext[HY_HALO:HY_HALO + rc]
        return um * cw[0:1, sl] + u * cw[1:2, sl] + up * cw[2:3, sl] + cb[:, sl]

    zc, yrc, yic = slice(0, 256), slice(256, 512), slice(512, 768)

    def one(bi):
        hy_pad[HY_HALO:HY_HALO + n, :] = hy_ref[bi]

        def conv_v(k):
            work[rows(k), zc] = short_conv(k, 2).astype(BF16)

        _chunk_loop(nchunk, conv_v)
        for o in range(2):
            def forward(k, o=o):
                a = _dot(cs_ref[rows(k), :], work[:, zc])
                b = _dot(ss_ref[rows(k), :], work[:, zc])
                fr = fr_scr[o, rows(k), :]
                fi = fi_scr[o, rows(k), :]
                work[rows(k), yrc] = (a * fr + b * fi).astype(BF16)
                work[rows(k), yic] = (a * fi - b * fr).astype(BF16)

            def inverse(k, o=o):
                y = _dot(cs_ref[rows(k), :], work[:, yrc]) - _dot(ss_ref[rows(k), :], work[:, yic])
                z = short_conv(k, o) * y
                if o == 0:
                    work[rows(k), zc] = z.astype(BF16)
                else:
                    o_ref[bi, rows(k), :] = (z * gate_ref[bi, rows(k), :].astype(F32)).astype(BF16)

            _chunk_loop(nchunk, forward)
            _chunk_loop(nchunk, inverse)

    if bt == 1:
        one(0)
    else:
        def body(bi, carry):
            one(bi)
            return carry
        lax.fori_loop(0, bt, body, 0)


def _hyena_call(layer, hy, gates, prep, dft, *, bt):
    nb, n, _ = hy.shape
    cs, ss, feats = dft

    def lspec(shape):
        nd = len(shape)
        return pl.BlockSpec((1,) + tuple(shape[1:]), lambda b: (layer,) + (0,) * (nd - 1))

    def cspec(shape):
        nd = len(shape)
        return pl.BlockSpec(tuple(shape), lambda b: (0,) * nd, pipeline_mode=pl.Buffered(1))

    big = n >= 1024
    in_specs = [
        pl.BlockSpec((bt, n, 3 * HY_W), lambda b: (b, 0, 0)),
        pl.BlockSpec((bt, n, GROUP_W), lambda b: (b, 0, 3)),
    ]
    args = [hy, gates]
    for name in ("hy_conv_w", "hy_conv_b"):
        in_specs.append(lspec(prep[name].shape))
        args.append(prep[name])
    in_specs.append(cspec(feats.shape))
    args.append(feats)
    for name in ("hy_fw1", "hy_fb1", "hy_fw2", "hy_fb2", "hy_fw3", "hy_freq", "hy_decay", "hy_bias"):
        in_specs.append(lspec(prep[name].shape))
        args.append(prep[name])
    in_specs += [cspec(cs.shape), cspec(ss.shape)]
    args += [cs, ss]

    return pl.pallas_call(
        functools.partial(_hyena_kernel, n=n, bt=bt),
        grid=(nb // bt,),
        in_specs=in_specs,
        out_specs=pl.BlockSpec((bt, n, HY_W), lambda b: (b, 0, 0)),
        out_shape=jax.ShapeDtypeStruct((nb, n, HY_W), BF16),
        scratch_shapes=[pltpu.VMEM((2, n, HY_W), F32), pltpu.VMEM((2, n, HY_W), F32),
                        pltpu.VMEM((n, 4 * HY_W), BF16),
                        pltpu.VMEM((n + 2 * HY_HALO, 3 * HY_W), BF16)],
        compiler_params=pltpu.CompilerParams(
            dimension_semantics=("arbitrary",), vmem_limit_bytes=VMEM_LIMIT),
        name="hyena_long" if big else "hyena_short",
    )(*args)


def _attn_kernel(*refs, has_ctx, lam_init, tq):
    it = iter(refs)
    q_ref, kv_ref = next(it), next(it)
    ctx_ref = next(it) if has_ctx else None
    gate_ref, ohy_ref, x_ref, mod_ref, gpost_ref, wout_ref, subln_ref, lamp_ref = (next(it) for _ in range(8))
    y_ref = next(it)

    lane = lax.broadcasted_iota(jnp.int32, (tq, MXU_W), 1)

    def band(lo, width):
        return (lane >= lo) & (lane < lo + width)

    def scores(bi, qs, kblk):
        s_l = _dot_nt(qs, kv_ref[bi, kblk])
        s_c = _dot_nt(qs, ctx_ref[bi, kblk]) if has_ctx else None
        return s_l, s_c

    def softmax_rows(s_l, s_c, narrow=False):
        def weights(s, m):
            p = jnp.exp2(s - m)
            if narrow:
                p = p.astype(BF16)
            return p, jnp.sum(p.astype(F32), axis=-1, keepdims=True)

        m = jnp.max(s_l, axis=-1, keepdims=True)
        if has_ctx:
            m = jnp.maximum(m, jnp.max(s_c, axis=-1, keepdims=True))
            p_c, l_c = weights(s_c, m)
        p_l, l = weights(s_l, m)
        if has_ctx:
            return p_l, p_c, l + l_c
        return p_l, None, l

    def pv(bi, p_l, p_c, vblk):
        o = _dot(p_l.astype(BF16), kv_ref[bi, vblk])
        if has_ctx:
            o = o + _dot(p_c.astype(BF16), ctx_ref[bi, vblk])
        return o

    def stack(parts):
        return jnp.concatenate(parts, axis=0)

    lamp = lamp_ref[0]
    lam = (jnp.exp(jnp.sum(lamp[0:1, :] * lamp[1:2, :], axis=-1, keepdims=True))
           - jnp.exp(jnp.sum(lamp[2:3, :] * lamp[3:4, :], axis=-1, keepdims=True)) + lam_init)
    lo_half = lax.broadcasted_iota(jnp.int32, (tq, LANES), 1) < GQA_HD

    bb = q_ref.shape[0]
    units = [(bi, kind, idx) for bi in range(bb)
             for kind, idx in (("mla", 0), ("mla", 1), ("gqa", 0), ("gqa", 1),
                               ("diff", 0), ("diff", 1), ("diff", 2), ("diff", 3))]

    def unit_scores(bi, kind, idx):
        if kind == "mla":
            qf = q_ref[bi, idx].astype(F32)
            masks = [band(64 * j, MLA_NOPE) | band(128 + MLA_ROPE * (2 * idx + j), MLA_ROPE) for j in range(2)]
            kblk = idx
        elif kind == "gqa":
            qf = q_ref[bi, 2 + idx].astype(F32)
            masks = [band(GQA_HD * g, GQA_HD) for g in range(2)]
            kblk = 3
        else:
            qf = q_ref[bi, 4].astype(F32)
            masks = [band(64 * idx + DIFF_HD * s, DIFF_HD) for s in range(2)]
            kblk = 4
        return scores(bi, stack([jnp.where(m, qf, 0.0) for m in masks]).astype(BF16), kblk)

    def unit_softmax(kind, s_l, s_c):
        if kind == "diff":
            p1_l, p1_c, l1 = softmax_rows(s_l[0:tq], s_c[0:tq] if has_ctx else None)
            p2_l, p2_c, l2 = softmax_rows(s_l[tq:2 * tq], s_c[tq:2 * tq] if has_ctx else None)
            w1 = 1.0 / l1
            w2 = lam / l2
            a_l = (p1_l * w1 - p2_l * w2).astype(BF16)
            a_c = (p1_c * w1 - p2_c * w2).astype(BF16) if has_ctx else None
            return a_l, a_c, None
        return softmax_rows(s_l, s_c, narrow=True)

    def unit_pv(bi, kind, idx, p_l, p_c, l):
        if kind == "diff":
            return jnp.where(band(DIFF_VD * idx, DIFF_VD), pv(bi, p_l, p_c, 5), 0.0)
        if kind == "mla":
            o = pv(bi, p_l, p_c, 2) / l
            return (jnp.where(band(128 * idx, MLA_V), o[0:tq], 0.0)
                    + jnp.where(band(128 * idx + MLA_V, MLA_V), o[tq:2 * tq], 0.0))
        o = (pv(bi, p_l, p_c, 3) / l)[:, 128:256]
        part = jnp.where(lo_half, o[0:tq], o[tq:2 * tq])
        zero = jnp.zeros_like(part)
        return jnp.concatenate([part, zero] if idx == 0 else [zero, part], axis=1)

    acc = {}
    sc_vals, sm_vals = {}, {}
    for t in range(len(units) + ATT_SCORE_LEAD + ATT_PV_LAG):
        if t < len(units):
            sc_vals[t] = unit_scores(*units[t])
        u = t - ATT_SCORE_LEAD
        if 0 <= u < len(units):
            sm_vals[u] = unit_softmax(units[u][1], *sc_vals.pop(u))
        u = t - ATT_SCORE_LEAD - ATT_PV_LAG
        if u >= 0:
            bi, kind, idx = units[u]
            c = unit_pv(bi, kind, idx, *sm_vals.pop(u))
            acc[bi, kind] = c if (bi, kind) not in acc else acc[bi, kind] + c

    def rows_of(get):
        parts = [get(bi) for bi in range(bb)]
        return parts[0] if bb == 1 else jnp.concatenate(parts, axis=0)

    y = _mix_out(rows_of(lambda bi: acc[bi, "mla"]), rows_of(lambda bi: acc[bi, "gqa"]),
                 rows_of(lambda bi: acc[bi, "diff"]), rows_of(lambda bi: gate_ref[bi]),
                 rows_of(lambda bi: ohy_ref[bi]), rows_of(lambda bi: x_ref[bi]), mod_ref[0], gpost_ref[0],
                 wout_ref[0], subln_ref[0], lam_init)
    for bi in range(bb):
        y_ref[bi] = y[bi * tq:(bi + 1) * tq]


def _mix_out(o_mla, o_gqa, od, gate, ohy, x, mod, gpost, wout, subln, lam_init):
    rows = od.shape[0]
    lane = lax.broadcasted_iota(jnp.int32, (rows, MXU_W), 1)
    od2 = od * od
    r = jnp.zeros_like(od)
    for h in range(DIFF_HEADS):
        mh = (lane >= DIFF_VD * h) & (lane < DIFF_VD * (h + 1))
        sh = jnp.sum(jnp.where(mh, od2, 0.0), axis=-1, keepdims=True)
        r = jnp.where(mh, lax.rsqrt(sh * (1.0 / DIFF_VD) + EPS), r)
    od = (od * r * subln) * (1.0 - lam_init)

    g = gate.astype(F32)
    ob = jnp.concatenate([
        (o_mla * g[:, 0:256]).astype(BF16),
        (o_gqa * g[:, 256:512]).astype(BF16),
        (od * g[:, 512:768]).astype(BF16),
        ohy,
    ], axis=1)
    out = _dot(ob, wout)
    ms = jnp.mean(out * out, axis=-1, keepdims=True)
    gate_mod = mod[:, 2 * D_MODEL:3 * D_MODEL]
    return x + gate_mod * (out * lax.rsqrt(ms + EPS) * gpost)


def _attn_call(layer, q, kv, ctx, gates, ohy, x, mod, prep, *, tq, bb=1):
    nb, n, _ = x.shape
    has_ctx = ctx is not None
    per_batch_mod = mod.shape[0] > 1
    assert bb == 1 or not (has_ctx or per_batch_mod)
    lam_init = 0.8 - 0.6 * math.exp(-0.3 * layer)

    def lspec(shape):
        nd = len(shape)
        return pl.BlockSpec((1,) + tuple(shape[1:]), lambda b, i: (layer,) + (0,) * (nd - 1))

    in_specs = [
        pl.BlockSpec((bb, N_QBLK, tq, MXU_W), lambda b, i: (b, 0, i, 0)),
        pl.BlockSpec((bb, N_KVBLK, n, MXU_W), lambda b, i: (b, 0, 0, 0)),
    ]
    args = [q, kv]
    if has_ctx:
        p = ctx.shape[3]
        in_specs.append(pl.BlockSpec((None, 1, N_KVBLK, p, MXU_W), lambda b, i: (layer, b, 0, 0, 0)))
        args.append(ctx)
    in_specs += [
        pl.BlockSpec((bb, tq, 4 * GROUP_W), lambda b, i: (b, i, 0)),
        pl.BlockSpec((bb, tq, HY_W), lambda b, i: (b, i, 0)),
        pl.BlockSpec((bb, tq, D_MODEL), lambda b, i: (b, i, 0)),
        pl.BlockSpec((1, 1, 3 * D_MODEL), (lambda b, i: (b, 0, 0)) if per_batch_mod else (lambda b, i: (0, 0, 0))),
    ]
    args += [gates, ohy, x, mod]
    for name in ("norm_post", "w_out", "diff_subln", "diff_lam"):
        in_specs.append(lspec(prep[name].shape))
        args.append(prep[name])

    return pl.pallas_call(
        functools.partial(_attn_kernel, has_ctx=has_ctx, lam_init=lam_init, tq=tq),
        grid=(nb // bb, n // tq),
        in_specs=in_specs,
        out_specs=pl.BlockSpec((bb, tq, D_MODEL), lambda b, i: (b, i, 0)),
        out_shape=jax.ShapeDtypeStruct((nb, n, D_MODEL), F32),
        compiler_params=pltpu.CompilerParams(
            dimension_semantics=("parallel", "parallel"), vmem_limit_bytes=VMEM_LIMIT),
        name="attn_lat" if has_ctx else "attn_ctx",
    )(*args)


def _rope_tables(n):
    tok = jnp.arange(n, dtype=jnp.int32)
    row = (tok // GRID_W).astype(F32)
    col = (tok % GRID_W).astype(F32)
    lane = jnp.arange(LANES, dtype=jnp.int32)
    tabs = []
    for d in (64, 32):
        m = d // 2
        i = lane % d
        ii = i % m
        f = ii % (m // 2)
        inv = ROPE_BASE ** (-(2 * f).astype(F32) / m)
        pos = jnp.where((i // m)[None, :] == 0, row[:, None], col[:, None])
        ang = pos * inv[None, :]
        sign = jnp.where(ii < m // 2, -1.0, 1.0).astype(F32)
        tabs += [jnp.cos(ang), jnp.sin(ang) * sign[None, :]]
    return tabs


def _dft_tables(n):
    k = jnp.arange(n, dtype=jnp.int32)
    unit = math.pi / (4 * n)
    kk = 2 * k[:, None] + 1
    ang_a = ((kk * (2 * LANES * jnp.arange(n // LANES, dtype=jnp.int32))[None, :]) % (8 * n)).astype(F32) * unit
    ang_b = ((kk * (2 * jnp.arange(LANES, dtype=jnp.int32) + 1)[None, :]) % (8 * n)).astype(F32) * unit
    ca, sa = jnp.cos(ang_a)[:, :, None], jnp.sin(ang_a)[:, :, None]
    cb, sb = jnp.cos(ang_b)[:, None, :], jnp.sin(ang_b)[:, None, :]
    cs = (ca * cb - sa * sb).reshape(n, n).astype(BF16)
    ss = (sa * cb + ca * sb).reshape(n, n).astype(BF16)
    th = (2 * k + 1).astype(F32) * (math.pi / (4 * n))
    ch = jnp.cos(th)[:, None]
    sh = jnp.sin(th)[:, None]
    t = jnp.linspace(0.0, 1.0, n, dtype=F32)[:, None]
    w = 2.0 * math.pi * jnp.arange(n, dtype=F32)[:, None] / n
    f = jnp.linspace(1e-4, HY_BANDS - 1, HY_BANDS, dtype=F32)[None, :]
    feats = jnp.concatenate([t, jnp.cos(f * w), -jnp.sin(f * w), jnp.zeros((n, 7), F32), ch, sh], axis=-1)
    feats = jnp.pad(feats, ((0, 0), (0, LANES - feats.shape[1])))
    return cs, ss, feats


def _prep_weights(p):
    offs = [0]
    for s in SPLIT_SIZES:
        offs.append(offs[-1] + s)
    (o_cq, o_ckv, o_kr, o_mg, o_gq, o_gk, o_gv, o_gg, o_dq, o_dk, o_dv, o_dg, o_hy, o_hg) = offs[:-1]
    w = p["w_in"]

    def cols(o, n):
        return w[:, :, o:o + n]

    def perm_heads(a, axis):
        parts = jnp.split(a, 4, axis=axis)
        return jnp.concatenate([parts[0], parts[2], parts[1], parts[3]], axis=axis)

    kr = cols(o_kr, 32)
    w_in = jnp.concatenate([
        cols(o_cq, 256), cols(o_ckv, 128), kr, kr, kr, kr,
        perm_heads(cols(o_gq, 256), 2), cols(o_gk, 128), cols(o_gv, 128),
        cols(o_dq, 256), cols(o_dk, 256), cols(o_dv, 256), cols(o_hy, 768),
        cols(o_mg, 256), perm_heads(cols(o_gg, 256), 2), cols(o_dg, 256), cols(o_hg, 256),
    ], axis=2).astype(BF16)

    wq = p["mla_wq_b"].reshape(DEPTH, MLA_Q_RANK, MLA_HEADS, MLA_NOPE + MLA_ROPE)
    wqb = jnp.concatenate([wq[..., :MLA_NOPE].reshape(DEPTH, MLA_Q_RANK, 256),
                           wq[..., MLA_NOPE:].reshape(DEPTH, MLA_Q_RANK, 128)], axis=2).astype(BF16)
    wk = p["mla_wkv_b"].reshape(DEPTH, MLA_KV_RANK, MLA_HEADS, MLA_NOPE + MLA_V)
    wkvb = jnp.concatenate([wk[..., :MLA_NOPE].reshape(DEPTH, MLA_KV_RANK, 256),
                            wk[..., MLA_NOPE:].reshape(DEPTH, MLA_KV_RANK, 256)], axis=2).astype(BF16)

    wo = p["w_out"]
    w_out = jnp.concatenate([wo[:, 0:256], perm_heads(wo[:, 256:512], 1), wo[:, 512:]], axis=1).astype(BF16)

    def pad_to(a, shape):
        return jnp.pad(a, [(0, s - d) for d, s in zip(a.shape, shape)])

    return {
        "norm_pre": p["norm_pre"].reshape(DEPTH, 1, D_MODEL),
        "norm_post": p["norm_post"].reshape(DEPTH, 1, D_MODEL),
        "w_in": w_in, "wqb": wqb, "wkvb": wkvb, "w_out": w_out,
        "mla_q_norm": p["mla_q_norm"].reshape(DEPTH, 1, MLA_Q_RANK),
        "mla_kv_norm": p["mla_kv_norm"].reshape(DEPTH, 1, MLA_KV_RANK),
        "gqa_q_norm": jnp.tile(p["gqa_q_norm"], (1, 4)).reshape(DEPTH, 1, 256),
        "gqa_k_norm": jnp.tile(p["gqa_k_norm"], (1, 2)).reshape(DEPTH, 1, 128),
        "diff_subln": jnp.tile(p["diff_subln"], (1, 4)).reshape(DEPTH, 1, 256),
        "diff_lam": jnp.stack([p["diff_lq1"], p["diff_lk1"], p["diff_lq2"], p["diff_lk2"]], axis=1),
        "hy_conv_w": p["hy_conv_w"],
        "hy_conv_b": p["hy_conv_b"].reshape(DEPTH, 1, 3 * HY_W),
        "hy_fw1": pad_to(p["hy_fw1"], (DEPTH, LANES, LANES)),
        "hy_fb1": pad_to(p["hy_fb1"].reshape(DEPTH, 1, HY_FFN), (DEPTH, 1, LANES)),
        "hy_fw2": pad_to(p["hy_fw2"], (DEPTH, LANES, LANES)),
        "hy_fb2": pad_to(p["hy_fb2"].reshape(DEPTH, 1, HY_FFN), (DEPTH, 1, LANES)),
        "hy_fw3": pad_to(p["hy_fw3"], (DEPTH, LANES, 4 * HY_W)),
        "hy_freq": pad_to(p["hy_freq"], (DEPTH, 2, LANES)),
        "hy_decay": p["hy_decay"].reshape(DEPTH, 1, HY_W),
        "hy_bias": p["hy_bias"],
    }


def kernel(x_prompt, x_sample, cache_mla_ckv, cache_mla_krope, cache_gqa_k, cache_gqa_v, cache_diff_k, cache_diff_v, c, c_ctx, norm_pre, norm_post, ada_w, ada_b, w_in, w_out, mla_q_norm, mla_wq_b, mla_kv_norm, mla_wkv_b, gqa_q_norm, gqa_k_norm, diff_lq1, diff_lk1, diff_lq2, diff_lk2, diff_subln, hy_conv_w, hy_conv_b, hy_fw1, hy_fb1, hy_fw2, hy_fb2, hy_fw3, hy_freq, hy_decay, hy_bias):
    nb_p, n_p, _ = x_prompt.shape
    nb_s, n_s, _ = x_sample.shape
    past = cache_mla_ckv.shape[2]

    prep = _prep_weights(dict(
        norm_pre=norm_pre, norm_post=norm_post, w_in=w_in, w_out=w_out, mla_q_norm=mla_q_norm,
        mla_wq_b=mla_wq_b, mla_kv_norm=mla_kv_norm, mla_wkv_b=mla_wkv_b, gqa_q_norm=gqa_q_norm,
        gqa_k_norm=gqa_k_norm, diff_lq1=diff_lq1, diff_lk1=diff_lk1, diff_lq2=diff_lq2, diff_lk2=diff_lk2,
        diff_subln=diff_subln, hy_conv_w=hy_conv_w, hy_conv_b=hy_conv_b, hy_fw1=hy_fw1, hy_fb1=hy_fb1,
        hy_fw2=hy_fw2, hy_fb2=hy_fb2, hy_fw3=hy_fw3, hy_freq=hy_freq, hy_decay=hy_decay, hy_bias=hy_bias))
    tabs = _rope_tables(n_s)
    dft_p = _dft_tables(n_p)
    dft_s = _dft_tables(n_s)

    cc8 = jnp.concatenate([c_ctx[None, :], c, jnp.zeros((8 - 1 - nb_s, D_MODEL), F32)], axis=0)
    mod = _ada_call(cc8, ada_w, ada_b)

    ctx = _ctx_call(
        cache_mla_ckv, jnp.tile(cache_mla_krope, (1, 1, 1, 4)),
        cache_gqa_k.reshape(nb_s, DEPTH, past, 128), cache_gqa_v.reshape(nb_s, DEPTH, past, 128),
        cache_diff_k.reshape(nb_s, DEPTH, past, 256), cache_diff_v.reshape(nb_s, DEPTH, past, 256),
        prep["wkvb"])

    y_p, y_s = x_prompt, x_sample
    st = None
    for i in range(DEPTH):
        mod_p = mod[i, 0:1].reshape(1, 1, 3 * D_MODEL)
        mod_s = mod[i, 1:1 + nb_s].reshape(nb_s, 1, 3 * D_MODEL)
        outs = _inproj_call(i, y_p, mod_p, prep, None, rope=False, states=True, tm=n_p, bb=4, prev_states=st)
        q, kv, hy, gates = outs[:4]
        st = outs[4:]
        ohy = _hyena_call(i, hy, gates, prep, dft_p, bt=4)
        y_p = _attn_call(i, q, kv, None, gates, ohy, y_p, mod_p, prep, tq=n_p, bb=4)
        q, kv, hy, gates = _inproj_call(i, y_s, mod_s, prep, tabs, rope=True, states=False, tm=1024)
        ohy = _hyena_call(i, hy, gates, prep, dft_s, bt=1)
        y_s = _attn_call(i, q, kv, ctx, gates, ohy, y_s, mod_s, prep, tq=128)

    new_ckv, new_kr = st[0], st[1]
    new_gk = st[2].reshape(nb_p, DEPTH, n_p, GQA_KV_HEADS, GQA_HD)
    new_gv = st[3].reshape(nb_p, DEPTH, n_p, GQA_KV_HEADS, GQA_HD)
    new_dk = st[4].reshape(nb_p, DEPTH, n_p, DIFF_HEADS, 2 * DIFF_HD)
    new_dv = st[5].reshape(nb_p, DEPTH, n_p, DIFF_HEADS, DIFF_VD)
    return (y_p, y_s, new_ckv, new_kr, new_gk, new_gv, new_dk, new_dv)
```

```python
import functools
import math

import jax
import jax.numpy as jnp
from jax import lax
from jax.experimental import pallas as pl
from jax.experimental.pallas import tpu as pltpu

F32 = jnp.float32
BF16 = jnp.bfloat16

D_MODEL = 1024
DEPTH = 2
GRID_W = 64
ROPE_BASE = 10000.0
EPS = 1e-6
GROUP_W = 256
MLA_HEADS = 4
MLA_NOPE = 64
MLA_ROPE = 32
MLA_V = 64
MLA_Q_RANK = 256
MLA_KV_RANK = 128
GQA_HEADS = 4
GQA_KV_HEADS = 2
GQA_HD = 64
DIFF_HEADS = 4
DIFF_VD = 64
DIFF_HD = 32
HY_W = 256
HY_EMB = 33
HY_BANDS = 16
HY_FFN = 64
SPLIT_SIZES = (256, 128, 32, 256, 256, 128, 128, 256, 256, 256, 256, 256, 768, 256)
P_IN = sum(SPLIT_SIZES)

LANES = 128
MOD_ROWS = 8
MXU_W = 256
P_PAD = 14 * MXU_W
N_QBLK = 5
N_KVBLK = 6
LOG2E = math.log2(math.e)
INPROJ_CHUNK = 256
HY_CHUNK = 512
HY_HALO = 16
ATT_SCORE_LEAD = 2
ATT_PV_LAG = 2
NT_DIMS = (((1,), (1,)), ((), ()))
VMEM_LIMIT = 56 * 1024 * 1024


def _dot(a, b):
    return jnp.dot(a, b, preferred_element_type=F32)


def _dot_nt(a, b):
    return lax.dot_general(a, b, NT_DIMS, preferred_element_type=F32)


def _dot_split(a, b):
    a_hi = a.astype(BF16)
    b_hi = b.astype(BF16)
    a_lo = (a - a_hi.astype(F32)).astype(BF16)
    b_lo = (b - b_hi.astype(F32)).astype(BF16)
    return _dot(a_hi, b_hi) + _dot(a_hi, b_lo) + _dot(a_lo, b_hi)


def _rms_full(v, g):
    ms = jnp.mean(v * v, axis=-1, keepdims=True)
    return v * lax.rsqrt(ms + EPS) * g


def _rms_heads64(v, g):
    outs = []
    lane = lax.broadcasted_iota(jnp.int32, (v.shape[0], LANES), 1)
    lo = lane < 64
    for b in range(v.shape[1] // LANES):
        vb = v[:, b * LANES:(b + 1) * LANES]
        v2 = vb * vb
        s_lo = jnp.sum(jnp.where(lo, v2, 0.0), axis=-1, keepdims=True)
        s_hi = jnp.sum(jnp.where(lo, 0.0, v2), axis=-1, keepdims=True)
        r = jnp.where(lo, lax.rsqrt(s_lo * (1.0 / 64) + EPS), lax.rsqrt(s_hi * (1.0 / 64) + EPS))
        outs.append(vb * r)
    out = outs[0] if len(outs) == 1 else jnp.concatenate(outs, axis=1)
    return out * g


def _rope(v, cos_t, sin_t, half):
    lane = lax.broadcasted_iota(jnp.int32, (v.shape[0], LANES), 1)
    first = (lane % (2 * half)) < half
    outs = []
    for b in range(v.shape[1] // LANES):
        vb = v[:, b * LANES:(b + 1) * LANES]
        up = pltpu.roll(vb, LANES - half, axis=1)
        dn = pltpu.roll(vb, half, axis=1)
        outs.append(vb * cos_t + jnp.where(first, up, dn) * sin_t)
    return outs[0] if len(outs) == 1 else jnp.concatenate(outs, axis=1)


def _ada_kernel(cc_ref, w_ref, b_ref, o_ref):
    a = cc_ref[...]
    a = a * jax.nn.sigmoid(a)
    o_ref[0] = _dot(a.astype(BF16), w_ref[0].astype(BF16)) + b_ref[0]


def _ada_call(cc8, ada_w, ada_b):
    tn = 768
    return pl.pallas_call(
        _ada_kernel,
        grid=(DEPTH, 3 * D_MODEL // tn),
        in_specs=[
            pl.BlockSpec((MOD_ROWS, D_MODEL), lambda l, j: (0, 0)),
            pl.BlockSpec((1, D_MODEL, tn), lambda l, j: (l, 0, j)),
            pl.BlockSpec((1, 1, tn), lambda l, j: (l, 0, j)),
        ],
        out_specs=pl.BlockSpec((1, MOD_ROWS, tn), lambda l, j: (l, 0, j)),
        out_shape=jax.ShapeDtypeStruct((DEPTH, MOD_ROWS, 3 * D_MODEL), F32),
        compiler_params=pltpu.CompilerParams(dimension_semantics=("parallel", "parallel")),
        name="ada_mod",
    )(cc8, ada_w, ada_b.reshape(DEPTH, 1, 3 * D_MODEL))


def _ctx_kernel(ckv_ref, kr4_ref, gk_ref, gv_ref, dk_ref, dv_ref, wkvb_ref, o_ref):
    kvp = _dot(ckv_ref[0, 0].astype(BF16), wkvb_ref[0])
    kr4 = kr4_ref[0, 0].astype(BF16)
    o_ref[0, 0, 0] = jnp.concatenate([kvp[:, 0:128].astype(BF16), kr4], axis=1)
    o_ref[0, 0, 1] = jnp.concatenate([kvp[:, 128:256].astype(BF16), kr4], axis=1)
    o_ref[0, 0, 2] = kvp[:, 256:512].astype(BF16)
    gv = gv_ref[0, 0].astype(BF16)
    o_ref[0, 0, 3] = jnp.concatenate([gk_ref[0, 0].astype(BF16), gv], axis=1)
    o_ref[0, 0, 4] = dk_ref[0, 0].astype(BF16)
    o_ref[0, 0, 5] = dv_ref[0, 0].astype(BF16)


def _ctx_call(ckv, kr4, gk, gv, dk, dv, wkvb):
    nb, _, p, _ = ckv.shape

    def spec(w):
        return pl.BlockSpec((1, 1, p, w), lambda l, b: (b, l, 0, 0))

    return pl.pallas_call(
        _ctx_kernel,
        grid=(DEPTH, nb),
        in_specs=[spec(128), spec(128), spec(128), spec(128), spec(256), spec(256),
                  pl.BlockSpec((1, MLA_KV_RANK, 512), lambda l, b: (l, 0, 0))],
        out_specs=pl.BlockSpec((1, 1, N_KVBLK, p, MXU_W), lambda l, b: (l, b, 0, 0, 0)),
        out_shape=jax.ShapeDtypeStruct((DEPTH, nb, N_KVBLK, p, MXU_W), BF16),
        compiler_params=pltpu.CompilerParams(dimension_semantics=("parallel", "parallel")),
        name="ctx_kv",
    )(ckv, kr4, gk, gv, dk, dv, wkvb)


def _mod_spec(layer, per_batch):
    if per_batch:
        return pl.BlockSpec((1, 1, 3 * D_MODEL), lambda b, i: (layer * MOD_ROWS + 1 + b, 0, 0))
    return pl.BlockSpec((1, 1, 3 * D_MODEL), lambda b, i: (layer * MOD_ROWS, 0, 0))


def _inproj_kernel(*refs, rope, states, n_alias):
    it = iter(refs)
    x_ref, mod_ref, gpre_ref, w_ref, wqb_ref, wkvb_ref = (next(it) for _ in range(6))
    gq_mla_ref, gkv_mla_ref, gqn_ref, gkn_ref = (next(it) for _ in range(4))
    if rope:
        cos64_ref, sin64_ref, cos32_ref, sin32_ref = (next(it) for _ in range(4))
    for _ in range(n_alias):
        next(it)
    q_ref, kv_ref, hy_ref, gate_ref = (next(it) for _ in range(4))
    if states:
        s_ckv_ref, s_kr_ref, s_gk_ref, s_gv_ref, s_dk_ref, s_dv_ref = (next(it) for _ in range(6))
        if n_alias == 0:
            for s_ref in (s_ckv_ref, s_kr_ref, s_gk_ref, s_gv_ref, s_dk_ref, s_dv_ref):
                s_ref[:, 1:] = jnp.zeros((s_ref.shape[0], s_ref.shape[1] - 1) + s_ref.shape[2:], F32)
    h_scrs = list(it)

    mod = mod_ref[0]
    shift = mod[:, 0:D_MODEL]
    scale = mod[:, D_MODEL:2 * D_MODEL]
    bb, tm = x_ref.shape[0], x_ref.shape[1]
    chunks = [(bi, slice(r0, r0 + INPROJ_CHUNK)) for bi in range(bb) for r0 in range(0, tm, INPROJ_CHUNK)]

    for (bi, rs), h_scr in zip(chunks, h_scrs):
        x = x_ref[bi, rs, :]
        ms = jnp.mean(x * x, axis=-1, keepdims=True)
        h = (x * lax.rsqrt(ms + EPS) * gpre_ref[0]) * (1.0 + scale) + shift
        h_scr[...] = h.astype(BF16)

    def proj(g):
        w = w_ref[0, :, g * MXU_W:(g + 1) * MXU_W]
        parts = [_dot(h_scr[...], w) for h_scr in h_scrs]
        return parts[0] if len(parts) == 1 else jnp.concatenate(parts, axis=0)

    def put(ref, lead, v, cols=slice(None)):
        for c, (bi, rs) in enumerate(chunks):
            ref[(bi,) + lead + (rs, cols)] = v[c * INPROJ_CHUNK:(c + 1) * INPROJ_CHUNK]

    def table(ref):
        parts = [ref[rs, :] for _, rs in chunks]
        return parts[0] if len(parts) == 1 else jnp.concatenate(parts, axis=0)

    def rope32(v):
        return _rope(v, table(cos32_ref), table(sin32_ref), 8) if rope else v

    def rope64(v):
        return _rope(v, table(cos64_ref), table(sin64_ref), 16) if rope else v

    cq = _rms_full(proj(0), gq_mla_ref[0])
    q3 = _dot(cq.astype(BF16), wqb_ref[0]) * (LOG2E * (MLA_NOPE + MLA_ROPE) ** -0.5)
    qr = rope32(q3[:, 256:384]).astype(BF16)
    put(q_ref, (0,), jnp.concatenate([q3[:, 0:128].astype(BF16), qr], axis=1))
    put(q_ref, (1,), jnp.concatenate([q3[:, 128:256].astype(BF16), qr], axis=1))

    z1 = proj(1)
    ckv = _rms_full(z1[:, 0:128], gkv_mla_ref[0])
    kr4 = z1[:, 128:256]
    if states:
        put(s_ckv_ref, (0,), ckv)
        put(s_kr_ref, (0,), kr4[:, 0:MLA_ROPE])
    kvp = _dot(ckv.astype(BF16), wkvb_ref[0])
    kr4 = rope32(kr4).astype(BF16)
    put(kv_ref, (0,), jnp.concatenate([kvp[:, 0:128].astype(BF16), kr4], axis=1))
    put(kv_ref, (1,), jnp.concatenate([kvp[:, 128:256].astype(BF16), kr4], axis=1))
    put(kv_ref, (2,), kvp[:, 256:512].astype(BF16))

    gq = (rope64(_rms_heads64(proj(2), gqn_ref[0])) * (LOG2E * GQA_HD ** -0.5)).astype(BF16)
    zq = jnp.zeros((gq.shape[0], LANES), BF16)
    put(q_ref, (2,), jnp.concatenate([gq[:, 0:128], zq], axis=1))
    put(q_ref, (3,), jnp.concatenate([gq[:, 128:256], zq], axis=1))
    z3 = proj(3)
    gk = _rms_heads64(z3[:, 0:128], gkn_ref[0])
    gv = z3[:, 128:256]
    if states:
        put(s_gk_ref, (0,), gk)
        put(s_gv_ref, (0,), gv)
    put(kv_ref, (3,), jnp.concatenate([rope64(gk).astype(BF16), gv.astype(BF16)], axis=1))

    put(q_ref, (4,), (rope32(proj(4)) * (LOG2E * DIFF_HD ** -0.5)).astype(BF16))
    dk = proj(5)
    dv = proj(6)
    if states:
        put(s_dk_ref, (0,), dk)
        put(s_dv_ref, (0,), dv)
    put(kv_ref, (4,), rope32(dk).astype(BF16))
    put(kv_ref, (5,), dv.astype(BF16))

    for j in range(3):
        put(hy_ref, (), proj(7 + j).astype(BF16), slice(j * MXU_W, (j + 1) * MXU_W))
    for j in range(4):
        g = proj(10 + j)
        put(gate_ref, (), (g * jax.nn.sigmoid(g)).astype(BF16), slice(j * MXU_W, (j + 1) * MXU_W))


def _inproj_call(layer, x, mod, prep, tabs, *, rope, states, tm, per_batch_mod, bb=1, prev_states=None):
    nb, n, _ = x.shape
    assert bb == 1 or not per_batch_mod
    tiles = n // tm

    def lspec(shape):
        nd = len(shape)
        return pl.BlockSpec((1,) + tuple(shape[1:]), lambda b, i: (layer,) + (0,) * (nd - 1))

    in_specs = [
        pl.BlockSpec((bb, tm, D_MODEL), lambda b, i: (b, i, 0)),
        _mod_spec(layer, per_batch_mod),
    ]
    args = [x, mod]
    for name in ("norm_pre", "w_in", "wqb", "wkvb", "mla_q_norm", "mla_kv_norm", "gqa_q_norm", "gqa_k_norm"):
        a = prep[name]
        in_specs.append(lspec(a.shape))
        args.append(a)
    if rope:
        for t in tabs:
            in_specs.append(pl.BlockSpec((tm, LANES), lambda b, i: (i, 0)))
            args.append(t)

    out_shape = [
        jax.ShapeDtypeStruct((nb, N_QBLK, n, MXU_W), BF16),
        jax.ShapeDtypeStruct((nb, N_KVBLK, n, MXU_W), BF16),
        jax.ShapeDtypeStruct((nb, n, 3 * HY_W), BF16),
        jax.ShapeDtypeStruct((nb, n, 4 * GROUP_W), BF16),
    ]
    out_specs = [
        pl.BlockSpec((bb, N_QBLK, tm, MXU_W), lambda b, i: (b, 0, i, 0)),
        pl.BlockSpec((bb, N_KVBLK, tm, MXU_W), lambda b, i: (b, 0, i, 0)),
        pl.BlockSpec((bb, tm, 3 * HY_W), lambda b, i: (b, i, 0)),
        pl.BlockSpec((bb, tm, 4 * GROUP_W), lambda b, i: (b, i, 0)),
    ]
    aliases = {}
    if states:
        for w in (MLA_KV_RANK, MLA_ROPE, 128, 128, 256, 256):
            out_shape.append(jax.ShapeDtypeStruct((nb, DEPTH, n, w), F32))
            if prev_states is None:
                out_specs.append(pl.BlockSpec((bb, DEPTH, tm, w), lambda b, i: (b, 0, i, 0)))
            else:
                out_specs.append(pl.BlockSpec((bb, 1, tm, w), lambda b, i: (b, layer, i, 0)))
        if prev_states is not None:
            for j, a in enumerate(prev_states):
                aliases[len(args)] = 4 + j
                in_specs.append(pl.BlockSpec(memory_space=pl.ANY))
                args.append(a)

    return pl.pallas_call(
        functools.partial(_inproj_kernel, rope=rope, states=states, n_alias=len(aliases)),
        grid=(nb // bb, tiles),
        in_specs=in_specs,
        out_specs=out_specs,
        out_shape=out_shape,
        scratch_shapes=[pltpu.VMEM((INPROJ_CHUNK, D_MODEL), BF16)] * (bb * tm // INPROJ_CHUNK),
        input_output_aliases=aliases,
        compiler_params=pltpu.CompilerParams(
            dimension_semantics=("parallel", "parallel"), vmem_limit_bytes=VMEM_LIMIT),
        name="inproj_rope" if rope else "inproj_ctx",
    )(*args)


def _chunk_loop(nchunk, body):
    if nchunk == 1:
        body(0)
    else:
        def wrapped(k, carry):
            body(k)
            return carry
        lax.fori_loop(0, nchunk, wrapped, 0)


def _hyena_kernel(hy_ref, gate_ref, cw_ref, cb_ref, feats_ref, fw1_ref, fb1_ref, fw2_ref, fb2_ref,
                  fw3_ref, freq_ref, decay_ref, bias_ref, cs_ref, ss_ref,
                  o_ref, fr_scr, fi_scr, work, hy_pad, *, n, bt):
    rc = min(n, HY_CHUNK)
    nchunk = n // rc
    rowc = lax.broadcasted_iota(jnp.int32, (rc, HY_W), 0)

    def rows(k):
        if nchunk == 1:
            return slice(0, rc)
        return pl.ds(pl.multiple_of(k * rc, rc), rc)

    @pl.when(pl.program_id(0) == 0)
    def _filters():
        freq = freq_ref[0]
        bias = bias_ref[0]

        def taps(k):
            ft = feats_ref[rows(k), :]
            t = ft[:, 0:1]
            h1 = jnp.sin(freq[0:1, :] * (_dot_split(ft, fw1_ref[0]) + fb1_ref[0]))
            h2 = jnp.sin(freq[1:2, :] * (_dot_split(h1, fw2_ref[0]) + fb2_ref[0]))
            h3 = _dot_split(h2, fw3_ref[0])
            dec = jnp.exp(-t * jnp.abs(decay_ref[0]))
            lag0 = (rowc + k * rc) == 0
            for o in range(2):
                hf = h3[:, o * 512:o * 512 + 256] * dec
                hb = h3[:, o * 512 + 256:o * 512 + 512] * dec
                hf = hf + jnp.where(lag0, bias[o:o + 1, :], 0.0)
                hb = jnp.where(lag0, 0.0, hb)
                work[rows(k), o * 512:o * 512 + 256] = hf.astype(BF16)
                work[rows(k), o * 512 + 256:o * 512 + 512] = hb.astype(BF16)

        _chunk_loop(nchunk, taps)

        def spectra(k):
            ft = feats_ref[rows(k), :]
            c = ft[:, HY_EMB + 7:HY_EMB + 8]
            s = ft[:, HY_EMB + 8:HY_EMB + 9]
            p = _dot(cs_ref[rows(k), :], work[...])
            q = _dot(ss_ref[rows(k), :], work[...])
            for o in range(2):
                pf, pb = p[:, o * 512:o * 512 + 256], p[:, o * 512 + 256:o * 512 + 512]
                qf, qb = q[:, o * 512:o * 512 + 256], q[:, o * 512 + 256:o * 512 + 512]
                fr_scr[o, rows(k), :] = (c * (pf + pb) + s * (qf + qb)) * (1.0 / n)
                fi_scr[o, rows(k), :] = (s * (pf - pb) - c * (qf - qb)) * (1.0 / n)

        _chunk_loop(nchunk, spectra)

    cw = cw_ref[0]
    cb = cb_ref[0]
    hy_pad[0:HY_HALO, :] = jnp.zeros((HY_HALO, 3 * HY_W), BF16)
    hy_pad[HY_HALO + n:2 * HY_HALO + n, :] = jnp.zeros((HY_HALO, 3 * HY_W), BF16)

    def short_conv(k, j):
        sl = slice(j * HY_W, (j + 1) * HY_W)
        if nchunk == 1:
            ext = hy_pad[:, sl].astype(F32)
        else:
            ext = hy_pad[pl.ds(pl.multiple_of(k * rc, rc), rc + 2 * HY_HALO), sl].astype(F32)
        um = pltpu.roll(ext, 1, axis=0)[HY_HALO:HY_HALO + rc]
        up = pltpu.roll(ext, rc + 2 * HY_HALO - 1, axis=0)[HY_HALO:HY_HALO + rc]
        u = ext[HY_HALO:HY_HALO + rc]
        return um * cw[0:1, sl] + u * cw[1:2, sl] + up * cw[2:3, sl] + cb[:, sl]

    zc, yrc, yic = slice(0, 256), slice(256, 512), slice(512, 768)

    def one(bi):
        hy_pad[HY_HALO:HY_HALO + n, :] = hy_ref[bi]

        def conv_v(k):
            work[rows(k), zc] = short_conv(k, 2).astype(BF16)

        _chunk_loop(nchunk, conv_v)
        for o in range(2):
            def forward(k, o=o):
                a = _dot(cs_ref[rows(k), :], work[:, zc])
                b = _dot(ss_ref[rows(k), :], work[:, zc])
                fr = fr_scr[o, rows(k), :]
                fi = fi_scr[o, rows(k), :]
                work[rows(k), yrc] = (a * fr + b * fi).astype(BF16)
                work[rows(k), yic] = (a * fi - b * fr).astype(BF16)

            def inverse(k, o=o):
                y = _dot(cs_ref[rows(k), :], work[:, yrc]) - _dot(ss_ref[rows(k), :], work[:, yic])
                z = short_conv(k, o) * y
                if o == 0:
                    work[rows(k), zc] = z.astype(BF16)
                else:
                    o_ref[bi, rows(k), :] = (z * gate_ref[bi, rows(k), :].astype(F32)).astype(BF16)

            _chunk_loop(nchunk, forward)
            _chunk_loop(nchunk, inverse)

    if bt == 1:
        one(0)
    else:
        def body(bi, carry):
            one(bi)
            return carry
        lax.fori_loop(0, bt, body, 0)


def _hyena_call(layer, hy, gates, prep, dft, *, bt):
    nb, n, _ = hy.shape
    cs, ss, feats = dft

    def lspec(shape):
        nd = len(shape)
        return pl.BlockSpec((1,) + tuple(shape[1:]), lambda b: (layer,) + (0,) * (nd - 1))

    def cspec(shape):
        nd = len(shape)
        return pl.BlockSpec(tuple(shape), lambda b: (0,) * nd, pipeline_mode=pl.Buffered(1))

    big = n >= 1024
    in_specs = [
        pl.BlockSpec((bt, n, 3 * HY_W), lambda b: (b, 0, 0)),
        pl.BlockSpec((bt, n, GROUP_W), lambda b: (b, 0, 3)),
    ]
    args = [hy, gates]
    for name in ("hy_conv_w", "hy_conv_b"):
        in_specs.append(lspec(prep[name].shape))
        args.append(prep[name])
    in_specs.append(cspec(feats.shape))
    args.append(feats)
    for name in ("hy_fw1", "hy_fb1", "hy_fw2", "hy_fb2", "hy_fw3", "hy_freq", "hy_decay", "hy_bias"):
        in_specs.append(lspec(prep[name].shape))
        args.append(prep[name])
    in_specs += [cspec(cs.shape), cspec(ss.shape)]
    args += [cs, ss]

    return pl.pallas_call(
        functools.partial(_hyena_kernel, n=n, bt=bt),
        grid=(nb // bt,),
        in_specs=in_specs,
        out_specs=pl.BlockSpec((bt, n, HY_W), lambda b: (b, 0, 0)),
        out_shape=jax.ShapeDtypeStruct((nb, n, HY_W), BF16),
        scratch_shapes=[pltpu.VMEM((2, n, HY_W), F32), pltpu.VMEM((2, n, HY_W), F32),
                        pltpu.VMEM((n, 4 * HY_W), BF16),
                        pltpu.VMEM((n + 2 * HY_HALO, 3 * HY_W), BF16)],
        compiler_params=pltpu.CompilerParams(
            dimension_semantics=("arbitrary",), vmem_limit_bytes=VMEM_LIMIT),
        name="hyena_long" if big else "hyena_short",
    )(*args)


def _attn_kernel(*refs, has_ctx, lam_init, tq, ur):
    it = iter(refs)
    q_ref, kv_ref = next(it), next(it)
    ctx_ref = next(it) if has_ctx else None
    gate_ref, ohy_ref, x_ref, mod_ref, gpost_ref, wout_ref, subln_ref, lamp_ref = (next(it) for _ in range(8))
    y_ref = next(it)

    lane = lax.broadcasted_iota(jnp.int32, (ur, MXU_W), 1)

    def band(lo, width):
        return (lane >= lo) & (lane < lo + width)

    def scores(bi, qs, kblk):
        s_l = _dot_nt(qs, kv_ref[bi, kblk])
        s_c = _dot_nt(qs, ctx_ref[bi, kblk]) if has_ctx else None
        return s_l, s_c

    def softmax_rows(s_l, s_c, narrow=False):
        def weights(s, m):
            p = jnp.exp2(s - m)
            if narrow:
                p = p.astype(BF16)
            return p, jnp.sum(p.astype(F32), axis=-1, keepdims=True)

        m = jnp.max(s_l, axis=-1, keepdims=True)
        if has_ctx:
            m = jnp.maximum(m, jnp.max(s_c, axis=-1, keepdims=True))
            p_c, l_c = weights(s_c, m)
        p_l, l = weights(s_l, m)
        if has_ctx:
            return p_l, p_c, l + l_c
        return p_l, None, l

    def pv(bi, p_l, p_c, vblk):
        o = _dot(p_l.astype(BF16), kv_ref[bi, vblk])
        if has_ctx:
            o = o + _dot(p_c.astype(BF16), ctx_ref[bi, vblk])
        return o

    def stack(parts):
        return jnp.concatenate(parts, axis=0)

    lamp = lamp_ref[0]
    lam = (jnp.exp(jnp.sum(lamp[0:1, :] * lamp[1:2, :], axis=-1, keepdims=True))
           - jnp.exp(jnp.sum(lamp[2:3, :] * lamp[3:4, :], axis=-1, keepdims=True)) + lam_init)
    lo_half = lax.broadcasted_iota(jnp.int32, (ur, LANES), 1) < GQA_HD

    qblocks = [(bi, slice(r0, r0 + ur)) for bi in range(q_ref.shape[0]) for r0 in range(0, tq, ur)]
    units = [(qb, kind, idx) for qb in qblocks
             for kind, idx in (("mla", 0), ("mla", 1), ("gqa", 0), ("gqa", 1),
                               ("diff", 0), ("diff", 1), ("diff", 2), ("diff", 3))]

    def unit_scores(qb, kind, idx):
        bi, rs = qb
        if kind == "mla":
            qf = q_ref[bi, idx, rs, :].astype(F32)
            masks = [band(64 * j, MLA_NOPE) | band(128 + MLA_ROPE * (2 * idx + j), MLA_ROPE) for j in range(2)]
            kblk = idx
        elif kind == "gqa":
            qf = q_ref[bi, 2 + idx, rs, :].astype(F32)
            masks = [band(GQA_HD * g, GQA_HD) for g in range(2)]
            kblk = 3
        else:
            qf = q_ref[bi, 4, rs, :].astype(F32)
            masks = [band(64 * idx + DIFF_HD * s, DIFF_HD) for s in range(2)]
            kblk = 4
        return scores(bi, stack([jnp.where(m, qf, 0.0) for m in masks]).astype(BF16), kblk)

    def unit_softmax(kind, s_l, s_c):
        if kind == "diff":
            p1_l, p1_c, l1 = softmax_rows(s_l[0:ur], s_c[0:ur] if has_ctx else None)
            p2_l, p2_c, l2 = softmax_rows(s_l[ur:2 * ur], s_c[ur:2 * ur] if has_ctx else None)
            w1 = 1.0 / l1
            w2 = lam / l2
            a_l = (p1_l * w1 - p2_l * w2).astype(BF16)
            a_c = (p1_c * w1 - p2_c * w2).astype(BF16) if has_ctx else None
            return a_l, a_c, None
        return softmax_rows(s_l, s_c, narrow=True)

    def unit_pv(bi, kind, idx, p_l, p_c, l):
        if kind == "diff":
            return jnp.where(band(DIFF_VD * idx, DIFF_VD), pv(bi, p_l, p_c, 5), 0.0)
        if kind == "mla":
            o = pv(bi, p_l, p_c, 2) / l
            return (jnp.where(band(128 * idx, MLA_V), o[0:ur], 0.0)
                    + jnp.where(band(128 * idx + MLA_V, MLA_V), o[ur:2 * ur], 0.0))
        o = (pv(bi, p_l, p_c, 3) / l)[:, 128:256]
        part = jnp.where(lo_half, o[0:ur], o[ur:2 * ur])
        zero = jnp.zeros_like(part)
        return jnp.concatenate([part, zero] if idx == 0 else [zero, part], axis=1)

    early_proj = len(qblocks) > 1
    mixer_cols = {"mla": 0, "gqa": GROUP_W, "diff": 2 * GROUP_W}
    units_left = {(k, kind): n for k in range(len(qblocks)) for kind, n in (("mla", 2), ("gqa", 2), ("diff", 4))}
    gated, out_acc = {}, {}

    def project(k, v, lo):
        c = _dot(v, wout_ref[0, lo:lo + v.shape[1], :])
        out_acc[k] = c if k not in out_acc else out_acc[k] + c

    def mixer_done(k, kind, o):
        bi, rs = qblocks[k]
        lo = mixer_cols[kind]
        if kind == "diff":
            o2 = o * o
            r = jnp.zeros_like(o)
            for h in range(DIFF_HEADS):
                mh = band(DIFF_VD * h, DIFF_VD)
                sh = jnp.sum(jnp.where(mh, o2, 0.0), axis=-1, keepdims=True)
                r = jnp.where(mh, lax.rsqrt(sh * (1.0 / DIFF_VD) + EPS), r)
            o = (o * r * subln_ref[0]) * (1.0 - lam_init)
        piece = (o * gate_ref[bi, rs, lo:lo + GROUP_W].astype(F32)).astype(BF16)
        if early_proj:
            project(k, piece, lo)
        else:
            gated[k, kind] = piece
        if kind == "diff":
            if early_proj:
                project(k, ohy_ref[bi, rs, :], 3 * GROUP_W)
            else:
                project(k, jnp.concatenate([gated.pop((k, "mla")), gated.pop((k, "gqa")), gated.pop((k, "diff")),
                                            ohy_ref[bi, rs, :]], axis=1), 0)
            out = out_acc.pop(k)
            ms = jnp.mean(out * out, axis=-1, keepdims=True)
            gate_mod = mod_ref[0][:, 2 * D_MODEL:3 * D_MODEL]
            y_ref[bi, rs, :] = x_ref[bi, rs, :] + gate_mod * (out * lax.rsqrt(ms + EPS) * gpost_ref[0])

    acc = {}
    sc_vals, sm_vals = {}, {}
    for t in range(len(units) + ATT_SCORE_LEAD + ATT_PV_LAG):
        if t < len(units):
            sc_vals[t] = unit_scores(*units[t])
        u = t - ATT_SCORE_LEAD
        if 0 <= u < len(units):
            sm_vals[u] = unit_softmax(units[u][1], *sc_vals.pop(u))
        u = t - ATT_SCORE_LEAD - ATT_PV_LAG
        if u >= 0:
            qb, kind, idx = units[u]
            c = unit_pv(qb[0], kind, idx, *sm_vals.pop(u))
            key = (qblocks.index(qb), kind)
            acc[key] = c if key not in acc else acc[key] + c
            units_left[key] -= 1
            if units_left[key] == 0:
                mixer_done(key[0], kind, acc.pop(key))


def _attn_call(layer, q, kv, ctx, gates, ohy, x, mod, prep, *, tq, ur, per_batch_mod, bb=1):
    nb, n, _ = x.shape
    has_ctx = ctx is not None
    assert bb == 1 or not (has_ctx or per_batch_mod)
    lam_init = 0.8 - 0.6 * math.exp(-0.3 * layer)

    def lspec(shape):
        nd = len(shape)
        return pl.BlockSpec((1,) + tuple(shape[1:]), lambda b, i: (layer,) + (0,) * (nd - 1))

    in_specs = [
        pl.BlockSpec((bb, N_QBLK, tq, MXU_W), lambda b, i: (b, 0, i, 0)),
        pl.BlockSpec((bb, N_KVBLK, n, MXU_W), lambda b, i: (b, 0, 0, 0)),
    ]
    args = [q, kv]
    if has_ctx:
        p = ctx.shape[3]
        in_specs.append(pl.BlockSpec((None, 1, N_KVBLK, p, MXU_W), lambda b, i: (layer, b, 0, 0, 0)))
        args.append(ctx)
    in_specs += [
        pl.BlockSpec((bb, tq, 4 * GROUP_W), lambda b, i: (b, i, 0)),
        pl.BlockSpec((bb, tq, HY_W), lambda b, i: (b, i, 0)),
        pl.BlockSpec((bb, tq, D_MODEL), lambda b, i: (b, i, 0)),
        _mod_spec(layer, per_batch_mod),
    ]
    args += [gates, ohy, x, mod]
    for name in ("norm_post", "w_out", "diff_subln", "diff_lam"):
        in_specs.append(lspec(prep[name].shape))
        args.append(prep[name])

    return pl.pallas_call(
        functools.partial(_attn_kernel, has_ctx=has_ctx, lam_init=lam_init, tq=tq, ur=ur),
        grid=(nb // bb, n // tq),
        in_specs=in_specs,
        out_specs=pl.BlockSpec((bb, tq, D_MODEL), lambda b, i: (b, i, 0)),
        out_shape=jax.ShapeDtypeStruct((nb, n, D_MODEL), F32),
        compiler_params=pltpu.CompilerParams(
            dimension_semantics=("parallel", "parallel"), vmem_limit_bytes=VMEM_LIMIT),
        name="attn_lat" if has_ctx else "attn_ctx",
    )(*args)


def _rope_tables(n):
    tok = jnp.arange(n, dtype=jnp.int32)
    row = (tok // GRID_W).astype(F32)
    col = (tok % GRID_W).astype(F32)
    lane = jnp.arange(LANES, dtype=jnp.int32)
    tabs = []
    for d in (64, 32):
        m = d // 2
        i = lane % d
        ii = i % m
        f = ii % (m // 2)
        inv = ROPE_BASE ** (-(2 * f).astype(F32) / m)
        pos = jnp.where((i // m)[None, :] == 0, row[:, None], col[:, None])
        ang = pos * inv[None, :]
        sign = jnp.where(ii < m // 2, -1.0, 1.0).astype(F32)
        tabs += [jnp.cos(ang), jnp.sin(ang) * sign[None, :]]
    return tabs


def _dft_tables(n):
    k = jnp.arange(n, dtype=jnp.int32)
    unit = math.pi / (4 * n)
    kk = 2 * k[:, None] + 1
    ang_a = ((kk * (2 * LANES * jnp.arange(n // LANES, dtype=jnp.int32))[None, :]) % (8 * n)).astype(F32) * unit
    ang_b = ((kk * (2 * jnp.arange(LANES, dtype=jnp.int32) + 1)[None, :]) % (8 * n)).astype(F32) * unit
    ca, sa = jnp.cos(ang_a)[:, :, None], jnp.sin(ang_a)[:, :, None]
    cb, sb = jnp.cos(ang_b)[:, None, :], jnp.sin(ang_b)[:, None, :]
    cs = (ca * cb - sa * sb).reshape(n, n).astype(BF16)
    ss = (sa * cb + ca * sb).reshape(n, n).astype(BF16)
    th = (2 * k + 1).astype(F32) * (math.pi / (4 * n))
    ch = jnp.cos(th)[:, None]
    sh = jnp.sin(th)[:, None]
    t = jnp.linspace(0.0, 1.0, n, dtype=F32)[:, None]
    w = 2.0 * math.pi * jnp.arange(n, dtype=F32)[:, None] / n
    f = jnp.linspace(1e-4, HY_BANDS - 1, HY_BANDS, dtype=F32)[None, :]
    feats = jnp.concatenate([t, jnp.cos(f * w), -jnp.sin(f * w), jnp.zeros((n, 7), F32), ch, sh], axis=-1)
    feats = jnp.pad(feats, ((0, 0), (0, LANES - feats.shape[1])))
    return cs, ss, feats


def _prep_weights(p):
    offs = [0]
    for s in SPLIT_SIZES:
        offs.append(offs[-1] + s)
    (o_cq, o_ckv, o_kr, o_mg, o_gq, o_gk, o_gv, o_gg, o_dq, o_dk, o_dv, o_dg, o_hy, o_hg) = offs[:-1]
    w = p["w_in"]

    def cols(o, n):
        return w[:, :, o:o + n]

    def perm_heads(a, axis):
        parts = jnp.split(a, 4, axis=axis)
        return jnp.concatenate([parts[0], parts[2], parts[1], parts[3]], axis=axis)

    kr = cols(o_kr, 32)
    w_in = jnp.concatenate([
        cols(o_cq, 256), cols(o_ckv, 128), kr, kr, kr, kr,
        perm_heads(cols(o_gq, 256), 2), cols(o_gk, 128), cols(o_gv, 128),
        cols(o_dq, 256), cols(o_dk, 256), cols(o_dv, 256), cols(o_hy, 768),
        cols(o_mg, 256), perm_heads(cols(o_gg, 256), 2), cols(o_dg, 256), cols(o_hg, 256),
    ], axis=2).astype(BF16)

    wq = p["mla_wq_b"].reshape(DEPTH, MLA_Q_RANK, MLA_HEADS, MLA_NOPE + MLA_ROPE)
    wqb = jnp.concatenate([wq[..., :MLA_NOPE].reshape(DEPTH, MLA_Q_RANK, 256),
                           wq[..., MLA_NOPE:].reshape(DEPTH, MLA_Q_RANK, 128)], axis=2).astype(BF16)
    wk = p["mla_wkv_b"].reshape(DEPTH, MLA_KV_RANK, MLA_HEADS, MLA_NOPE + MLA_V)
    wkvb = jnp.concatenate([wk[..., :MLA_NOPE].reshape(DEPTH, MLA_KV_RANK, 256),
                            wk[..., MLA_NOPE:].reshape(DEPTH, MLA_KV_RANK, 256)], axis=2).astype(BF16)

    wo = p["w_out"]
    w_out = jnp.concatenate([wo[:, 0:256], perm_heads(wo[:, 256:512], 1), wo[:, 512:]], axis=1).astype(BF16)

    def pad_to(a, shape):
        return jnp.pad(a, [(0, s - d) for d, s in zip(a.shape, shape)])

    return {
        "norm_pre": p["norm_pre"].reshape(DEPTH, 1, D_MODEL),
        "norm_post": p["norm_post"].reshape(DEPTH, 1, D_MODEL),
        "w_in": w_in, "wqb": wqb, "wkvb": wkvb, "w_out": w_out,
        "mla_q_norm": p["mla_q_norm"].reshape(DEPTH, 1, MLA_Q_RANK),
        "mla_kv_norm": p["mla_kv_norm"].reshape(DEPTH, 1, MLA_KV_RANK),
        "gqa_q_norm": jnp.tile(p["gqa_q_norm"], (1, 4)).reshape(DEPTH, 1, 256),
        "gqa_k_norm": jnp.tile(p["gqa_k_norm"], (1, 2)).reshape(DEPTH, 1, 128),
        "diff_subln": jnp.tile(p["diff_subln"], (1, 4)).reshape(DEPTH, 1, 256),
        "diff_lam": jnp.stack([p["diff_lq1"], p["diff_lk1"], p["diff_lq2"], p["diff_lk2"]], axis=1),
        "hy_conv_w": p["hy_conv_w"],
        "hy_conv_b": p["hy_conv_b"].reshape(DEPTH, 1, 3 * HY_W),
        "hy_fw1": pad_to(p["hy_fw1"], (DEPTH, LANES, LANES)),
        "hy_fb1": pad_to(p["hy_fb1"].reshape(DEPTH, 1, HY_FFN), (DEPTH, 1, LANES)),
        "hy_fw2": pad_to(p["hy_fw2"], (DEPTH, LANES, LANES)),
        "hy_fb2": pad_to(p["hy_fb2"].reshape(DEPTH, 1, HY_FFN), (DEPTH, 1, LANES)),
        "hy_fw3": pad_to(p["hy_fw3"], (DEPTH, LANES, 4 * HY_W)),
        "hy_freq": pad_to(p["hy_freq"], (DEPTH, 2, LANES)),
        "hy_decay": p["hy_decay"].reshape(DEPTH, 1, HY_W),
        "hy_bias": p["hy_bias"],
    }


def kernel(x_prompt, x_sample, cache_mla_ckv, cache_mla_krope, cache_gqa_k, cache_gqa_v, cache_diff_k, cache_diff_v, c, c_ctx, norm_pre, norm_post, ada_w, ada_b, w_in, w_out, mla_q_norm, mla_wq_b, mla_kv_norm, mla_wkv_b, gqa_q_norm, gqa_k_norm, diff_lq1, diff_lk1, diff_lq2, diff_lk2, diff_subln, hy_conv_w, hy_conv_b, hy_fw1, hy_fb1, hy_fw2, hy_fb2, hy_fw3, hy_freq, hy_decay, hy_bias):
    nb_p, n_p, _ = x_prompt.shape
    nb_s, n_s, _ = x_sample.shape
    past = cache_mla_ckv.shape[2]

    prep = _prep_weights(dict(
        norm_pre=norm_pre, norm_post=norm_post, w_in=w_in, w_out=w_out, mla_q_norm=mla_q_norm,
        mla_wq_b=mla_wq_b, mla_kv_norm=mla_kv_norm, mla_wkv_b=mla_wkv_b, gqa_q_norm=gqa_q_norm,
        gqa_k_norm=gqa_k_norm, diff_lq1=diff_lq1, diff_lk1=diff_lk1, diff_lq2=diff_lq2, diff_lk2=diff_lk2,
        diff_subln=diff_subln, hy_conv_w=hy_conv_w, hy_conv_b=hy_conv_b, hy_fw1=hy_fw1, hy_fb1=hy_fb1,
        hy_fw2=hy_fw2, hy_fb2=hy_fb2, hy_fw3=hy_fw3, hy_freq=hy_freq, hy_decay=hy_decay, hy_bias=hy_bias))
    tabs = _rope_tables(n_s)
    dft_p = _dft_tables(n_p)
    dft_s = _dft_tables(n_s)

    cc8 = jnp.concatenate([c_ctx[None, :], c, jnp.zeros((MOD_ROWS - 1 - nb_s, D_MODEL), F32)], axis=0)
    mod = _ada_call(cc8, ada_w, ada_b).reshape(DEPTH * MOD_ROWS, 1, 3 * D_MODEL)

    ctx = _ctx_call(
        cache_mla_ckv, jnp.tile(cache_mla_krope, (1, 1, 1, 4)),
        cache_gqa_k.reshape(nb_s, DEPTH, past, 128), cache_gqa_v.reshape(nb_s, DEPTH, past, 128),
        cache_diff_k.reshape(nb_s, DEPTH, past, 256), cache_diff_v.reshape(nb_s, DEPTH, past, 256),
        prep["wkvb"])

    y_p, y_s = x_prompt, x_sample
    st = None
    for i in range(DEPTH):
        outs = _inproj_call(i, y_p, mod, prep, None, rope=False, states=True, tm=n_p, per_batch_mod=False, bb=4,
                            prev_states=st)
        q, kv, hy, gates = outs[:4]
        st = outs[4:]
        ohy = _hyena_call(i, hy, gates, prep, dft_p, bt=4)
        y_p = _attn_call(i, q, kv, None, gates, ohy, y_p, mod, prep, tq=n_p, ur=n_p, per_batch_mod=False, bb=4)
        q, kv, hy, gates = _inproj_call(i, y_s, mod, prep, tabs, rope=True, states=False, tm=1024, per_batch_mod=True)
        ohy = _hyena_call(i, hy, gates, prep, dft_s, bt=1)
        y_s = _attn_call(i, q, kv, ctx, gates, ohy, y_s, mod, prep, tq=128, ur=128, per_batch_mod=True)

    new_ckv, new_kr = st[0], st[1]
    new_gk = st[2].reshape(nb_p, DEPTH, n_p, GQA_KV_HEADS, GQA_HD)
    new_gv = st[3].reshape(nb_p, DEPTH, n_p, GQA_KV_HEADS, GQA_HD)
    new_dk = st[4].reshape(nb_p, DEPTH, n_p, DIFF_HEADS, 2 * DIFF_HD)
    new_dv = st[5].reshape(nb_p, DEPTH, n_p, DIFF_HEADS, DIFF_VD)
    return (y_p, y_s, new_ckv, new_kr, new_gk, new_gv, new_dk, new_dv)
```

```python
import functools
import math

import jax
import jax.numpy as jnp
from jax import lax
from jax.experimental import pallas as pl
from jax.experimental.pallas import tpu as pltpu

F32 = jnp.float32
BF16 = jnp.bfloat16

D_MODEL = 1024
DEPTH = 2
GRID_W = 64
ROPE_BASE = 10000.0
EPS = 1e-6
GROUP_W = 256
MLA_HEADS = 4
MLA_NOPE = 64
MLA_ROPE = 32
MLA_V = 64
MLA_Q_RANK = 256
MLA_KV_RANK = 128
GQA_HEADS = 4
GQA_KV_HEADS = 2
GQA_HD = 64
DIFF_HEADS = 4
DIFF_VD = 64
DIFF_HD = 32
HY_W = 256
HY_EMB = 33
HY_BANDS = 16
HY_FFN = 64
SPLIT_SIZES = (256, 128, 32, 256, 256, 128, 128, 256, 256, 256, 256, 256, 768, 256)
P_IN = sum(SPLIT_SIZES)

LANES = 128
MOD_ROWS = 8
MXU_W = 256
P_PAD = 14 * MXU_W
N_QBLK = 5
N_KVBLK = 6
LOG2E = math.log2(math.e)
INPROJ_CHUNK = 256
HY_CHUNK = 512
HY_HALO = 16
ATT_SCORE_LEAD = 2
ATT_PV_LAG = 2
NT_DIMS = (((1,), (1,)), ((), ()))
VMEM_LIMIT = 56 * 1024 * 1024


def _dot(a, b):
    return jnp.dot(a, b, preferred_element_type=F32)


def _dot_nt(a, b):
    return lax.dot_general(a, b, NT_DIMS, preferred_element_type=F32)


def _dot_split(a, b):
    a_hi = a.astype(BF16)
    b_hi = b.astype(BF16)
    a_lo = (a - a_hi.astype(F32)).astype(BF16)
    b_lo = (b - b_hi.astype(F32)).astype(BF16)
    return _dot(a_hi, b_hi) + _dot(a_hi, b_lo) + _dot(a_lo, b_hi)


def _rms_full(v, g):
    ms = jnp.mean(v * v, axis=-1, keepdims=True)
    return v * lax.rsqrt(ms + EPS) * g


def _rms_heads64(v, g):
    outs = []
    lane = lax.broadcasted_iota(jnp.int32, (v.shape[0], LANES), 1)
    lo = lane < 64
    for b in range(v.shape[1] // LANES):
        vb = v[:, b * LANES:(b + 1) * LANES]
        v2 = vb * vb
        s_lo = jnp.sum(jnp.where(lo, v2, 0.0), axis=-1, keepdims=True)
        s_hi = jnp.sum(jnp.where(lo, 0.0, v2), axis=-1, keepdims=True)
        r = jnp.where(lo, lax.rsqrt(s_lo * (1.0 / 64) + EPS), lax.rsqrt(s_hi * (1.0 / 64) + EPS))
        outs.append(vb * r)
    out = outs[0] if len(outs) == 1 else jnp.concatenate(outs, axis=1)
    return out * g


def _rope(v, cos_t, sin_t, half):
    lane = lax.broadcasted_iota(jnp.int32, (v.shape[0], LANES), 1)
    first = (lane % (2 * half)) < half
    outs = []
    for b in range(v.shape[1] // LANES):
        vb = v[:, b * LANES:(b + 1) * LANES]
        up = pltpu.roll(vb, LANES - half, axis=1)
        dn = pltpu.roll(vb, half, axis=1)
        outs.append(vb * cos_t + jnp.where(first, up, dn) * sin_t)
    return outs[0] if len(outs) == 1 else jnp.concatenate(outs, axis=1)


def _ada_kernel(cc_ref, w_ref, b_ref, o_ref):
    a = cc_ref[...]
    a = a * jax.nn.sigmoid(a)
    o_ref[0] = _dot(a.astype(BF16), w_ref[0].astype(BF16)) + b_ref[0]


def _ada_call(cc8, ada_w, ada_b):
    tn = 768
    return pl.pallas_call(
        _ada_kernel,
        grid=(DEPTH, 3 * D_MODEL // tn),
        in_specs=[
            pl.BlockSpec((MOD_ROWS, D_MODEL), lambda l, j: (0, 0)),
            pl.BlockSpec((1, D_MODEL, tn), lambda l, j: (l, 0, j)),
            pl.BlockSpec((1, 1, tn), lambda l, j: (l, 0, j)),
        ],
        out_specs=pl.BlockSpec((1, MOD_ROWS, tn), lambda l, j: (l, 0, j)),
        out_shape=jax.ShapeDtypeStruct((DEPTH, MOD_ROWS, 3 * D_MODEL), F32),
        compiler_params=pltpu.CompilerParams(dimension_semantics=("parallel", "parallel")),
        name="ada_mod",
    )(cc8, ada_w, ada_b.reshape(DEPTH, 1, 3 * D_MODEL))


def _ctx_kernel(ckv_ref, kr4_ref, gk_ref, gv_ref, dk_ref, dv_ref, wkvb_ref, o_ref):
    kvp = _dot(ckv_ref[0, 0].astype(BF16), wkvb_ref[0])
    kr4 = kr4_ref[0, 0].astype(BF16)
    o_ref[0, 0, 0] = jnp.concatenate([kvp[:, 0:128].astype(BF16), kr4], axis=1)
    o_ref[0, 0, 1] = jnp.concatenate([kvp[:, 128:256].astype(BF16), kr4], axis=1)
    o_ref[0, 0, 2] = kvp[:, 256:512].astype(BF16)
    gv = gv_ref[0, 0].astype(BF16)
    o_ref[0, 0, 3] = jnp.concatenate([gk_ref[0, 0].astype(BF16), gv], axis=1)
    o_ref[0, 0, 4] = dk_ref[0, 0].astype(BF16)
    o_ref[0, 0, 5] = dv_ref[0, 0].astype(BF16)


def _ctx_call(ckv, kr4, gk, gv, dk, dv, wkvb):
    nb, _, p, _ = ckv.shape

    def spec(w):
        return pl.BlockSpec((1, 1, p, w), lambda l, b: (b, l, 0, 0))

    return pl.pallas_call(
        _ctx_kernel,
        grid=(DEPTH, nb),
        in_specs=[spec(128), spec(128), spec(128), spec(128), spec(256), spec(256),
                  pl.BlockSpec((1, MLA_KV_RANK, 512), lambda l, b: (l, 0, 0))],
        out_specs=pl.BlockSpec((1, 1, N_KVBLK, p, MXU_W), lambda l, b: (l, b, 0, 0, 0)),
        out_shape=jax.ShapeDtypeStruct((DEPTH, nb, N_KVBLK, p, MXU_W), BF16),
        compiler_params=pltpu.CompilerParams(dimension_semantics=("parallel", "parallel")),
        name="ctx_kv",
    )(ckv, kr4, gk, gv, dk, dv, wkvb)


def _mod_spec(layer, per_batch):
    if per_batch:
        return pl.BlockSpec((1, 1, 3 * D_MODEL), lambda b, i: (layer * MOD_ROWS + 1 + b, 0, 0))
    return pl.BlockSpec((1, 1, 3 * D_MODEL), lambda b, i: (layer * MOD_ROWS, 0, 0))


def _inproj_kernel(*refs, rope, states, n_alias):
    it = iter(refs)
    x_ref, mod_ref, gpre_ref, w_ref, wqb_ref, wkvb_ref = (next(it) for _ in range(6))
    gq_mla_ref, gkv_mla_ref, gqn_ref, gkn_ref = (next(it) for _ in range(4))
    if rope:
        cos64_ref, sin64_ref, cos32_ref, sin32_ref = (next(it) for _ in range(4))
    for _ in range(n_alias):
        next(it)
    q_ref, kv_ref, hy_ref, gate_ref = (next(it) for _ in range(4))
    if states:
        s_ckv_ref, s_kr_ref, s_gk_ref, s_gv_ref, s_dk_ref, s_dv_ref = (next(it) for _ in range(6))
        if n_alias == 0:
            for s_ref in (s_ckv_ref, s_kr_ref, s_gk_ref, s_gv_ref, s_dk_ref, s_dv_ref):
                s_ref[:, 1:] = jnp.zeros((s_ref.shape[0], s_ref.shape[1] - 1) + s_ref.shape[2:], F32)
    h_scrs = list(it)

    mod = mod_ref[0]
    shift = mod[:, 0:D_MODEL]
    scale = mod[:, D_MODEL:2 * D_MODEL]
    bb, tm = x_ref.shape[0], x_ref.shape[1]
    chunks = [(bi, slice(r0, r0 + INPROJ_CHUNK)) for bi in range(bb) for r0 in range(0, tm, INPROJ_CHUNK)]

    for (bi, rs), h_scr in zip(chunks, h_scrs):
        x = x_ref[bi, rs, :]
        ms = jnp.mean(x * x, axis=-1, keepdims=True)
        h = (x * lax.rsqrt(ms + EPS) * gpre_ref[0]) * (1.0 + scale) + shift
        h_scr[...] = h.astype(BF16)

    def proj(g):
        w = w_ref[0, :, g * MXU_W:(g + 1) * MXU_W]
        parts = [_dot(h_scr[...], w) for h_scr in h_scrs]
        return parts[0] if len(parts) == 1 else jnp.concatenate(parts, axis=0)

    def put(ref, lead, v, cols=slice(None)):
        for c, (bi, rs) in enumerate(chunks):
            ref[(bi,) + lead + (rs, cols)] = v[c * INPROJ_CHUNK:(c + 1) * INPROJ_CHUNK]

    def table(ref):
        parts = [ref[rs, :] for _, rs in chunks]
        return parts[0] if len(parts) == 1 else jnp.concatenate(parts, axis=0)

    def rope32(v):
        return _rope(v, table(cos32_ref), table(sin32_ref), 8) if rope else v

    def rope64(v):
        return _rope(v, table(cos64_ref), table(sin64_ref), 16) if rope else v

    cq = _rms_full(proj(0), gq_mla_ref[0])
    q3 = _dot(cq.astype(BF16), wqb_ref[0]) * (LOG2E * (MLA_NOPE + MLA_ROPE) ** -0.5)
    qr = rope32(q3[:, 256:384]).astype(BF16)
    put(q_ref, (0,), jnp.concatenate([q3[:, 0:128].astype(BF16), qr], axis=1))
    put(q_ref, (1,), jnp.concatenate([q3[:, 128:256].astype(BF16), qr], axis=1))

    z1 = proj(1)
    ckv = _rms_full(z1[:, 0:128], gkv_mla_ref[0])
    kr4 = z1[:, 128:256]
    if states:
        put(s_ckv_ref, (0,), ckv)
        put(s_kr_ref, (0,), kr4[:, 0:MLA_ROPE])
    kvp = _dot(ckv.astype(BF16), wkvb_ref[0])
    kr4 = rope32(kr4).astype(BF16)
    put(kv_ref, (0,), jnp.concatenate([kvp[:, 0:128].astype(BF16), kr4], axis=1))
    put(kv_ref, (1,), jnp.concatenate([kvp[:, 128:256].astype(BF16), kr4], axis=1))
    put(kv_ref, (2,), kvp[:, 256:512].astype(BF16))

    gq = (rope64(_rms_heads64(proj(2), gqn_ref[0])) * (LOG2E * GQA_HD ** -0.5)).astype(BF16)
    zq = jnp.zeros((gq.shape[0], LANES), BF16)
    put(q_ref, (2,), jnp.concatenate([gq[:, 0:128], zq], axis=1))
    put(q_ref, (3,), jnp.concatenate([gq[:, 128:256], zq], axis=1))
    z3 = proj(3)
    gk = _rms_heads64(z3[:, 0:128], gkn_ref[0])
    gv = z3[:, 128:256]
    if states:
        put(s_gk_ref, (0,), gk)
        put(s_gv_ref, (0,), gv)
    put(kv_ref, (3,), jnp.concatenate([rope64(gk).astype(BF16), gv.astype(BF16)], axis=1))

    put(q_ref, (4,), (rope32(proj(4)) * (LOG2E * DIFF_HD ** -0.5)).astype(BF16))
    dk = proj(5)
    dv = proj(6)
    if states:
        put(s_dk_ref, (0,), dk)
        put(s_dv_ref, (0,), dv)
    put(kv_ref, (4,), rope32(dk).astype(BF16))
    put(kv_ref, (5,), dv.astype(BF16))

    for j in range(3):
        put(hy_ref, (), proj(7 + j).astype(BF16), slice(j * MXU_W, (j + 1) * MXU_W))
    for j in range(4):
        g = proj(10 + j)
        put(gate_ref, (), (g * jax.nn.sigmoid(g)).astype(BF16), slice(j * MXU_W, (j + 1) * MXU_W))


def _inproj_call(layer, x, mod, prep, tabs, *, rope, states, tm, per_batch_mod, bb=1, prev_states=None):
    nb, n, _ = x.shape
    assert bb == 1 or not per_batch_mod
    tiles = n // tm

    def lspec(shape):
        nd = len(shape)
        return pl.BlockSpec((1,) + tuple(shape[1:]), lambda b, i: (layer,) + (0,) * (nd - 1))

    in_specs = [
        pl.BlockSpec((bb, tm, D_MODEL), lambda b, i: (b, i, 0)),
        _mod_spec(layer, per_batch_mod),
    ]
    args = [x, mod]
    for name in ("norm_pre", "w_in", "wqb", "wkvb", "mla_q_norm", "mla_kv_norm", "gqa_q_norm", "gqa_k_norm"):
        a = prep[name]
        in_specs.append(lspec(a.shape))
        args.append(a)
    if rope:
        for t in tabs:
            in_specs.append(pl.BlockSpec((tm, LANES), lambda b, i: (i, 0)))
            args.append(t)

    out_shape = [
        jax.ShapeDtypeStruct((nb, N_QBLK, n, MXU_W), BF16),
        jax.ShapeDtypeStruct((nb, N_KVBLK, n, MXU_W), BF16),
        jax.ShapeDtypeStruct((nb, n, 3 * HY_W), BF16),
        jax.ShapeDtypeStruct((nb, n, 4 * GROUP_W), BF16),
    ]
    out_specs = [
        pl.BlockSpec((bb, N_QBLK, tm, MXU_W), lambda b, i: (b, 0, i, 0)),
        pl.BlockSpec((bb, N_KVBLK, tm, MXU_W), lambda b, i: (b, 0, i, 0)),
        pl.BlockSpec((bb, tm, 3 * HY_W), lambda b, i: (b, i, 0)),
        pl.BlockSpec((bb, tm, 4 * GROUP_W), lambda b, i: (b, i, 0)),
    ]
    aliases = {}
    if states:
        for w in (MLA_KV_RANK, MLA_ROPE, 128, 128, 256, 256):
            out_shape.append(jax.ShapeDtypeStruct((nb, DEPTH, n, w), F32))
            if prev_states is None:
                out_specs.append(pl.BlockSpec((bb, DEPTH, tm, w), lambda b, i: (b, 0, i, 0)))
            else:
                out_specs.append(pl.BlockSpec((bb, 1, tm, w), lambda b, i: (b, layer, i, 0)))
        if prev_states is not None:
            for j, a in enumerate(prev_states):
                aliases[len(args)] = 4 + j
                in_specs.append(pl.BlockSpec(memory_space=pl.ANY))
                args.append(a)

    return pl.pallas_call(
        functools.partial(_inproj_kernel, rope=rope, states=states, n_alias=len(aliases)),
        grid=(nb // bb, tiles),
        in_specs=in_specs,
        out_specs=out_specs,
        out_shape=out_shape,
        scratch_shapes=[pltpu.VMEM((INPROJ_CHUNK, D_MODEL), BF16)] * (bb * tm // INPROJ_CHUNK),
        input_output_aliases=aliases,
        compiler_params=pltpu.CompilerParams(
            dimension_semantics=("parallel", "parallel"), vmem_limit_bytes=VMEM_LIMIT),
        name="inproj_rope" if rope else "inproj_ctx",
    )(*args)


def _chunk_loop(nchunk, body):
    if nchunk == 1:
        body(0)
    else:
        def wrapped(k, carry):
            body(k)
            return carry
        lax.fori_loop(0, nchunk, wrapped, 0)


def _hyena_kernel(hy_ref, gate_ref, cw_ref, cb_ref, feats_ref, fw1_ref, fb1_ref, fw2_ref, fb2_ref,
                  fw3_ref, freq_ref, decay_ref, bias_ref, cs_ref, ss_ref,
                  o_ref, fr_scr, fi_scr, work, hy_pad, *, n, bt):
    rc = min(n, HY_CHUNK)
    nchunk = n // rc
    rowc = lax.broadcasted_iota(jnp.int32, (rc, HY_W), 0)

    def rows(k):
        if nchunk == 1:
            return slice(0, rc)
        return pl.ds(pl.multiple_of(k * rc, rc), rc)

    @pl.when(pl.program_id(0) == 0)
    def _filters():
        freq = freq_ref[0]
        bias = bias_ref[0]

        def taps(k):
            ft = feats_ref[rows(k), :]
            t = ft[:, 0:1]
            h1 = jnp.sin(freq[0:1, :] * (_dot_split(ft, fw1_ref[0]) + fb1_ref[0]))
            h2 = jnp.sin(freq[1:2, :] * (_dot_split(h1, fw2_ref[0]) + fb2_ref[0]))
            h3 = _dot_split(h2, fw3_ref[0])
            dec = jnp.exp(-t * jnp.abs(decay_ref[0]))
            lag0 = (rowc + k * rc) == 0
            for o in range(2):
                hf = h3[:, o * 512:o * 512 + 256] * dec
                hb = h3[:, o * 512 + 256:o * 512 + 512] * dec
                hf = hf + jnp.where(lag0, bias[o:o + 1, :], 0.0)
                hb = jnp.where(lag0, 0.0, hb)
                work[rows(k), o * 512:o * 512 + 256] = hf.astype(BF16)
                work[rows(k), o * 512 + 256:o * 512 + 512] = hb.astype(BF16)

        _chunk_loop(nchunk, taps)

        def spectra(k):
            ft = feats_ref[rows(k), :]
            c = ft[:, HY_EMB + 7:HY_EMB + 8]
            s = ft[:, HY_EMB + 8:HY_EMB + 9]
            p = _dot(cs_ref[rows(k), :], work[...])
            q = _dot(ss_ref[rows(k), :], work[...])
            for o in range(2):
                pf, pb = p[:, o * 512:o * 512 + 256], p[:, o * 512 + 256:o * 512 + 512]
                qf, qb = q[:, o * 512:o * 512 + 256], q[:, o * 512 + 256:o * 512 + 512]
                fr_scr[o, rows(k), :] = (c * (pf + pb) + s * (qf + qb)) * (1.0 / n)
                fi_scr[o, rows(k), :] = (s * (pf - pb) - c * (qf - qb)) * (1.0 / n)

        _chunk_loop(nchunk, spectra)

    cw = cw_ref[0]
    cb = cb_ref[0]
    hy_pad[0:HY_HALO, :] = jnp.zeros((HY_HALO, 3 * HY_W), BF16)
    hy_pad[HY_HALO + n:2 * HY_HALO + n, :] = jnp.zeros((HY_HALO, 3 * HY_W), BF16)

    def short_conv(k, j):
        sl = slice(j * HY_W, (j + 1) * HY_W)
        if nchunk == 1:
            ext = hy_pad[:, sl].astype(F32)
        else:
            ext = hy_pad[pl.ds(pl.multiple_of(k * rc, rc), rc + 2 * HY_HALO), sl].astype(F32)
        um = pltpu.roll(ext, 1, axis=0)[HY_HALO:HY_HALO + rc]
        up = pltpu.roll(ext, rc + 2 * HY_HALO - 1, axis=0)[HY_HALO:HY_HALO + rc]
        u = ext[HY_HALO:HY_HALO + rc]
        return um * cw[0:1, sl] + u * cw[1:2, sl] + up * cw[2:3, sl] + cb[:, sl]

    zc, yrc, yic = slice(0, 256), slice(256, 512), slice(512, 768)

    def one(bi):
        hy_pad[HY_HALO:HY_HALO + n, :] = hy_ref[bi]

        def conv_v(k):
            work[rows(k), zc] = short_conv(k, 2).astype(BF16)

        _chunk_loop(nchunk, conv_v)
        for o in range(2):
            def forward(k, o=o):
                a = _dot(cs_ref[rows(k), :], work[:, zc])
                b = _dot(ss_ref[rows(k), :], work[:, zc])
                fr = fr_scr[o, rows(k), :]
                fi = fi_scr[o, rows(k), :]
                work[rows(k), yrc] = (a * fr + b * fi).astype(BF16)
                work[rows(k), yic] = (a * fi - b * fr).astype(BF16)

            def inverse(k, o=o):
                y = _dot(cs_ref[rows(k), :], work[:, yrc]) - _dot(ss_ref[rows(k), :], work[:, yic])
                z = short_conv(k, o) * y
                if o == 0:
                    work[rows(k), zc] = z.astype(BF16)
                else:
                    o_ref[bi, rows(k), :] = (z * gate_ref[bi, rows(k), :].astype(F32)).astype(BF16)

            _chunk_loop(nchunk, forward)
            _chunk_loop(nchunk, inverse)

    if bt == 1:
        one(0)
    else:
        def body(bi, carry):
            one(bi)
            return carry
        lax.fori_loop(0, bt, body, 0)


def _hyena_call(layer, hy, gates, prep, dft, *, bt):
    nb, n, _ = hy.shape
    cs, ss, feats = dft

    def lspec(shape):
        nd = len(shape)
        return pl.BlockSpec((1,) + tuple(shape[1:]), lambda b: (layer,) + (0,) * (nd - 1))

    def cspec(shape):
        nd = len(shape)
        return pl.BlockSpec(tuple(shape), lambda b: (0,) * nd, pipeline_mode=pl.Buffered(1))

    big = n >= 1024
    in_specs = [
        pl.BlockSpec((bt, n, 3 * HY_W), lambda b: (b, 0, 0)),
        pl.BlockSpec((bt, n, GROUP_W), lambda b: (b, 0, 3)),
    ]
    args = [hy, gates]
    for name in ("hy_conv_w", "hy_conv_b"):
        in_specs.append(lspec(prep[name].shape))
        args.append(prep[name])
    in_specs.append(cspec(feats.shape))
    args.append(feats)
    for name in ("hy_fw1", "hy_fb1", "hy_fw2", "hy_fb2", "hy_fw3", "hy_freq", "hy_decay", "hy_bias"):
        in_specs.append(lspec(prep[name].shape))
        args.append(prep[name])
    in_specs += [cspec(cs.shape), cspec(ss.shape)]
    args += [cs, ss]

    return pl.pallas_call(
        functools.partial(_hyena_kernel, n=n, bt=bt),
        grid=(nb // bt,),
        in_specs=in_specs,
        out_specs=pl.BlockSpec((bt, n, HY_W), lambda b: (b, 0, 0)),
        out_shape=jax.ShapeDtypeStruct((nb, n, HY_W), BF16),
        scratch_shapes=[pltpu.VMEM((2, n, HY_W), F32), pltpu.VMEM((2, n, HY_W), F32),
                        pltpu.VMEM((n, 4 * HY_W), BF16),
                        pltpu.VMEM((n + 2 * HY_HALO, 3 * HY_W), BF16)],
        compiler_params=pltpu.CompilerParams(
            dimension_semantics=("arbitrary",), vmem_limit_bytes=VMEM_LIMIT),
        name="hyena_long" if big else "hyena_short",
    )(*args)


def _attn_kernel(*refs, has_ctx, lam_init, tq, ur, early_proj):
    it = iter(refs)
    q_ref, kv_ref = next(it), next(it)
    ctx_ref = next(it) if has_ctx else None
    gate_ref, ohy_ref, x_ref, mod_ref, gpost_ref, wout_ref, subln_ref, lamp_ref = (next(it) for _ in range(8))
    y_ref = next(it)

    lane = lax.broadcasted_iota(jnp.int32, (ur, MXU_W), 1)

    def band(lo, width):
        return (lane >= lo) & (lane < lo + width)

    def scores(bi, qs, kblk):
        s_l = _dot_nt(qs, kv_ref[bi, kblk])
        s_c = _dot_nt(qs, ctx_ref[bi, kblk]) if has_ctx else None
        return s_l, s_c

    def softmax_rows(s_l, s_c, narrow=False):
        def weights(s, m):
            p = jnp.exp2(s - m)
            if narrow:
                p = p.astype(BF16)
            return p, jnp.sum(p.astype(F32), axis=-1, keepdims=True)

        m = jnp.max(s_l, axis=-1, keepdims=True)
        if has_ctx:
            m = jnp.maximum(m, jnp.max(s_c, axis=-1, keepdims=True))
            p_c, l_c = weights(s_c, m)
        p_l, l = weights(s_l, m)
        if has_ctx:
            return p_l, p_c, l + l_c
        return p_l, None, l

    def pv(bi, p_l, p_c, vblk):
        o = _dot(p_l.astype(BF16), kv_ref[bi, vblk])
        if has_ctx:
            o = o + _dot(p_c.astype(BF16), ctx_ref[bi, vblk])
        return o

    def stack(parts):
        return jnp.concatenate(parts, axis=0)

    lamp = lamp_ref[0]
    lam = (jnp.exp(jnp.sum(lamp[0:1, :] * lamp[1:2, :], axis=-1, keepdims=True))
           - jnp.exp(jnp.sum(lamp[2:3, :] * lamp[3:4, :], axis=-1, keepdims=True)) + lam_init)
    lo_half = lax.broadcasted_iota(jnp.int32, (ur, LANES), 1) < GQA_HD

    qblocks = [(bi, slice(r0, r0 + ur)) for bi in range(q_ref.shape[0]) for r0 in range(0, tq, ur)]
    units = [(qb, kind, idx) for qb in qblocks
             for kind, idx in (("mla", 0), ("mla", 1), ("gqa", 0), ("gqa", 1),
                               ("diff", 0), ("diff", 1), ("diff", 2), ("diff", 3))]

    def unit_scores(qb, kind, idx):
        bi, rs = qb
        if kind == "mla":
            qf = q_ref[bi, idx, rs, :].astype(F32)
            masks = [band(64 * j, MLA_NOPE) | band(128 + MLA_ROPE * (2 * idx + j), MLA_ROPE) for j in range(2)]
            kblk = idx
        elif kind == "gqa":
            qf = q_ref[bi, 2 + idx, rs, :].astype(F32)
            masks = [band(GQA_HD * g, GQA_HD) for g in range(2)]
            kblk = 3
        else:
            qf = q_ref[bi, 4, rs, :].astype(F32)
            masks = [band(64 * idx + DIFF_HD * s, DIFF_HD) for s in range(2)]
            kblk = 4
        return scores(bi, stack([jnp.where(m, qf, 0.0) for m in masks]).astype(BF16), kblk)

    def unit_softmax(kind, s_l, s_c):
        if kind == "diff":
            p1_l, p1_c, l1 = softmax_rows(s_l[0:ur], s_c[0:ur] if has_ctx else None)
            p2_l, p2_c, l2 = softmax_rows(s_l[ur:2 * ur], s_c[ur:2 * ur] if has_ctx else None)
            w1 = 1.0 / l1
            w2 = lam / l2
            a_l = (p1_l * w1 - p2_l * w2).astype(BF16)
            a_c = (p1_c * w1 - p2_c * w2).astype(BF16) if has_ctx else None
            return a_l, a_c, None
        return softmax_rows(s_l, s_c, narrow=True)

    def unit_pv(bi, kind, idx, p_l, p_c, l):
        if kind == "diff":
            return jnp.where(band(DIFF_VD * idx, DIFF_VD), pv(bi, p_l, p_c, 5), 0.0)
        if kind == "mla":
            o = pv(bi, p_l, p_c, 2) / l
            return (jnp.where(band(128 * idx, MLA_V), o[0:ur], 0.0)
                    + jnp.where(band(128 * idx + MLA_V, MLA_V), o[ur:2 * ur], 0.0))
        o = (pv(bi, p_l, p_c, 3) / l)[:, 128:256]
        part = jnp.where(lo_half, o[0:ur], o[ur:2 * ur])
        zero = jnp.zeros_like(part)
        return jnp.concatenate([part, zero] if idx == 0 else [zero, part], axis=1)

    mixer_cols = {"mla": 0, "gqa": GROUP_W, "diff": 2 * GROUP_W}
    units_left = {(k, kind): n for k in range(len(qblocks)) for kind, n in (("mla", 2), ("gqa", 2), ("diff", 4))}
    gated, out_acc = {}, {}

    def project(k, v, lo):
        c = _dot(v, wout_ref[0, lo:lo + v.shape[1], :])
        out_acc[k] = c if k not in out_acc else out_acc[k] + c

    def mixer_done(k, kind, o):
        bi, rs = qblocks[k]
        lo = mixer_cols[kind]
        if kind == "diff":
            o2 = o * o
            r = jnp.zeros_like(o)
            for h in range(DIFF_HEADS):
                mh = band(DIFF_VD * h, DIFF_VD)
                sh = jnp.sum(jnp.where(mh, o2, 0.0), axis=-1, keepdims=True)
                r = jnp.where(mh, lax.rsqrt(sh * (1.0 / DIFF_VD) + EPS), r)
            o = (o * r * subln_ref[0]) * (1.0 - lam_init)
        piece = (o * gate_ref[bi, rs, lo:lo + GROUP_W].astype(F32)).astype(BF16)
        if early_proj:
            project(k, piece, lo)
        else:
            gated[k, kind] = piece
        if kind == "diff" and early_proj:
            project(k, ohy_ref[bi, rs, :], 3 * GROUP_W)
            finish(k, out_acc.pop(k))

    def finish(k, out):
        bi, rs = qblocks[k]
        ms = jnp.mean(out * out, axis=-1, keepdims=True)
        gate_mod = mod_ref[0][:, 2 * D_MODEL:3 * D_MODEL]
        y_ref[bi, rs, :] = x_ref[bi, rs, :] + gate_mod * (out * lax.rsqrt(ms + EPS) * gpost_ref[0])

    acc = {}
    sc_vals, sm_vals = {}, {}
    for t in range(len(units) + ATT_SCORE_LEAD + ATT_PV_LAG):
        if t < len(units):
            sc_vals[t] = unit_scores(*units[t])
        u = t - ATT_SCORE_LEAD
        if 0 <= u < len(units):
            sm_vals[u] = unit_softmax(units[u][1], *sc_vals.pop(u))
        u = t - ATT_SCORE_LEAD - ATT_PV_LAG
        if u >= 0:
            qb, kind, idx = units[u]
            c = unit_pv(qb[0], kind, idx, *sm_vals.pop(u))
            key = (qblocks.index(qb), kind)
            acc[key] = c if key not in acc else acc[key] + c
            units_left[key] -= 1
            if units_left[key] == 0:
                mixer_done(key[0], kind, acc.pop(key))

    if not early_proj:
        rows = [jnp.concatenate([gated[k, "mla"], gated[k, "gqa"], gated[k, "diff"], ohy_ref[bi, rs, :]], axis=1)
                for k, (bi, rs) in enumerate(qblocks)]
        out = _dot(rows[0] if len(rows) == 1 else jnp.concatenate(rows, axis=0), wout_ref[0])
        for k in range(len(qblocks)):
            finish(k, out[k * ur:(k + 1) * ur])


def _attn_call(layer, q, kv, ctx, gates, ohy, x, mod, prep, *, tq, ur, per_batch_mod, early_proj, bb=1):
    nb, n, _ = x.shape
    has_ctx = ctx is not None
    assert bb == 1 or not (has_ctx or per_batch_mod)
    lam_init = 0.8 - 0.6 * math.exp(-0.3 * layer)

    def lspec(shape):
        nd = len(shape)
        return pl.BlockSpec((1,) + tuple(shape[1:]), lambda b, i: (layer,) + (0,) * (nd - 1))

    in_specs = [
        pl.BlockSpec((bb, N_QBLK, tq, MXU_W), lambda b, i: (b, 0, i, 0)),
        pl.BlockSpec((bb, N_KVBLK, n, MXU_W), lambda b, i: (b, 0, 0, 0)),
    ]
    args = [q, kv]
    if has_ctx:
        p = ctx.shape[3]
        in_specs.append(pl.BlockSpec((None, 1, N_KVBLK, p, MXU_W), lambda b, i: (layer, b, 0, 0, 0)))
        args.append(ctx)
    in_specs += [
        pl.BlockSpec((bb, tq, 4 * GROUP_W), lambda b, i: (b, i, 0)),
        pl.BlockSpec((bb, tq, HY_W), lambda b, i: (b, i, 0)),
        pl.BlockSpec((bb, tq, D_MODEL), lambda b, i: (b, i, 0)),
        _mod_spec(layer, per_batch_mod),
    ]
    args += [gates, ohy, x, mod]
    for name in ("norm_post", "w_out", "diff_subln", "diff_lam"):
        in_specs.append(lspec(prep[name].shape))
        args.append(prep[name])

    return pl.pallas_call(
        functools.partial(_attn_kernel, has_ctx=has_ctx, lam_init=lam_init, tq=tq, ur=ur, early_proj=early_proj),
        grid=(nb // bb, n // tq),
        in_specs=in_specs,
        out_specs=pl.BlockSpec((bb, tq, D_MODEL), lambda b, i: (b, i, 0)),
        out_shape=jax.ShapeDtypeStruct((nb, n, D_MODEL), F32),
        compiler_params=pltpu.CompilerParams(
            dimension_semantics=("parallel", "parallel"), vmem_limit_bytes=VMEM_LIMIT),
        name="attn_lat" if has_ctx else "attn_ctx",
    )(*args)


def _rope_tables(n):
    tok = jnp.arange(n, dtype=jnp.int32)
    row = (tok // GRID_W).astype(F32)
    col = (tok % GRID_W).astype(F32)
    lane = jnp.arange(LANES, dtype=jnp.int32)
    tabs = []
    for d in (64, 32):
        m = d // 2
        i = lane % d
        ii = i % m
        f = ii % (m // 2)
        inv = ROPE_BASE ** (-(2 * f).astype(F32) / m)
        pos = jnp.where((i // m)[None, :] == 0, row[:, None], col[:, None])
        ang = pos * inv[None, :]
        sign = jnp.where(ii < m // 2, -1.0, 1.0).astype(F32)
        tabs += [jnp.cos(ang), jnp.sin(ang) * sign[None, :]]
    return tabs


def _dft_tables(n):
    k = jnp.arange(n, dtype=jnp.int32)
    unit = math.pi / (4 * n)
    kk = 2 * k[:, None] + 1
    ang_a = ((kk * (2 * LANES * jnp.arange(n // LANES, dtype=jnp.int32))[None, :]) % (8 * n)).astype(F32) * unit
    ang_b = ((kk * (2 * jnp.arange(LANES, dtype=jnp.int32) + 1)[None, :]) % (8 * n)).astype(F32) * unit
    ca, sa = jnp.cos(ang_a)[:, :, None], jnp.sin(ang_a)[:, :, None]
    cb, sb = jnp.cos(ang_b)[:, None, :], jnp.sin(ang_b)[:, None, :]
    cs = (ca * cb - sa * sb).reshape(n, n).astype(BF16)
    ss = (sa * cb + ca * sb).reshape(n, n).astype(BF16)
    th = (2 * k + 1).astype(F32) * (math.pi / (4 * n))
    ch = jnp.cos(th)[:, None]
    sh = jnp.sin(th)[:, None]
    t = jnp.linspace(0.0, 1.0, n, dtype=F32)[:, None]
    w = 2.0 * math.pi * jnp.arange(n, dtype=F32)[:, None] / n
    f = jnp.linspace(1e-4, HY_BANDS - 1, HY_BANDS, dtype=F32)[None, :]
    feats = jnp.concatenate([t, jnp.cos(f * w), -jnp.sin(f * w), jnp.zeros((n, 7), F32), ch, sh], axis=-1)
    feats = jnp.pad(feats, ((0, 0), (0, LANES - feats.shape[1])))
    return cs, ss, feats


def _prep_weights(p):
    offs = [0]
    for s in SPLIT_SIZES:
        offs.append(offs[-1] + s)
    (o_cq, o_ckv, o_kr, o_mg, o_gq, o_gk, o_gv, o_gg, o_dq, o_dk, o_dv, o_dg, o_hy, o_hg) = offs[:-1]
    w = p["w_in"]

    def cols(o, n):
        return w[:, :, o:o + n]

    def perm_heads(a, axis):
        parts = jnp.split(a, 4, axis=axis)
        return jnp.concatenate([parts[0], parts[2], parts[1], parts[3]], axis=axis)

    kr = cols(o_kr, 32)
    w_in = jnp.concatenate([
        cols(o_cq, 256), cols(o_ckv, 128), kr, kr, kr, kr,
        perm_heads(cols(o_gq, 256), 2), cols(o_gk, 128), cols(o_gv, 128),
        cols(o_dq, 256), cols(o_dk, 256), cols(o_dv, 256), cols(o_hy, 768),
        cols(o_mg, 256), perm_heads(cols(o_gg, 256), 2), cols(o_dg, 256), cols(o_hg, 256),
    ], axis=2).astype(BF16)

    wq = p["mla_wq_b"].reshape(DEPTH, MLA_Q_RANK, MLA_HEADS, MLA_NOPE + MLA_ROPE)
    wqb = jnp.concatenate([wq[..., :MLA_NOPE].reshape(DEPTH, MLA_Q_RANK, 256),
                           wq[..., MLA_NOPE:].reshape(DEPTH, MLA_Q_RANK, 128)], axis=2).astype(BF16)
    wk = p["mla_wkv_b"].reshape(DEPTH, MLA_KV_RANK, MLA_HEADS, MLA_NOPE + MLA_V)
    wkvb = jnp.concatenate([wk[..., :MLA_NOPE].reshape(DEPTH, MLA_KV_RANK, 256),
                            wk[..., MLA_NOPE:].reshape(DEPTH, MLA_KV_RANK, 256)], axis=2).astype(BF16)

    wo = p["w_out"]
    w_out = jnp.concatenate([wo[:, 0:256], perm_heads(wo[:, 256:512], 1), wo[:, 512:]], axis=1).astype(BF16)

    def pad_to(a, shape):
        return jnp.pad(a, [(0, s - d) for d, s in zip(a.shape, shape)])

    return {
        "norm_pre": p["norm_pre"].reshape(DEPTH, 1, D_MODEL),
        "norm_post": p["norm_post"].reshape(DEPTH, 1, D_MODEL),
        "w_in": w_in, "wqb": wqb, "wkvb": wkvb, "w_out": w_out,
        "mla_q_norm": p["mla_q_norm"].reshape(DEPTH, 1, MLA_Q_RANK),
        "mla_kv_norm": p["mla_kv_norm"].reshape(DEPTH, 1, MLA_KV_RANK),
        "gqa_q_norm": jnp.tile(p["gqa_q_norm"], (1, 4)).reshape(DEPTH, 1, 256),
        "gqa_k_norm": jnp.tile(p["gqa_k_norm"], (1, 2)).reshape(DEPTH, 1, 128),
        "diff_subln": jnp.tile(p["diff_subln"], (1, 4)).reshape(DEPTH, 1, 256),
        "diff_lam": jnp.stack([p["diff_lq1"], p["diff_lk1"], p["diff_lq2"], p["diff_lk2"]], axis=1),
        "hy_conv_w": p["hy_conv_w"],
        "hy_conv_b": p["hy_conv_b"].reshape(DEPTH, 1, 3 * HY_W),
        "hy_fw1": pad_to(p["hy_fw1"], (DEPTH, LANES, LANES)),
        "hy_fb1": pad_to(p["hy_fb1"].reshape(DEPTH, 1, HY_FFN), (DEPTH, 1, LANES)),
        "hy_fw2": pad_to(p["hy_fw2"], (DEPTH, LANES, LANES)),
        "hy_fb2": pad_to(p["hy_fb2"].reshape(DEPTH, 1, HY_FFN), (DEPTH, 1, LANES)),
        "hy_fw3": pad_to(p["hy_fw3"], (DEPTH, LANES, 4 * HY_W)),
        "hy_freq": pad_to(p["hy_freq"], (DEPTH, 2, LANES)),
        "hy_decay": p["hy_decay"].reshape(DEPTH, 1, HY_W),
        "hy_bias": p["hy_bias"],
    }


def kernel(x_prompt, x_sample, cache_mla_ckv, cache_mla_krope, cache_gqa_k, cache_gqa_v, cache_diff_k, cache_diff_v, c, c_ctx, norm_pre, norm_post, ada_w, ada_b, w_in, w_out, mla_q_norm, mla_wq_b, mla_kv_norm, mla_wkv_b, gqa_q_norm, gqa_k_norm, diff_lq1, diff_lk1, diff_lq2, diff_lk2, diff_subln, hy_conv_w, hy_conv_b, hy_fw1, hy_fb1, hy_fw2, hy_fb2, hy_fw3, hy_freq, hy_decay, hy_bias):
    nb_p, n_p, _ = x_prompt.shape
    nb_s, n_s, _ = x_sample.shape
    past = cache_mla_ckv.shape[2]

    prep = _prep_weights(dict(
        norm_pre=norm_pre, norm_post=norm_post, w_in=w_in, w_out=w_out, mla_q_norm=mla_q_norm,
        mla_wq_b=mla_wq_b, mla_kv_norm=mla_kv_norm, mla_wkv_b=mla_wkv_b, gqa_q_norm=gqa_q_norm,
        gqa_k_norm=gqa_k_norm, diff_lq1=diff_lq1, diff_lk1=diff_lk1, diff_lq2=diff_lq2, diff_lk2=diff_lk2,
        diff_subln=diff_subln, hy_conv_w=hy_conv_w, hy_conv_b=hy_conv_b, hy_fw1=hy_fw1, hy_fb1=hy_fb1,
        hy_fw2=hy_fw2, hy_fb2=hy_fb2, hy_fw3=hy_fw3, hy_freq=hy_freq, hy_decay=hy_decay, hy_bias=hy_bias))
    tabs = _rope_tables(n_s)
    dft_p = _dft_tables(n_p)
    dft_s = _dft_tables(n_s)

    cc8 = jnp.concatenate([c_ctx[None, :], c, jnp.zeros((MOD_ROWS - 1 - nb_s, D_MODEL), F32)], axis=0)
    mod = _ada_call(cc8, ada_w, ada_b).reshape(DEPTH * MOD_ROWS, 1, 3 * D_MODEL)

    ctx = _ctx_call(
        cache_mla_ckv, jnp.tile(cache_mla_krope, (1, 1, 1, 4)),
        cache_gqa_k.reshape(nb_s, DEPTH, past, 128), cache_gqa_v.reshape(nb_s, DEPTH, past, 128),
        cache_diff_k.reshape(nb_s, DEPTH, past, 256), cache_diff_v.reshape(nb_s, DEPTH, past, 256),
        prep["wkvb"])

    y_p, y_s = x_prompt, x_sample
    st = None
    for i in range(DEPTH):
        outs = _inproj_call(i, y_p, mod, prep, None, rope=False, states=True, tm=n_p, per_batch_mod=False, bb=4,
                            prev_states=st)
        q, kv, hy, gates = outs[:4]
        st = outs[4:]
        ohy = _hyena_call(i, hy, gates, prep, dft_p, bt=4)
        y_p = _attn_call(i, q, kv, None, gates, ohy, y_p, mod, prep, tq=n_p, ur=n_p, per_batch_mod=False,
                         early_proj=True, bb=4)
        q, kv, hy, gates = _inproj_call(i, y_s, mod, prep, tabs, rope=True, states=False, tm=1024, per_batch_mod=True)
        ohy = _hyena_call(i, hy, gates, prep, dft_s, bt=1)
        y_s = _attn_call(i, q, kv, ctx, gates, ohy, y_s, mod, prep, tq=256, ur=128, per_batch_mod=True,
                         early_proj=False)

    new_ckv, new_kr = st[0], st[1]
    new_gk = st[2].reshape(nb_p, DEPTH, n_p, GQA_KV_HEADS, GQA_HD)
    new_gv = st[3].reshape(nb_p, DEPTH, n_p, GQA_KV_HEADS, GQA_HD)
    new_dk = st[4].reshape(nb_p, DEPTH, n_p, DIFF_HEADS, 2 * DIFF_HD)
    new_dv = st[5].reshape(nb_p, DEPTH, n_p, DIFF_HEADS, DIFF_VD)
    return (y_p, y_s, new_ckv, new_kr, new_gk, new_gv, new_dk, new_dv)
```

```python
import functools
import math

import jax
import jax.numpy as jnp
from jax import lax
from jax.experimental import pallas as pl
from jax.experimental.pallas import tpu as pltpu

F32 = jnp.float32
BF16 = jnp.bfloat16

D_MODEL = 1024
DEPTH = 2
GRID_W = 64
ROPE_BASE = 10000.0
EPS = 1e-6
GROUP_W = 256
MLA_HEADS = 4
MLA_NOPE = 64
MLA_ROPE = 32
MLA_V = 64
MLA_Q_RANK = 256
MLA_KV_RANK = 128
GQA_HEADS = 4
GQA_KV_HEADS = 2
GQA_HD = 64
DIFF_HEADS = 4
DIFF_VD = 64
DIFF_HD = 32
HY_W = 256
HY_EMB = 33
HY_BANDS = 16
HY_FFN = 64
SPLIT_SIZES = (256, 128, 32, 256, 256, 128, 128, 256, 256, 256, 256, 256, 768, 256)
P_IN = sum(SPLIT_SIZES)

LANES = 128
MOD_ROWS = 8
MXU_W = 256
P_PAD = 14 * MXU_W
N_QBLK = 5
N_KVBLK = 6
LOG2E = math.log2(math.e)
INPROJ_CHUNK = 256
HY_CHUNK = 512
HY_HALO = 16
ATT_SCORE_LEAD = 2
ATT_PV_LAG = 2
NT_DIMS = (((1,), (1,)), ((), ()))
VMEM_LIMIT = 56 * 1024 * 1024


def _dot(a, b):
    return jnp.dot(a, b, preferred_element_type=F32)


def _dot_nt(a, b):
    return lax.dot_general(a, b, NT_DIMS, preferred_element_type=F32)


def _dot_split(a, b):
    a_hi = a.astype(BF16)
    b_hi = b.astype(BF16)
    a_lo = (a - a_hi.astype(F32)).astype(BF16)
    b_lo = (b - b_hi.astype(F32)).astype(BF16)
    return _dot(a_hi, b_hi) + _dot(a_hi, b_lo) + _dot(a_lo, b_hi)


def _rms_full(v, g):
    ms = jnp.mean(v * v, axis=-1, keepdims=True)
    return v * lax.rsqrt(ms + EPS) * g


def _rms_heads64(v, g):
    outs = []
    lane = lax.broadcasted_iota(jnp.int32, (v.shape[0], LANES), 1)
    lo = lane < 64
    for b in range(v.shape[1] // LANES):
        vb = v[:, b * LANES:(b + 1) * LANES]
        v2 = vb * vb
        s_lo = jnp.sum(jnp.where(lo, v2, 0.0), axis=-1, keepdims=True)
        s_hi = jnp.sum(jnp.where(lo, 0.0, v2), axis=-1, keepdims=True)
        r = jnp.where(lo, lax.rsqrt(s_lo * (1.0 / 64) + EPS), lax.rsqrt(s_hi * (1.0 / 64) + EPS))
        outs.append(vb * r)
    out = outs[0] if len(outs) == 1 else jnp.concatenate(outs, axis=1)
    return out * g


def _rope(v, cos_t, sin_t, half):
    lane = lax.broadcasted_iota(jnp.int32, (v.shape[0], LANES), 1)
    first = (lane % (2 * half)) < half
    outs = []
    for b in range(v.shape[1] // LANES):
        vb = v[:, b * LANES:(b + 1) * LANES]
        up = pltpu.roll(vb, LANES - half, axis=1)
        dn = pltpu.roll(vb, half, axis=1)
        outs.append(vb * cos_t + jnp.where(first, up, dn) * sin_t)
    return outs[0] if len(outs) == 1 else jnp.concatenate(outs, axis=1)


def _ada_kernel(cc_ref, w_ref, b_ref, o_ref):
    a = cc_ref[...]
    a = a * jax.nn.sigmoid(a)
    o_ref[0] = _dot(a.astype(BF16), w_ref[0].astype(BF16)) + b_ref[0]


def _ada_call(cc8, ada_w, ada_b):
    tn = 768
    return pl.pallas_call(
        _ada_kernel,
        grid=(DEPTH, 3 * D_MODEL // tn),
        in_specs=[
            pl.BlockSpec((MOD_ROWS, D_MODEL), lambda l, j: (0, 0)),
            pl.BlockSpec((1, D_MODEL, tn), lambda l, j: (l, 0, j)),
            pl.BlockSpec((1, 1, tn), lambda l, j: (l, 0, j)),
        ],
        out_specs=pl.BlockSpec((1, MOD_ROWS, tn), lambda l, j: (l, 0, j)),
        out_shape=jax.ShapeDtypeStruct((DEPTH, MOD_ROWS, 3 * D_MODEL), F32),
        compiler_params=pltpu.CompilerParams(dimension_semantics=("parallel", "parallel")),
        name="ada_mod",
    )(cc8, ada_w, ada_b.reshape(DEPTH, 1, 3 * D_MODEL))


def _ctx_kernel(ckv_ref, kr4_ref, gk_ref, gv_ref, dk_ref, dv_ref, wkvb_ref, o_ref):
    kvp = _dot(ckv_ref[0, 0].astype(BF16), wkvb_ref[0])
    kr4 = kr4_ref[0, 0].astype(BF16)
    o_ref[0, 0, 0] = jnp.concatenate([kvp[:, 0:128].astype(BF16), kr4], axis=1)
    o_ref[0, 0, 1] = jnp.concatenate([kvp[:, 128:256].astype(BF16), kr4], axis=1)
    o_ref[0, 0, 2] = kvp[:, 256:512].astype(BF16)
    gv = gv_ref[0, 0].astype(BF16)
    o_ref[0, 0, 3] = jnp.concatenate([gk_ref[0, 0].astype(BF16), gv], axis=1)
    o_ref[0, 0, 4] = dk_ref[0, 0].astype(BF16)
    o_ref[0, 0, 5] = dv_ref[0, 0].astype(BF16)


def _ctx_call(ckv, kr4, gk, gv, dk, dv, wkvb):
    nb, _, p, _ = ckv.shape

    def spec(w):
        return pl.BlockSpec((1, 1, p, w), lambda l, b: (b, l, 0, 0))

    return pl.pallas_call(
        _ctx_kernel,
        grid=(DEPTH, nb),
        in_specs=[spec(128), spec(128), spec(128), spec(128), spec(256), spec(256),
                  pl.BlockSpec((1, MLA_KV_RANK, 512), lambda l, b: (l, 0, 0))],
        out_specs=pl.BlockSpec((1, 1, N_KVBLK, p, MXU_W), lambda l, b: (l, b, 0, 0, 0)),
        out_shape=jax.ShapeDtypeStruct((DEPTH, nb, N_KVBLK, p, MXU_W), BF16),
        compiler_params=pltpu.CompilerParams(dimension_semantics=("parallel", "parallel")),
        name="ctx_kv",
    )(ckv, kr4, gk, gv, dk, dv, wkvb)


def _mod_spec(layer, per_batch):
    if per_batch:
        return pl.BlockSpec((1, 1, 3 * D_MODEL), lambda b, i: (layer * MOD_ROWS + 1 + b, 0, 0))
    return pl.BlockSpec((1, 1, 3 * D_MODEL), lambda b, i: (layer * MOD_ROWS, 0, 0))


def _inproj_kernel(*refs, rope, states, n_alias):
    it = iter(refs)
    x_ref, mod_ref, gpre_ref, w_ref, wqb_ref, wkvb_ref = (next(it) for _ in range(6))
    gq_mla_ref, gkv_mla_ref, gqn_ref, gkn_ref = (next(it) for _ in range(4))
    if rope:
        cos64_ref, sin64_ref, cos32_ref, sin32_ref = (next(it) for _ in range(4))
    for _ in range(n_alias):
        next(it)
    q_ref, kv_ref, hy_ref, gate_ref = (next(it) for _ in range(4))
    if states:
        s_ckv_ref, s_kr_ref, s_gk_ref, s_gv_ref, s_dk_ref, s_dv_ref = (next(it) for _ in range(6))
        if n_alias == 0:
            for s_ref in (s_ckv_ref, s_kr_ref, s_gk_ref, s_gv_ref, s_dk_ref, s_dv_ref):
                s_ref[:, 1:] = jnp.zeros((s_ref.shape[0], s_ref.shape[1] - 1) + s_ref.shape[2:], F32)
    h_scrs = list(it)

    mod = mod_ref[0]
    shift = mod[:, 0:D_MODEL]
    scale = mod[:, D_MODEL:2 * D_MODEL]
    bb, tm = x_ref.shape[0], x_ref.shape[1]
    chunks = [(bi, slice(r0, r0 + INPROJ_CHUNK)) for bi in range(bb) for r0 in range(0, tm, INPROJ_CHUNK)]

    for (bi, rs), h_scr in zip(chunks, h_scrs):
        x = x_ref[bi, rs, :]
        ms = jnp.mean(x * x, axis=-1, keepdims=True)
        h = (x * lax.rsqrt(ms + EPS) * gpre_ref[0]) * (1.0 + scale) + shift
        h_scr[...] = h.astype(BF16)

    def proj(g):
        w = w_ref[0, :, g * MXU_W:(g + 1) * MXU_W]
        parts = [_dot(h_scr[...], w) for h_scr in h_scrs]
        return parts[0] if len(parts) == 1 else jnp.concatenate(parts, axis=0)

    def put(ref, lead, v, cols=slice(None)):
        for c, (bi, rs) in enumerate(chunks):
            ref[(bi,) + lead + (rs, cols)] = v[c * INPROJ_CHUNK:(c + 1) * INPROJ_CHUNK]

    def table(ref):
        parts = [ref[rs, :] for _, rs in chunks]
        return parts[0] if len(parts) == 1 else jnp.concatenate(parts, axis=0)

    def rope32(v):
        return _rope(v, table(cos32_ref), table(sin32_ref), 8) if rope else v

    def rope64(v):
        return _rope(v, table(cos64_ref), table(sin64_ref), 16) if rope else v

    cq = _rms_full(proj(0), gq_mla_ref[0])
    q3 = _dot(cq.astype(BF16), wqb_ref[0]) * (LOG2E * (MLA_NOPE + MLA_ROPE) ** -0.5)
    qr = rope32(q3[:, 256:384]).astype(BF16)
    put(q_ref, (0,), jnp.concatenate([q3[:, 0:128].astype(BF16), qr], axis=1))
    put(q_ref, (1,), jnp.concatenate([q3[:, 128:256].astype(BF16), qr], axis=1))

    z1 = proj(1)
    ckv = _rms_full(z1[:, 0:128], gkv_mla_ref[0])
    kr4 = z1[:, 128:256]
    if states:
        put(s_ckv_ref, (0,), ckv)
        put(s_kr_ref, (0,), kr4[:, 0:MLA_ROPE])
    kvp = _dot(ckv.astype(BF16), wkvb_ref[0])
    kr4 = rope32(kr4).astype(BF16)
    put(kv_ref, (0,), jnp.concatenate([kvp[:, 0:128].astype(BF16), kr4], axis=1))
    put(kv_ref, (1,), jnp.concatenate([kvp[:, 128:256].astype(BF16), kr4], axis=1))
    put(kv_ref, (2,), kvp[:, 256:512].astype(BF16))

    gq = (rope64(_rms_heads64(proj(2), gqn_ref[0])) * (LOG2E * GQA_HD ** -0.5)).astype(BF16)
    zq = jnp.zeros((gq.shape[0], LANES), BF16)
    put(q_ref, (2,), jnp.concatenate([gq[:, 0:128], zq], axis=1))
    put(q_ref, (3,), jnp.concatenate([gq[:, 128:256], zq], axis=1))
    z3 = proj(3)
    gk = _rms_heads64(z3[:, 0:128], gkn_ref[0])
    gv = z3[:, 128:256]
    if states:
        put(s_gk_ref, (0,), gk)
        put(s_gv_ref, (0,), gv)
    put(kv_ref, (3,), jnp.concatenate([rope64(gk).astype(BF16), gv.astype(BF16)], axis=1))

    put(q_ref, (4,), (rope32(proj(4)) * (LOG2E * DIFF_HD ** -0.5)).astype(BF16))
    dk = proj(5)
    dv = proj(6)
    if states:
        put(s_dk_ref, (0,), dk)
        put(s_dv_ref, (0,), dv)
    put(kv_ref, (4,), rope32(dk).astype(BF16))
    put(kv_ref, (5,), dv.astype(BF16))

    for j in range(3):
        put(hy_ref, (), proj(7 + j).astype(BF16), slice(j * MXU_W, (j + 1) * MXU_W))
    for j in range(4):
        g = proj(10 + j)
        put(gate_ref, (), (g * jax.nn.sigmoid(g)).astype(BF16), slice(j * MXU_W, (j + 1) * MXU_W))


def _inproj_call(layer, x, mod, prep, tabs, *, rope, states, tm, per_batch_mod, bb=1, prev_states=None):
    nb, n, _ = x.shape
    assert bb == 1 or not per_batch_mod
    tiles = n // tm

    def lspec(shape):
        nd = len(shape)
        return pl.BlockSpec((1,) + tuple(shape[1:]), lambda b, i: (layer,) + (0,) * (nd - 1))

    in_specs = [
        pl.BlockSpec((bb, tm, D_MODEL), lambda b, i: (b, i, 0)),
        _mod_spec(layer, per_batch_mod),
    ]
    args = [x, mod]
    for name in ("norm_pre", "w_in", "wqb", "wkvb", "mla_q_norm", "mla_kv_norm", "gqa_q_norm", "gqa_k_norm"):
        a = prep[name]
        in_specs.append(lspec(a.shape))
        args.append(a)
    if rope:
        for t in tabs:
            in_specs.append(pl.BlockSpec((tm, LANES), lambda b, i: (i, 0)))
            args.append(t)

    out_shape = [
        jax.ShapeDtypeStruct((nb, N_QBLK, n, MXU_W), BF16),
        jax.ShapeDtypeStruct((nb, N_KVBLK, n, MXU_W), BF16),
        jax.ShapeDtypeStruct((nb, n, 3 * HY_W), BF16),
        jax.ShapeDtypeStruct((nb, n, 4 * GROUP_W), BF16),
    ]
    out_specs = [
        pl.BlockSpec((bb, N_QBLK, tm, MXU_W), lambda b, i: (b, 0, i, 0)),
        pl.BlockSpec((bb, N_KVBLK, tm, MXU_W), lambda b, i: (b, 0, i, 0)),
        pl.BlockSpec((bb, tm, 3 * HY_W), lambda b, i: (b, i, 0)),
        pl.BlockSpec((bb, tm, 4 * GROUP_W), lambda b, i: (b, i, 0)),
    ]
    aliases = {}
    if states:
        for w in (MLA_KV_RANK, MLA_ROPE, 128, 128, 256, 256):
            out_shape.append(jax.ShapeDtypeStruct((nb, DEPTH, n, w), F32))
            if prev_states is None:
                out_specs.append(pl.BlockSpec((bb, DEPTH, tm, w), lambda b, i: (b, 0, i, 0)))
            else:
                out_specs.append(pl.BlockSpec((bb, 1, tm, w), lambda b, i: (b, layer, i, 0)))
        if prev_states is not None:
            for j, a in enumerate(prev_states):
                aliases[len(args)] = 4 + j
                in_specs.append(pl.BlockSpec(memory_space=pl.ANY))
                args.append(a)

    return pl.pallas_call(
        functools.partial(_inproj_kernel, rope=rope, states=states, n_alias=len(aliases)),
        grid=(nb // bb, tiles),
        in_specs=in_specs,
        out_specs=out_specs,
        out_shape=out_shape,
        scratch_shapes=[pltpu.VMEM((INPROJ_CHUNK, D_MODEL), BF16)] * (bb * tm // INPROJ_CHUNK),
        input_output_aliases=aliases,
        compiler_params=pltpu.CompilerParams(
            dimension_semantics=("parallel", "parallel"), vmem_limit_bytes=VMEM_LIMIT),
        name="inproj_rope" if rope else "inproj_ctx",
    )(*args)


def _chunk_loop(nchunk, body):
    if nchunk == 1:
        body(0)
    else:
        def wrapped(k, carry):
            body(k)
            return carry
        lax.fori_loop(0, nchunk, wrapped, 0)


def _hyena_kernel(hy_ref, gate_ref, cw_ref, cb_ref, feats_ref, fw1_ref, fb1_ref, fw2_ref, fb2_ref,
                  fw3_ref, freq_ref, decay_ref, bias_ref, cs_ref, ss_ref,
                  o_ref, fr_scr, fi_scr, work, hy_pad, *, n, bt):
    rc = min(n, HY_CHUNK)
    nchunk = n // rc
    rowc = lax.broadcasted_iota(jnp.int32, (rc, HY_W), 0)

    def rows(k):
        if nchunk == 1:
            return slice(0, rc)
        return pl.ds(pl.multiple_of(k * rc, rc), rc)

    @pl.when(pl.program_id(0) == 0)
    def _filters():
        freq = freq_ref[0]
        bias = bias_ref[0]

        def taps(k):
            ft = feats_ref[rows(k), :]
            t = ft[:, 0:1]
            h1 = jnp.sin(freq[0:1, :] * (_dot_split(ft, fw1_ref[0]) + fb1_ref[0]))
            h2 = jnp.sin(freq[1:2, :] * (_dot_split(h1, fw2_ref[0]) + fb2_ref[0]))
            h3 = _dot_split(h2, fw3_ref[0])
            dec = jnp.exp(-t * jnp.abs(decay_ref[0]))
            lag0 = (rowc + k * rc) == 0
            for o in range(2):
                hf = h3[:, o * 512:o * 512 + 256] * dec
                hb = h3[:, o * 512 + 256:o * 512 + 512] * dec
                hf = hf + jnp.where(lag0, bias[o:o + 1, :], 0.0)
                hb = jnp.where(lag0, 0.0, hb)
                work[rows(k), o * 512:o * 512 + 256] = hf.astype(BF16)
                work[rows(k), o * 512 + 256:o * 512 + 512] = hb.astype(BF16)

        _chunk_loop(nchunk, taps)

        def spectra(k):
            ft = feats_ref[rows(k), :]
            c = ft[:, HY_EMB + 7:HY_EMB + 8]
            s = ft[:, HY_EMB + 8:HY_EMB + 9]
            p = _dot(cs_ref[rows(k), :], work[:, 0:4 * HY_W])
            q = _dot(ss_ref[rows(k), :], work[:, 0:4 * HY_W])
            for o in range(2):
                pf, pb = p[:, o * 512:o * 512 + 256], p[:, o * 512 + 256:o * 512 + 512]
                qf, qb = q[:, o * 512:o * 512 + 256], q[:, o * 512 + 256:o * 512 + 512]
                fr_scr[o, rows(k), :] = (c * (pf + pb) + s * (qf + qb)) * (1.0 / n)
                fi_scr[o, rows(k), :] = (s * (pf - pb) - c * (qf - qb)) * (1.0 / n)

        _chunk_loop(nchunk, spectra)

    cw = cw_ref[0]
    cb = cb_ref[0]
    hy_pad[0:HY_HALO, :] = jnp.zeros((HY_HALO, 3 * HY_W), BF16)
    hy_pad[HY_HALO + n:2 * HY_HALO + n, :] = jnp.zeros((HY_HALO, 3 * HY_W), BF16)

    def short_conv(k, j):
        sl = slice(j * HY_W, (j + 1) * HY_W)
        if nchunk == 1:
            ext = hy_pad[:, sl].astype(F32)
        else:
            ext = hy_pad[pl.ds(pl.multiple_of(k * rc, rc), rc + 2 * HY_HALO), sl].astype(F32)
        um = pltpu.roll(ext, 1, axis=0)[HY_HALO:HY_HALO + rc]
        up = pltpu.roll(ext, rc + 2 * HY_HALO - 1, axis=0)[HY_HALO:HY_HALO + rc]
        u = ext[HY_HALO:HY_HALO + rc]
        return um * cw[0:1, sl] + u * cw[1:2, sl] + up * cw[2:3, sl] + cb[:, sl]

    zc, yrc, yic = slice(0, 256), slice(256, 512), slice(512, 768)

    def one(bi):
        hy_pad[HY_HALO:HY_HALO + n, :] = hy_ref[bi]

        def conv_v(k):
            work[rows(k), zc] = short_conv(k, 2).astype(BF16)

        _chunk_loop(nchunk, conv_v)
        for o in range(2):
            def forward(k, o=o):
                a = _dot(cs_ref[rows(k), :], work[:, zc])
                b = _dot(ss_ref[rows(k), :], work[:, zc])
                fr = fr_scr[o, rows(k), :]
                fi = fi_scr[o, rows(k), :]
                work[rows(k), yrc] = (a * fr + b * fi).astype(BF16)
                work[rows(k), yic] = (a * fi - b * fr).astype(BF16)

            def inverse(k, o=o):
                y = _dot(cs_ref[rows(k), :], work[:, yrc]) - _dot(ss_ref[rows(k), :], work[:, yic])
                z = short_conv(k, o) * y
                if o == 0:
                    work[rows(k), zc] = z.astype(BF16)
                else:
                    o_ref[bi, rows(k), :] = (z * gate_ref[bi, rows(k), :].astype(F32)).astype(BF16)

            _chunk_loop(nchunk, forward)
            _chunk_loop(nchunk, inverse)

    def group():
        wide = bt * HY_W
        rown = lax.broadcasted_iota(jnp.int32, (n, HY_W), 0)

        def conv(b, j):
            sl = slice(j * HY_W, (j + 1) * HY_W)
            u = hy_ref[b, :, sl].astype(F32)
            um = jnp.where(rown == 0, 0.0, pltpu.roll(u, 1, axis=0))
            up = jnp.where(rown == n - 1, 0.0, pltpu.roll(u, n - 1, axis=0))
            return um * cw[0:1, sl] + u * cw[1:2, sl] + up * cw[2:3, sl] + cb[:, sl]

        def cols(b, base=0):
            return slice(base + b * HY_W, base + (b + 1) * HY_W)

        for b in range(bt):
            work[:, cols(b)] = conv(b, 2).astype(BF16)
        for o in range(2):
            z = work[:, 0:wide]
            a = _dot(cs_ref[...], z)
            s = _dot(ss_ref[...], z)
            fr = jnp.concatenate([fr_scr[o]] * bt, axis=1)
            fi = jnp.concatenate([fi_scr[o]] * bt, axis=1)
            work[:, wide:2 * wide] = (a * fr + s * fi).astype(BF16)
            work[:, 2 * wide:3 * wide] = (a * fi - s * fr).astype(BF16)
            y = _dot(cs_ref[...], work[:, wide:2 * wide]) - _dot(ss_ref[...], work[:, 2 * wide:3 * wide])
            for b in range(bt):
                zb = conv(b, o) * y[:, cols(b)]
                if o == 0:
                    work[:, cols(b)] = zb.astype(BF16)
                else:
                    o_ref[b] = (zb * gate_ref[b].astype(F32)).astype(BF16)

    if bt == 1:
        one(0)
    else:
        group()


def _hyena_call(layer, hy, gates, prep, dft, *, bt):
    nb, n, _ = hy.shape
    cs, ss, feats = dft

    def lspec(shape):
        nd = len(shape)
        return pl.BlockSpec((1,) + tuple(shape[1:]), lambda b: (layer,) + (0,) * (nd - 1))

    def cspec(shape):
        nd = len(shape)
        return pl.BlockSpec(tuple(shape), lambda b: (0,) * nd, pipeline_mode=pl.Buffered(1))

    big = n >= 1024
    in_specs = [
        pl.BlockSpec((bt, n, 3 * HY_W), lambda b: (b, 0, 0)),
        pl.BlockSpec((bt, n, GROUP_W), lambda b: (b, 0, 3)),
    ]
    args = [hy, gates]
    for name in ("hy_conv_w", "hy_conv_b"):
        in_specs.append(lspec(prep[name].shape))
        args.append(prep[name])
    in_specs.append(cspec(feats.shape))
    args.append(feats)
    for name in ("hy_fw1", "hy_fb1", "hy_fw2", "hy_fb2", "hy_fw3", "hy_freq", "hy_decay", "hy_bias"):
        in_specs.append(lspec(prep[name].shape))
        args.append(prep[name])
    in_specs += [cspec(cs.shape), cspec(ss.shape)]
    args += [cs, ss]

    return pl.pallas_call(
        functools.partial(_hyena_kernel, n=n, bt=bt),
        grid=(nb // bt,),
        in_specs=in_specs,
        out_specs=pl.BlockSpec((bt, n, HY_W), lambda b: (b, 0, 0)),
        out_shape=jax.ShapeDtypeStruct((nb, n, HY_W), BF16),
        scratch_shapes=[pltpu.VMEM((2, n, HY_W), F32), pltpu.VMEM((2, n, HY_W), F32),
                        pltpu.VMEM((n, max(4, 3 * bt) * HY_W), BF16),
                        pltpu.VMEM((n + 2 * HY_HALO, 3 * HY_W), BF16)],
        compiler_params=pltpu.CompilerParams(
            dimension_semantics=("arbitrary",), vmem_limit_bytes=VMEM_LIMIT),
        name="hyena_long" if big else "hyena_short",
    )(*args)


def _attn_kernel(*refs, has_ctx, lam_init, tq, ur, early_proj):
    it = iter(refs)
    q_ref, kv_ref = next(it), next(it)
    ctx_ref = next(it) if has_ctx else None
    gate_ref, ohy_ref, x_ref, mod_ref, gpost_ref, wout_ref, subln_ref, lamp_ref = (next(it) for _ in range(8))
    y_ref = next(it)

    lane = lax.broadcasted_iota(jnp.int32, (ur, MXU_W), 1)

    def band(lo, width):
        return (lane >= lo) & (lane < lo + width)

    def scores(bi, qs, kblk):
        s_l = _dot_nt(qs, kv_ref[bi, kblk])
        s_c = _dot_nt(qs, ctx_ref[bi, kblk]) if has_ctx else None
        return s_l, s_c

    def softmax_rows(s_l, s_c, narrow=False):
        def weights(s, m):
            p = jnp.exp2(s - m)
            if narrow:
                p = p.astype(BF16)
            return p, jnp.sum(p.astype(F32), axis=-1, keepdims=True)

        m = jnp.max(s_l, axis=-1, keepdims=True)
        if has_ctx:
            m = jnp.maximum(m, jnp.max(s_c, axis=-1, keepdims=True))
            p_c, l_c = weights(s_c, m)
        p_l, l = weights(s_l, m)
        if has_ctx:
            return p_l, p_c, l + l_c
        return p_l, None, l

    def pv(bi, p_l, p_c, vblk):
        o = _dot(p_l.astype(BF16), kv_ref[bi, vblk])
        if has_ctx:
            o = o + _dot(p_c.astype(BF16), ctx_ref[bi, vblk])
        return o

    def stack(parts):
        return jnp.concatenate(parts, axis=0)

    lamp = lamp_ref[0]
    lam = (jnp.exp(jnp.sum(lamp[0:1, :] * lamp[1:2, :], axis=-1, keepdims=True))
           - jnp.exp(jnp.sum(lamp[2:3, :] * lamp[3:4, :], axis=-1, keepdims=True)) + lam_init)
    lo_half = lax.broadcasted_iota(jnp.int32, (ur, LANES), 1) < GQA_HD

    qblocks = [(bi, slice(r0, r0 + ur)) for bi in range(q_ref.shape[0]) for r0 in range(0, tq, ur)]
    units = [(qb, kind, idx) for qb in qblocks
             for kind, idx in (("mla", 0), ("mla", 1), ("gqa", 0), ("gqa", 1),
                               ("diff", 0), ("diff", 1), ("diff", 2), ("diff", 3))]

    def unit_scores(qb, kind, idx):
        bi, rs = qb
        if kind == "mla":
            qf = q_ref[bi, idx, rs, :].astype(F32)
            masks = [band(64 * j, MLA_NOPE) | band(128 + MLA_ROPE * (2 * idx + j), MLA_ROPE) for j in range(2)]
            kblk = idx
        elif kind == "gqa":
            qf = q_ref[bi, 2 + idx, rs, :].astype(F32)
            masks = [band(GQA_HD * g, GQA_HD) for g in range(2)]
            kblk = 3
        else:
            qf = q_ref[bi, 4, rs, :].astype(F32)
            masks = [band(64 * idx + DIFF_HD * s, DIFF_HD) for s in range(2)]
            kblk = 4
        return scores(bi, stack([jnp.where(m, qf, 0.0) for m in masks]).astype(BF16), kblk)

    def unit_softmax(kind, s_l, s_c):
        if kind == "diff":
            p1_l, p1_c, l1 = softmax_rows(s_l[0:ur], s_c[0:ur] if has_ctx else None)
            p2_l, p2_c, l2 = softmax_rows(s_l[ur:2 * ur], s_c[ur:2 * ur] if has_ctx else None)
            w1 = 1.0 / l1
            w2 = lam / l2
            a_l = (p1_l * w1 - p2_l * w2).astype(BF16)
            a_c = (p1_c * w1 - p2_c * w2).astype(BF16) if has_ctx else None
            return a_l, a_c, None
        return softmax_rows(s_l, s_c, narrow=True)

    def unit_pv(bi, kind, idx, p_l, p_c, l):
        if kind == "diff":
            return jnp.where(band(DIFF_VD * idx, DIFF_VD), pv(bi, p_l, p_c, 5), 0.0)
        if kind == "mla":
            o = pv(bi, p_l, p_c, 2) / l
            return (jnp.where(band(128 * idx, MLA_V), o[0:ur], 0.0)
                    + jnp.where(band(128 * idx + MLA_V, MLA_V), o[ur:2 * ur], 0.0))
        o = (pv(bi, p_l, p_c, 3) / l)[:, 128:256]
        part = jnp.where(lo_half, o[0:ur], o[ur:2 * ur])
        zero = jnp.zeros_like(part)
        return jnp.concatenate([part, zero] if idx == 0 else [zero, part], axis=1)

    mixer_cols = {"mla": 0, "gqa": GROUP_W, "diff": 2 * GROUP_W}
    units_left = {(k, kind): n for k in range(len(qblocks)) for kind, n in (("mla", 2), ("gqa", 2), ("diff", 4))}
    gated, out_acc = {}, {}

    def project(k, v, lo):
        c = _dot(v, wout_ref[0, lo:lo + v.shape[1], :])
        out_acc[k] = c if k not in out_acc else out_acc[k] + c

    def mixer_done(k, kind, o):
        bi, rs = qblocks[k]
        lo = mixer_cols[kind]
        if kind == "diff":
            o2 = o * o
            r = jnp.zeros_like(o)
            for h in range(DIFF_HEADS):
                mh = band(DIFF_VD * h, DIFF_VD)
                sh = jnp.sum(jnp.where(mh, o2, 0.0), axis=-1, keepdims=True)
                r = jnp.where(mh, lax.rsqrt(sh * (1.0 / DIFF_VD) + EPS), r)
            o = (o * r * subln_ref[0]) * (1.0 - lam_init)
        piece = (o * gate_ref[bi, rs, lo:lo + GROUP_W].astype(F32)).astype(BF16)
        if early_proj:
            project(k, piece, lo)
        else:
            gated[k, kind] = piece
        if kind == "diff" and early_proj:
            project(k, ohy_ref[bi, rs, :], 3 * GROUP_W)
            finish(k, out_acc.pop(k))

    def finish(k, out):
        bi, rs = qblocks[k]
        ms = jnp.mean(out * out, axis=-1, keepdims=True)
        gate_mod = mod_ref[0][:, 2 * D_MODEL:3 * D_MODEL]
        y_ref[bi, rs, :] = x_ref[bi, rs, :] + gate_mod * (out * lax.rsqrt(ms + EPS) * gpost_ref[0])

    acc = {}
    sc_vals, sm_vals = {}, {}
    for t in range(len(units) + ATT_SCORE_LEAD + ATT_PV_LAG):
        if t < len(units):
            sc_vals[t] = unit_scores(*units[t])
        u = t - ATT_SCORE_LEAD
        if 0 <= u < len(units):
            sm_vals[u] = unit_softmax(units[u][1], *sc_vals.pop(u))
        u = t - ATT_SCORE_LEAD - ATT_PV_LAG
        if u >= 0:
            qb, kind, idx = units[u]
            c = unit_pv(qb[0], kind, idx, *sm_vals.pop(u))
            key = (qblocks.index(qb), kind)
            acc[key] = c if key not in acc else acc[key] + c
            units_left[key] -= 1
            if units_left[key] == 0:
                mixer_done(key[0], kind, acc.pop(key))

    if not early_proj:
        rows = [jnp.concatenate([gated[k, "mla"], gated[k, "gqa"], gated[k, "diff"], ohy_ref[bi, rs, :]], axis=1)
                for k, (bi, rs) in enumerate(qblocks)]
        out = _dot(rows[0] if len(rows) == 1 else jnp.concatenate(rows, axis=0), wout_ref[0])
        for k in range(len(qblocks)):
            finish(k, out[k * ur:(k + 1) * ur])


def _attn_call(layer, q, kv, ctx, gates, ohy, x, mod, prep, *, tq, ur, per_batch_mod, early_proj, bb=1):
    nb, n, _ = x.shape
    has_ctx = ctx is not None
    assert bb == 1 or not (has_ctx or per_batch_mod)
    lam_init = 0.8 - 0.6 * math.exp(-0.3 * layer)

    def lspec(shape):
        nd = len(shape)
        return pl.BlockSpec((1,) + tuple(shape[1:]), lambda b, i: (layer,) + (0,) * (nd - 1))

    in_specs = [
        pl.BlockSpec((bb, N_QBLK, tq, MXU_W), lambda b, i: (b, 0, i, 0)),
        pl.BlockSpec((bb, N_KVBLK, n, MXU_W), lambda b, i: (b, 0, 0, 0)),
    ]
    args = [q, kv]
    if has_ctx:
        p = ctx.shape[3]
        in_specs.append(pl.BlockSpec((None, 1, N_KVBLK, p, MXU_W), lambda b, i: (layer, b, 0, 0, 0)))
        args.append(ctx)
    in_specs += [
        pl.BlockSpec((bb, tq, 4 * GROUP_W), lambda b, i: (b, i, 0)),
        pl.BlockSpec((bb, tq, HY_W), lambda b, i: (b, i, 0)),
        pl.BlockSpec((bb, tq, D_MODEL), lambda b, i: (b, i, 0)),
        _mod_spec(layer, per_batch_mod),
    ]
    args += [gates, ohy, x, mod]
    for name in ("norm_post", "w_out", "diff_subln", "diff_lam"):
        in_specs.append(lspec(prep[name].shape))
        args.append(prep[name])

    return pl.pallas_call(
        functools.partial(_attn_kernel, has_ctx=has_ctx, lam_init=lam_init, tq=tq, ur=ur, early_proj=early_proj),
        grid=(nb // bb, n // tq),
        in_specs=in_specs,
        out_specs=pl.BlockSpec((bb, tq, D_MODEL), lambda b, i: (b, i, 0)),
        out_shape=jax.ShapeDtypeStruct((nb, n, D_MODEL), F32),
        compiler_params=pltpu.CompilerParams(
            dimension_semantics=("parallel", "parallel"), vmem_limit_bytes=VMEM_LIMIT),
        name="attn_lat" if has_ctx else "attn_ctx",
    )(*args)


def _rope_tables(n):
    tok = jnp.arange(n, dtype=jnp.int32)
    row = (tok // GRID_W).astype(F32)
    col = (tok % GRID_W).astype(F32)
    lane = jnp.arange(LANES, dtype=jnp.int32)
    tabs = []
    for d in (64, 32):
        m = d // 2
        i = lane % d
        ii = i % m
        f = ii % (m // 2)
        inv = ROPE_BASE ** (-(2 * f).astype(F32) / m)
        pos = jnp.where((i // m)[None, :] == 0, row[:, None], col[:, None])
        ang = pos * inv[None, :]
        sign = jnp.where(ii < m // 2, -1.0, 1.0).astype(F32)
        tabs += [jnp.cos(ang), jnp.sin(ang) * sign[None, :]]
    return tabs


def _dft_tables(n):
    k = jnp.arange(n, dtype=jnp.int32)
    unit = math.pi / (4 * n)
    kk = 2 * k[:, None] + 1
    ang_a = ((kk * (2 * LANES * jnp.arange(n // LANES, dtype=jnp.int32))[None, :]) % (8 * n)).astype(F32) * unit
    ang_b = ((kk * (2 * jnp.arange(LANES, dtype=jnp.int32) + 1)[None, :]) % (8 * n)).astype(F32) * unit
    ca, sa = jnp.cos(ang_a)[:, :, None], jnp.sin(ang_a)[:, :, None]
    cb, sb = jnp.cos(ang_b)[:, None, :], jnp.sin(ang_b)[:, None, :]
    cs = (ca * cb - sa * sb).reshape(n, n).astype(BF16)
    ss = (sa * cb + ca * sb).reshape(n, n).astype(BF16)
    th = (2 * k + 1).astype(F32) * (math.pi / (4 * n))
    ch = jnp.cos(th)[:, None]
    sh = jnp.sin(th)[:, None]
    t = jnp.linspace(0.0, 1.0, n, dtype=F32)[:, None]
    w = 2.0 * math.pi * jnp.arange(n, dtype=F32)[:, None] / n
    f = jnp.linspace(1e-4, HY_BANDS - 1, HY_BANDS, dtype=F32)[None, :]
    feats = jnp.concatenate([t, jnp.cos(f * w), -jnp.sin(f * w), jnp.zeros((n, 7), F32), ch, sh], axis=-1)
    feats = jnp.pad(feats, ((0, 0), (0, LANES - feats.shape[1])))
    return cs, ss, feats


def _prep_weights(p):
    offs = [0]
    for s in SPLIT_SIZES:
        offs.append(offs[-1] + s)
    (o_cq, o_ckv, o_kr, o_mg, o_gq, o_gk, o_gv, o_gg, o_dq, o_dk, o_dv, o_dg, o_hy, o_hg) = offs[:-1]
    w = p["w_in"]

    def cols(o, n):
        return w[:, :, o:o + n]

    def perm_heads(a, axis):
        parts = jnp.split(a, 4, axis=axis)
        return jnp.concatenate([parts[0], parts[2], parts[1], parts[3]], axis=axis)

    kr = cols(o_kr, 32)
    w_in = jnp.concatenate([
        cols(o_cq, 256), cols(o_ckv, 128), kr, kr, kr, kr,
        perm_heads(cols(o_gq, 256), 2), cols(o_gk, 128), cols(o_gv, 128),
        cols(o_dq, 256), cols(o_dk, 256), cols(o_dv, 256), cols(o_hy, 768),
        cols(o_mg, 256), perm_heads(cols(o_gg, 256), 2), cols(o_dg, 256), cols(o_hg, 256),
    ], axis=2).astype(BF16)

    wq = p["mla_wq_b"].reshape(DEPTH, MLA_Q_RANK, MLA_HEADS, MLA_NOPE + MLA_ROPE)
    wqb = jnp.concatenate([wq[..., :MLA_NOPE].reshape(DEPTH, MLA_Q_RANK, 256),
                           wq[..., MLA_NOPE:].reshape(DEPTH, MLA_Q_RANK, 128)], axis=2).astype(BF16)
    wk = p["mla_wkv_b"].reshape(DEPTH, MLA_KV_RANK, MLA_HEADS, MLA_NOPE + MLA_V)
    wkvb = jnp.concatenate([wk[..., :MLA_NOPE].reshape(DEPTH, MLA_KV_RANK, 256),
                            wk[..., MLA_NOPE:].reshape(DEPTH, MLA_KV_RANK, 256)], axis=2).astype(BF16)

    wo = p["w_out"]
    w_out = jnp.concatenate([wo[:, 0:256], perm_heads(wo[:, 256:512], 1), wo[:, 512:]], axis=1).astype(BF16)

    def pad_to(a, shape):
        return jnp.pad(a, [(0, s - d) for d, s in zip(a.shape, shape)])

    return {
        "norm_pre": p["norm_pre"].reshape(DEPTH, 1, D_MODEL),
        "norm_post": p["norm_post"].reshape(DEPTH, 1, D_MODEL),
        "w_in": w_in, "wqb": wqb, "wkvb": wkvb, "w_out": w_out,
        "mla_q_norm": p["mla_q_norm"].reshape(DEPTH, 1, MLA_Q_RANK),
        "mla_kv_norm": p["mla_kv_norm"].reshape(DEPTH, 1, MLA_KV_RANK),
        "gqa_q_norm": jnp.tile(p["gqa_q_norm"], (1, 4)).reshape(DEPTH, 1, 256),
        "gqa_k_norm": jnp.tile(p["gqa_k_norm"], (1, 2)).reshape(DEPTH, 1, 128),
        "diff_subln": jnp.tile(p["diff_subln"], (1, 4)).reshape(DEPTH, 1, 256),
        "diff_lam": jnp.stack([p["diff_lq1"], p["diff_lk1"], p["diff_lq2"], p["diff_lk2"]], axis=1),
        "hy_conv_w": p["hy_conv_w"],
        "hy_conv_b": p["hy_conv_b"].reshape(DEPTH, 1, 3 * HY_W),
        "hy_fw1": pad_to(p["hy_fw1"], (DEPTH, LANES, LANES)),
        "hy_fb1": pad_to(p["hy_fb1"].reshape(DEPTH, 1, HY_FFN), (DEPTH, 1, LANES)),
        "hy_fw2": pad_to(p["hy_fw2"], (DEPTH, LANES, LANES)),
        "hy_fb2": pad_to(p["hy_fb2"].reshape(DEPTH, 1, HY_FFN), (DEPTH, 1, LANES)),
        "hy_fw3": pad_to(p["hy_fw3"], (DEPTH, LANES, 4 * HY_W)),
        "hy_freq": pad_to(p["hy_freq"], (DEPTH, 2, LANES)),
        "hy_decay": p["hy_decay"].reshape(DEPTH, 1, HY_W),
        "hy_bias": p["hy_bias"],
    }


def kernel(x_prompt, x_sample, cache_mla_ckv, cache_mla_krope, cache_gqa_k, cache_gqa_v, cache_diff_k, cache_diff_v, c, c_ctx, norm_pre, norm_post, ada_w, ada_b, w_in, w_out, mla_q_norm, mla_wq_b, mla_kv_norm, mla_wkv_b, gqa_q_norm, gqa_k_norm, diff_lq1, diff_lk1, diff_lq2, diff_lk2, diff_subln, hy_conv_w, hy_conv_b, hy_fw1, hy_fb1, hy_fw2, hy_fb2, hy_fw3, hy_freq, hy_decay, hy_bias):
    nb_p, n_p, _ = x_prompt.shape
    nb_s, n_s, _ = x_sample.shape
    past = cache_mla_ckv.shape[2]

    prep = _prep_weights(dict(
        norm_pre=norm_pre, norm_post=norm_post, w_in=w_in, w_out=w_out, mla_q_norm=mla_q_norm,
        mla_wq_b=mla_wq_b, mla_kv_norm=mla_kv_norm, mla_wkv_b=mla_wkv_b, gqa_q_norm=gqa_q_norm,
        gqa_k_norm=gqa_k_norm, diff_lq1=diff_lq1, diff_lk1=diff_lk1, diff_lq2=diff_lq2, diff_lk2=diff_lk2,
        diff_subln=diff_subln, hy_conv_w=hy_conv_w, hy_conv_b=hy_conv_b, hy_fw1=hy_fw1, hy_fb1=hy_fb1,
        hy_fw2=hy_fw2, hy_fb2=hy_fb2, hy_fw3=hy_fw3, hy_freq=hy_freq, hy_decay=hy_decay, hy_bias=hy_bias))
    tabs = _rope_tables(n_s)
    dft_p = _dft_tables(n_p)
    dft_s = _dft_tables(n_s)

    cc8 = jnp.concatenate([c_ctx[None, :], c, jnp.zeros((MOD_ROWS - 1 - nb_s, D_MODEL), F32)], axis=0)
    mod = _ada_call(cc8, ada_w, ada_b).reshape(DEPTH * MOD_ROWS, 1, 3 * D_MODEL)

    ctx = _ctx_call(
        cache_mla_ckv, jnp.tile(cache_mla_krope, (1, 1, 1, 4)),
        cache_gqa_k.reshape(nb_s, DEPTH, past, 128), cache_gqa_v.reshape(nb_s, DEPTH, past, 128),
        cache_diff_k.reshape(nb_s, DEPTH, past, 256), cache_diff_v.reshape(nb_s, DEPTH, past, 256),
        prep["wkvb"])

    y_p, y_s = x_prompt, x_sample
    st = None
    for i in range(DEPTH):
        outs = _inproj_call(i, y_p, mod, prep, None, rope=False, states=True, tm=n_p, per_batch_mod=False, bb=4,
                            prev_states=st)
        q, kv, hy, gates = outs[:4]
        st = outs[4:]
        ohy = _hyena_call(i, hy, gates, prep, dft_p, bt=4)
        y_p = _attn_call(i, q, kv, None, gates, ohy, y_p, mod, prep, tq=n_p, ur=n_p, per_batch_mod=False,
                         early_proj=True, bb=4)
        q, kv, hy, gates = _inproj_call(i, y_s, mod, prep, tabs, rope=True, states=False, tm=1024, per_batch_mod=True)
        ohy = _hyena_call(i, hy, gates, prep, dft_s, bt=1)
        y_s = _attn_call(i, q, kv, ctx, gates, ohy, y_s, mod, prep, tq=256, ur=128, per_batch_mod=True,
                         early_proj=False)

    new_ckv, new_kr = st[0], st[1]
    new_gk = st[2].reshape(nb_p, DEPTH, n_p, GQA_KV_HEADS, GQA_HD)
    new_gv = st[3].reshape(nb_p, DEPTH, n_p, GQA_KV_HEADS, GQA_HD)
    new_dk = st[4].reshape(nb_p, DEPTH, n_p, DIFF_HEADS, 2 * DIFF_HD)
    new_dv = st[5].reshape(nb_p, DEPTH, n_p, DIFF_HEADS, DIFF_VD)
    return (y_p, y_s, new_ckv, new_kr, new_gk, new_gv, new_dk, new_dv)
```
